```python
import functools
import jax, jax.numpy as jnp
from jax import lax
import numpy as np

D_MODEL = 2048
BATCH = 1
SEQ = 8192
DEPTH = 1
DEC_BATCH = 32
DEC_SEQ = 4
PAST_LEN = 8192
PAGE_SIZE = 128

M_WIDTH = D_MODEL // 2
M_HEADS = 4
M_DH = M_WIDTH // M_HEADS
CONV_W = 4
M_CHUNK = 64
A_WIDTH = D_MODEL // 2
A_DH = 128
A_HEADS = A_WIDTH // A_DH
IDX_HEADS = 8
IDX_DIM = 64
TOPK_MAX = 256
Q_BLOCK = 128
N_GROUPS = 4
EXP_PER_GROUP = 8
N_EXPERTS = N_GROUPS * EXP_PER_GROUP
TOP_E = 2
D_EXPERT = D_MODEL // 4
RMS_EPS = 1e-6
IN_SIZES = (M_WIDTH, M_WIDTH, M_WIDTH, M_WIDTH, M_HEADS, M_HEADS, A_WIDTH, A_WIDTH, A_WIDTH, IDX_HEADS * IDX_DIM, IDX_DIM, IDX_HEADS, D_MODEL, D_MODEL)
IN_SPLITS = tuple(int(s) for s in np.cumsum(IN_SIZES)[:-1])
D_IN = int(sum(IN_SIZES))

kernel_name = 'hybrid_mlstm_dsa_hmoe_step'


def rmsnorm(x, g):
    x32 = x.astype(jnp.float32)
    y = x32 * lax.rsqrt(jnp.mean(x32 * x32, axis=-1, keepdims=True) + RMS_EPS)
    return y.astype(x.dtype) * g


def causal_conv(x, buf, w, b):
    T = x.shape[1]
    xp = jnp.concatenate([buf.astype(x.dtype), x], axis=1)
    y = b
    for j in range(CONV_W):
        y = y + xp[:, j:j + T] * w[j]
    return y, xp[:, -(CONV_W - 1):]


def mlstm_chunkwise(q, k, v, ig, fg, C0, n0, m0):
    B, T = q.shape[:2]
    L = M_CHUNK if T % M_CHUNK == 0 else T
    nc = T // L
    f32 = jnp.float32

    def to_chunks(a):
        a = a.astype(f32).reshape((B, nc, L) + a.shape[2:])
        return jnp.moveaxis(a, (1, 2), (0, 3))

    qc = to_chunks(q) * (M_DH ** -0.5)
    kc, vc, ic = to_chunks(k), to_chunks(v), to_chunks(ig)
    lfc = jax.nn.log_sigmoid(to_chunks(fg))
    causal = jnp.tril(jnp.ones((L, L), dtype=bool))

    def step(carry, xs):
        C, n, m = carry
        qb, kb, vb, ib, lfb = xs
        b = jnp.cumsum(lfb, axis=-1)
        D = jnp.where(causal, b[..., :, None] - b[..., None, :] + ib[..., None, :], -jnp.inf)
        m_t = jnp.maximum(b + m[..., None], jnp.max(D, axis=-1))
        S = jnp.einsum('bhtd,bhsd->bhts', qb, kb) * jnp.exp(D - m_t[..., None])
        inter = jnp.exp(b + m[..., None] - m_t)
        num = jnp.einsum('bhts,bhsd->bhtd', S, vb) + inter[..., None] * jnp.einsum('bhed,bhtd->bhte', C, qb)
        den = jnp.sum(S, axis=-1) + inter * jnp.einsum('bhd,bhtd->bht', n, qb)
        h = num / jnp.maximum(jnp.abs(den), jnp.exp(-m_t))[..., None]
        m_new = m_t[..., -1]
        w_s = jnp.exp(b[..., -1:] - b + ib - m_new[..., None])
        decay = jnp.exp(b[..., -1] + m - m_new)
        C = decay[..., None, None] * C + jnp.einsum('bhs,bhse,bhsd->bhed', w_s, vb, kb)
        n = decay[..., None] * n + jnp.einsum('bhs,bhsd->bhd', w_s, kb)
        return (C, n, m_new), h

    (C, n, m), h = lax.scan(step, (C0.astype(f32), n0.astype(f32), m0.astype(f32)), (qc, kc, vc, ic, lfc))
    h = jnp.moveaxis(h, (0, 3), (1, 2)).reshape(B, T, M_HEADS, M_DH)
    return h, C, n, m


def indexer_scores(q_idx, w_idx, k_idx):
    s = jnp.einsum('bthe,bse->bths', q_idx.astype(jnp.float32), k_idx.astype(jnp.float32)) * (IDX_DIM ** -0.5)
    return jnp.einsum('bth,bths->bts', w_idx.astype(jnp.float32) * (IDX_HEADS ** -0.5), jax.nn.relu(s))


def sparse_attend(q, kg, vg, valid):
    logits = jnp.einsum('bthd,btkhd->bthk', q.astype(jnp.float32), kg.astype(jnp.float32)) * (A_DH ** -0.5)
    logits = jnp.where(valid[:, :, None, :], logits, -jnp.inf)
    p = jax.nn.softmax(logits, axis=-1)
    return jnp.einsum('bthk,btkhd->bthd', p, vg.astype(jnp.float32)).astype(q.dtype)


_take_rows = jax.vmap(lambda a, i: a[i])


def dsa_prompt(q, k, v, q_idx, k_idx, w_idx):
    B, S = q.shape[:2]
    topk = min(TOPK_MAX, S // 4)
    n_blk = S // Q_BLOCK
    key_pos = jnp.arange(S)

    def blk(i):
        t0 = i * Q_BLOCK
        sl = lambda a: lax.dynamic_slice_in_dim(a, t0, Q_BLOCK, axis=1)
        pos = t0 + jnp.arange(Q_BLOCK)
        visible = key_pos[None, :] <= pos[:, None]
        scores = jnp.where(visible[None], indexer_scores(sl(q_idx), sl(w_idx), k_idx), -jnp.inf)
        _, idx = lax.top_k(scores, topk)
        valid = idx <= pos[None, :, None]
        return sparse_attend(sl(q), _take_rows(k, idx), _take_rows(v, idx), valid)

    out = lax.map(blk, jnp.arange(n_blk))
    return jnp.moveaxis(out, 0, 1).reshape(B, S, A_WIDTH)


def dsa_sample(q, k_new, v_new, q_idx, k_idx_new, w_idx, cache_k, cache_v, cache_kidx, page_table):
    B, T = q.shape[:2]
    P = page_table.shape[1] * PAGE_SIZE
    L = P + T
    topk = min(TOPK_MAX, L // 4)
    k_idx_past = cache_kidx[page_table].reshape(B, P, IDX_DIM)
    k_idx_all = jnp.concatenate([k_idx_past.astype(k_idx_new.dtype), k_idx_new], axis=1)
    pos = P + jnp.arange(T)
    visible = jnp.arange(L)[None, :] <= pos[:, None]
    scores = jnp.where(visible[None], indexer_scores(q_idx, w_idx, k_idx_all), -jnp.inf)
    _, idx = lax.top_k(scores, topk)
    in_past = idx < P
    ip = jnp.minimum(idx, P - 1)
    phys = page_table[jnp.arange(B)[:, None, None], ip // PAGE_SIZE]
    off = ip % PAGE_SIZE
    inew = jnp.clip(idx - P, 0, T - 1)
    kg = jnp.where(in_past[..., None, None], cache_k[phys, off].astype(k_new.dtype), _take_rows(k_new, inew))
    vg = jnp.where(in_past[..., None, None], cache_v[phys, off].astype(v_new.dtype), _take_rows(v_new, inew))
    valid = idx <= pos[None, :, None]
    return sparse_attend(q, kg, vg, valid).reshape(B, T, A_WIDTH)


def hier_moe(x, w_rg, b_rg, w_re, b_re, w_gate, w_up, w_down):
    B, T = x.shape[:2]
    f32 = jnp.float32
    pg = jax.nn.softmax((x @ w_rg).astype(f32) + b_rg, axis=-1)
    g_sel = jnp.argmax(pg, axis=-1)
    p_g = jnp.max(pg, axis=-1)
    le = ((x @ w_re).astype(f32) + b_re).reshape(B, T, N_GROUPS, EXP_PER_GROUP)
    le_sel = jnp.take_along_axis(le, g_sel[..., None, None], axis=2)[..., 0, :]
    top_p, top_e = lax.top_k(jax.nn.softmax(le_sel, axis=-1), TOP_E)
    top_p = top_p / jnp.sum(top_p, axis=-1, keepdims=True)
    expert_id = g_sel[..., None] * EXP_PER_GROUP + top_e
    gates = jnp.sum(jax.nn.one_hot(expert_id, N_EXPERTS, dtype=f32) * (p_g[..., None] * top_p)[..., None], axis=-2)
    h = jax.nn.silu(jnp.einsum('btd,edf->btef', x, w_gate)) * jnp.einsum('btd,edf->btef', x, w_up)
    h = h * gates[..., None].astype(x.dtype)
    return jnp.einsum('btef,efd->btd', h, w_down)


def trunk_layer(x, conv_buf, C0, n0, m0, attend, g_attn, w_in, b_gates_m, conv_w, conv_b, m_norm_w,
                w_proj_m, w_proj_a, w_out, g_ffn, w_rg, b_rg, w_re, b_re, w_gate, w_up, w_down):
    B, T = x.shape[:2]
    xn = rmsnorm(x, g_attn)
    (q_m, k_m, v_m, o_m, i_m, f_m, q_a, k_a, v_a, q_i, k_i, w_i, g_m, g_a) = jnp.split(xn @ w_in, IN_SPLITS, axis=-1)
    qk, conv_new = causal_conv(jnp.concatenate([q_m, k_m], axis=-1), conv_buf, conv_w, conv_b)
    qk = jax.nn.silu(qk)
    heads_m = lambda a: a.reshape(B, T, M_HEADS, M_DH)
    h, C, n, m = mlstm_chunkwise(heads_m(qk[..., :M_WIDTH]), heads_m(qk[..., M_WIDTH:]), heads_m(v_m),
                                 i_m + b_gates_m[:M_HEADS], f_m + b_gates_m[M_HEADS:], C0, n0, m0)
    h = h * lax.rsqrt(jnp.mean(h * h, axis=-1, keepdims=True) + RMS_EPS)
    h_m = jax.nn.sigmoid(o_m) * (h.reshape(B, T, M_WIDTH).astype(x.dtype) * m_norm_w)
    heads_a = lambda a: a.reshape(B, T, A_HEADS, A_DH)
    k = heads_a(k_a)
    v = heads_a(v_a)
    h_a = attend(heads_a(q_a), k, v, q_i.reshape(B, T, IDX_HEADS, IDX_DIM), k_i, w_i)
    merged = jax.nn.sigmoid(g_m) * (h_m @ w_proj_m) + jax.nn.sigmoid(g_a) * (h_a @ w_proj_a)
    x = x + merged @ w_out
    x = x + hier_moe(rmsnorm(x, g_ffn), w_rg, b_rg, w_re, b_re, w_gate, w_up, w_down)
    return x, (k, v, k_i, conv_new, C, n, m)


def setup_inputs(seed: int = 0) -> dict:
    key = jax.random.key(seed)
    ks = jax.random.split(key, 32)
    f32 = jnp.float32
    nrm = lambda k, shape, s: jax.random.normal(k, shape, f32) * s
    n_pages = PAST_LEN // PAGE_SIZE
    n_used = DEC_BATCH * n_pages
    n_pool = n_used + max(1, n_used // 4)
    page_table = jax.random.permutation(ks[0], n_pool)[:n_used].reshape(DEC_BATCH, n_pages).astype(jnp.int32)
    b_gates_m = jnp.concatenate([nrm(ks[1], (DEPTH, M_HEADS), 0.1),
                                 jnp.linspace(3.0, 6.0, M_HEADS, dtype=f32)[None, :] + nrm(ks[2], (DEPTH, M_HEADS), 0.1)], axis=-1)
    return {
        'x_prompt': nrm(ks[3], (BATCH, SEQ, D_MODEL), 1.0),
        'x_sample': nrm(ks[4], (DEC_BATCH, DEC_SEQ, D_MODEL), 1.0),
        'cache_k': nrm(ks[5], (DEPTH, n_pool, PAGE_SIZE, A_HEADS, A_DH), 1.0),
        'cache_v': nrm(ks[6], (DEPTH, n_pool, PAGE_SIZE, A_HEADS, A_DH), 1.0),
        'cache_kidx': nrm(ks[7], (DEPTH, n_pool, PAGE_SIZE, IDX_DIM), 1.0),
        'state_conv': nrm(ks[8], (DEPTH, DEC_BATCH, CONV_W - 1, 2 * M_WIDTH), 1.0),
        'state_C': nrm(ks[9], (DEPTH, DEC_BATCH, M_HEADS, M_DH, M_DH), 0.5),
        'state_n': nrm(ks[10], (DEPTH, DEC_BATCH, M_HEADS, M_DH), 0.5),
        'state_m': nrm(ks[11], (DEPTH, DEC_BATCH, M_HEADS), 0.5),
        'page_table': page_table,
        'g_attn': 1.0 + nrm(ks[12], (DEPTH, D_MODEL), 0.01),
        'w_in': nrm(ks[13], (DEPTH, D_MODEL, D_IN), D_MODEL ** -0.5),
        'b_gates_m': b_gates_m,
        'conv_w': nrm(ks[14], (DEPTH, CONV_W, 2 * M_WIDTH), CONV_W ** -0.5),
        'conv_b': nrm(ks[15], (DEPTH, 2 * M_WIDTH), 0.01),
        'm_norm_w': 1.0 + nrm(ks[16], (DEPTH, M_WIDTH), 0.01),
        'w_proj_m': nrm(ks[17], (DEPTH, M_WIDTH, D_MODEL), M_WIDTH ** -0.5),
        'w_proj_a': nrm(ks[18], (DEPTH, A_WIDTH, D_MODEL), A_WIDTH ** -0.5),
        'w_out': nrm(ks[19], (DEPTH, D_MODEL, D_MODEL), D_MODEL ** -0.5),
        'g_ffn': 1.0 + nrm(ks[20], (DEPTH, D_MODEL), 0.01),
        'w_rg': nrm(ks[21], (DEPTH, D_MODEL, N_GROUPS), D_MODEL ** -0.5),
        'b_rg': nrm(ks[22], (DEPTH, N_GROUPS), 0.01),
        'w_re': nrm(ks[23], (DEPTH, D_MODEL, N_EXPERTS), D_MODEL ** -0.5),
        'b_re': nrm(ks[24], (DEPTH, N_EXPERTS), 0.01),
        'w_gate': nrm(ks[25], (DEPTH, N_EXPERTS, D_MODEL, D_EXPERT), D_MODEL ** -0.5),
        'w_up': nrm(ks[26], (DEPTH, N_EXPERTS, D_MODEL, D_EXPERT), D_MODEL ** -0.5),
        'w_down': nrm(ks[27], (DEPTH, N_EXPERTS, D_EXPERT, D_MODEL), D_EXPERT ** -0.5),
        'g_final': 1.0 + nrm(ks[28], (D_MODEL,), 0.01),
    }


def reference(x_prompt, x_sample, cache_k, cache_v, cache_kidx, state_conv, state_C, state_n, state_m, page_table,
              g_attn, w_in, b_gates_m, conv_w, conv_b, m_norm_w, w_proj_m, w_proj_a, w_out, g_ffn,
              w_rg, b_rg, w_re, b_re, w_gate, w_up, w_down, g_final):
    Bp = x_prompt.shape[0]
    f32 = jnp.float32
    xp, xs = x_prompt, x_sample
    st_p, st_s = [], []
    for l in range(DEPTH):
        params = (g_attn[l], w_in[l], b_gates_m[l], conv_w[l], conv_b[l], m_norm_w[l], w_proj_m[l], w_proj_a[l],
                  w_out[l], g_ffn[l], w_rg[l], b_rg[l], w_re[l], b_re[l], w_gate[l], w_up[l], w_down[l])
        xp, sp = trunk_layer(xp, jnp.zeros((Bp, CONV_W - 1, 2 * M_WIDTH), xp.dtype),
                             jnp.zeros((Bp, M_HEADS, M_DH, M_DH), f32), jnp.zeros((Bp, M_HEADS, M_DH), f32),
                             jnp.zeros((Bp, M_HEADS), f32), dsa_prompt, *params)
        attend_s = functools.partial(dsa_sample, cache_k=cache_k[l], cache_v=cache_v[l],
                                     cache_kidx=cache_kidx[l], page_table=page_table)
        xs, ss = trunk_layer(xs, state_conv[l], state_C[l], state_n[l], state_m[l], attend_s, *params)
        st_p.append(sp)
        st_s.append(ss)
    y_prompt = rmsnorm(xp, g_final)
    y_sample = rmsnorm(xs, g_final)
    k_p, v_p, ki_p, conv_p, C_p, n_p, m_p = [jnp.stack(a) for a in zip(*st_p)]
    k_s, v_s, ki_s, conv_s, C_s, n_s, m_s = [jnp.stack(a) for a in zip(*st_s)]
    return (y_prompt, y_sample, k_p, v_p, ki_p, conv_p, C_p, n_p, m_p, k_s, v_s, ki_s, conv_s, C_s, n_s, m_s)
```

```python
import functools

import jax
import jax.numpy as jnp
import numpy as np
from jax import lax
from jax.experimental import pallas as pl
from jax.experimental.pallas import tpu as pltpu

F32 = jnp.float32
BF16 = jnp.bfloat16
I32 = jnp.int32

D_MODEL = 2048
M_WIDTH = D_MODEL // 2
M_HEADS = 4
M_DH = M_WIDTH // M_HEADS
CONV_W = 4
A_WIDTH = D_MODEL // 2
A_DH = 128
A_HEADS = A_WIDTH // A_DH
IDX_HEADS = 8
IDX_DIM = 64
TOPK_MAX = 256
PAGE_SIZE = 128
N_GROUPS = 4
EXP_PER_GROUP = 8
N_EXPERTS = N_GROUPS * EXP_PER_GROUP
TOP_E = 2
D_EXPERT = D_MODEL // 4
RMS_EPS = 1e-6
IN_SIZES = (M_WIDTH, M_WIDTH, M_WIDTH, M_WIDTH, M_HEADS, M_HEADS, A_WIDTH, A_WIDTH, A_WIDTH,
            IDX_HEADS * IDX_DIM, IDX_DIM, IDX_HEADS, D_MODEL, D_MODEL)
IN_SPLITS = tuple(int(s) for s in np.cumsum(IN_SIZES)[:-1])

LANES = 128
SUBLANES = 8
VMEM_LIMIT = 56 * 1024 * 1024

C_QM, C_KM, C_VM, C_OM = 0, 1024, 2048, 3072
C_QA, C_KA, C_VA = 4096, 5120, 6144
C_GM, C_GA = 7168, 9216
C_QI = 11264
C_TAIL = 12288
T_KI, T_WI, T_IM, T_FM = 0, 64, 72, 76
D_CAT = 12800
PROJ_TN = 512

NEG = -1e30
INT_MIN = -2 ** 31


def _cparams(sem):
    return pltpu.CompilerParams(dimension_semantics=sem, vmem_limit_bytes=VMEM_LIMIT)


def _rms_kernel(x_ref, g_ref, o_ref):
    x = x_ref[...]
    y = x * lax.rsqrt(jnp.mean(x * x, axis=-1, keepdims=True) + RMS_EPS)
    o_ref[...] = (y * g_ref[...]).astype(o_ref.dtype)


def _rmsnorm(x, g, out_dtype, tm):
    m, d = x.shape
    return pl.pallas_call(
        _rms_kernel,
        grid=(m // tm,),
        in_specs=[pl.BlockSpec((tm, d), lambda i: (i, 0)), pl.BlockSpec((1, d), lambda i: (0, 0))],
        out_specs=pl.BlockSpec((tm, d), lambda i: (i, 0)),
        out_shape=jax.ShapeDtypeStruct((m, d), out_dtype),
        compiler_params=_cparams(("parallel",)),
        name="rmsnorm",
    )(x, g.reshape(1, d))


def _inproj_kernel(x_ref, w_ref, o32_ref, o16_ref):
    acc = jnp.dot(x_ref[...], w_ref[...], preferred_element_type=F32)
    o32_ref[...] = acc
    o16_ref[...] = acc.astype(BF16)


def _inproj(xn, w_cat, tm):
    m, d = xn.shape
    n = w_cat.shape[1]
    tn = PROJ_TN
    return pl.pallas_call(
        _inproj_kernel,
        grid=(n // tn, m // tm),
        in_specs=[pl.BlockSpec((tm, d), lambda j, i: (i, 0)), pl.BlockSpec((d, tn), lambda j, i: (0, j))],
        out_specs=[pl.BlockSpec((tm, tn), lambda j, i: (i, j)), pl.BlockSpec((tm, tn), lambda j, i: (i, j))],
        out_shape=[jax.ShapeDtypeStruct((m, n), F32), jax.ShapeDtypeStruct((m, n), BF16)],
        compiler_params=_cparams(("parallel", "parallel")),
        name="inproj",
    )(xn, w_cat)


def _prep_w_in(w_in):
    (q_m, k_m, v_m, o_m, i_m, f_m, q_a, k_a, v_a, q_i, k_i, w_i, g_m, g_a) = jnp.split(w_in, IN_SPLITS, axis=-1)
    d = w_in.shape[0]
    q_i = jnp.pad(q_i.reshape(d, IDX_HEADS, IDX_DIM), ((0, 0), (0, 0), (0, LANES - IDX_DIM))).reshape(d, IDX_HEADS * LANES)
    tail = jnp.concatenate([k_i, w_i, i_m, f_m], axis=-1)
    cols = [q_m, k_m, v_m, o_m, q_a, k_a, v_a, g_m, g_a, q_i, tail]
    w = jnp.concatenate(cols, axis=-1)
    return jnp.pad(w, ((0, 0), (0, D_CAT - w.shape[1]))).astype(BF16)


def _sigmoid(x):
    return 1.0 / (1.0 + jnp.exp(-x))


def _log_sigmoid(x):
    return jnp.minimum(x, 0.0) - jnp.log1p(jnp.exp(-jnp.abs(x)))


def _mlstm_kernel(q_ref, k_ref, v_ref, o_ref, tail_ref, grow_ref, cb_ref, convw_ref, convb_ref, bl_ref, bs_ref,
                  nw_ref, c0_ref, n0_ref, m0_ref,
                  h_ref, cout_ref, nout_ref, mout_ref,
                  xq_s, xk_s, c_s, n_s, m_s, vp_s, op_s, tp_s, *, L, RIN, valid):
    c = pl.program_id(1)
    nc = pl.num_programs(1)

    @pl.when(c == 0)
    def _init():
        xq_s[0:SUBLANES, :] = cb_ref[0, :, 0:M_WIDTH]
        xk_s[0:SUBLANES, :] = cb_ref[0, :, M_WIDTH:2 * M_WIDTH]
        c_s[...] = c0_ref[0]
        n_s[...] = n0_ref[0]
        m_s[...] = m0_ref[0]

    if RIN < L:
        zpad = jnp.zeros((L - RIN, M_WIDTH), F32)
        xq_s[SUBLANES + RIN:SUBLANES + L, :] = zpad
        xk_s[SUBLANES + RIN:SUBLANES + L, :] = zpad
        vp_s[RIN:L, :] = zpad
        op_s[RIN:L, :] = zpad
        tp_s[RIN:L, :] = jnp.zeros((L - RIN, LANES), F32)
    xq_s[SUBLANES:SUBLANES + RIN, :] = q_ref[0]
    xk_s[SUBLANES:SUBLANES + RIN, :] = k_ref[0]
    vp_s[0:RIN, :] = v_ref[0]
    op_s[0:RIN, :] = o_ref[0]
    tp_s[0:RIN, :] = tail_ref[0]

    def conv(xs, col0):
        w = convw_ref[:, col0:col0 + M_WIDTH]
        y = convb_ref[:, col0:col0 + M_WIDTH]
        for j in range(CONV_W):
            r0 = SUBLANES - (CONV_W - 1) + j
            y = y + xs[r0:r0 + L, :] * w[j:j + 1, :]
        return y * _sigmoid(y)

    q_all = conv(xq_s, 0) * (M_DH ** -0.5)
    k_all = conv(xk_s, M_WIDTH)
    v_all = vp_s[...]
    tail = tp_s[...] + bl_ref[...]
    grow = grow_ref[0] + bs_ref[...]

    xq_s[0:SUBLANES, :] = xq_s[L:L + SUBLANES, :]
    xk_s[0:SUBLANES, :] = xk_s[L:L + SUBLANES, :]

    tt = lax.broadcasted_iota(I32, (L, L), 0)
    ss = lax.broadcasted_iota(I32, (L, L), 1)
    causal = ss <= tt
    row_ok = lax.broadcasted_iota(I32, (L, 1), 0) < valid
    col_ok = lax.broadcasted_iota(I32, (1, L), 1) < valid

    for h in range(M_HEADS):
        hs = slice(h * M_DH, (h + 1) * M_DH)
        qh = q_all[:, hs]
        kh = k_all[:, hs]
        vh = v_all[:, hs]
        ig_c = jnp.where(row_ok, tail[:, T_IM + h:T_IM + h + 1], NEG)
        lf_c = jnp.where(row_ok, _log_sigmoid(tail[:, T_FM + h:T_FM + h + 1]), 0.0)
        ig_r = jnp.where(col_ok, grow[h:h + 1, :], NEG)
        lf_r = jnp.where(col_ok, _log_sigmoid(grow[M_HEADS + h:M_HEADS + h + 1, :]), 0.0)
        b_c = jnp.sum(jnp.where(causal, lf_r, 0.0), axis=1, keepdims=True)
        b_r = jnp.sum(jnp.where(tt <= ss, lf_c, 0.0), axis=0, keepdims=True)
        dmat = jnp.where(causal, b_c - b_r + ig_r, NEG)
        m_prev = m_s[h][:, 0:1]
        m_t = jnp.maximum(b_c + m_prev, jnp.max(dmat, axis=1, keepdims=True))
        e = jnp.exp(dmat - m_t)
        qb = qh.astype(BF16)
        kb = kh.astype(BF16)
        s = lax.dot_general(qb, kb, (((1,), (1,)), ((), ())), preferred_element_type=F32) * e
        inter = jnp.exp(b_c + m_prev - m_t)
        ch = c_s[h]
        num = jnp.dot(s.astype(BF16), vh.astype(BF16), preferred_element_type=F32) + inter * lax.dot_general(
            qb, ch.astype(BF16), (((1,), (1,)), ((), ())), preferred_element_type=F32)
        nh = n_s[h]
        den = jnp.sum(s, axis=1, keepdims=True) + inter * jnp.sum(qh * nh, axis=1, keepdims=True)
        hh = num / jnp.maximum(jnp.abs(den), jnp.exp(-m_t))
        hh = hh * lax.rsqrt(jnp.mean(hh * hh, axis=1, keepdims=True) + RMS_EPS)
        out = _sigmoid(op_s[:, hs]) * (hh * nw_ref[:, hs])
        h_ref[0, :, hs] = out[0:RIN, :].astype(h_ref.dtype)
        m_new = m_t[L - 1:L, :]
        b_last = b_c[L - 1:L, :]
        w_c = jnp.exp(b_last - b_c + ig_c - m_new)
        decay = jnp.exp(b_last + m_prev - m_new)
        upd = lax.dot_general((w_c * vh).astype(BF16), kb, (((0,), (0,)), ((), ())), preferred_element_type=F32)
        c_s[h] = decay * ch + upd
        n_s[h] = decay * nh + jnp.sum(w_c * kh, axis=0, keepdims=True)
        m_s[h] = jnp.broadcast_to(m_new, (1, LANES))

    @pl.when(c == nc - 1)
    def _fin():
        cout_ref[0] = c_s[...]
        nout_ref[0] = n_s[...]
        mout_ref[0] = m_s[...]


def _mlstm(p32, grow, convbuf, conv_w, conv_b, b_gates, m_norm_w, c0, n0, m0, *, T, L, RIN, valid):
    B = p32.shape[0]
    nc = T // RIN
    bl = jnp.zeros((1, LANES), F32).at[0, T_IM:T_IM + 2 * M_HEADS].set(b_gates)
    bs = jnp.broadcast_to(b_gates[:, None], (2 * M_HEADS, L))
    kern = functools.partial(_mlstm_kernel, L=L, RIN=RIN, valid=valid)
    cblk = lambda col: pl.BlockSpec((1, RIN, M_WIDTH), lambda b, c, col=col: (b, c, col // M_WIDTH))
    const2 = lambda shape: pl.BlockSpec(shape, lambda b, c: (0, 0))
    per_b = lambda shape: pl.BlockSpec(shape, lambda b, c: (b,) + (0,) * (len(shape) - 1))
    return pl.pallas_call(
        kern,
        grid=(B, nc),
        in_specs=[cblk(C_QM), cblk(C_KM), cblk(C_VM), cblk(C_OM),
                  pl.BlockSpec((1, RIN, LANES), lambda b, c: (b, c, C_TAIL // LANES)),
                  pl.BlockSpec((1, 2 * M_HEADS, L), lambda b, c: (b, 0, c)),
                  per_b((1, SUBLANES, 2 * M_WIDTH)),
                  const2((CONV_W, 2 * M_WIDTH)), const2((1, 2 * M_WIDTH)), const2((1, LANES)),
                  const2((2 * M_HEADS, L)), const2((1, M_WIDTH)),
                  per_b((1, M_HEADS, M_DH, M_DH)), per_b((1, M_HEADS, 1, M_DH)), per_b((1, M_HEADS, 1, LANES))],
        out_specs=[pl.BlockSpec((1, RIN, M_WIDTH), lambda b, c: (b, c, 0)),
                   per_b((1, M_HEADS, M_DH, M_DH)), per_b((1, M_HEADS, 1, M_DH)), per_b((1, M_HEADS, 1, LANES))],
        out_shape=[jax.ShapeDtypeStruct((B, T, M_WIDTH), BF16),
                   jax.ShapeDtypeStruct((B, M_HEADS, M_DH, M_DH), F32),
                   jax.ShapeDtypeStruct((B, M_HEADS, 1, M_DH), F32),
                   jax.ShapeDtypeStruct((B, M_HEADS, 1, LANES), F32)],
        scratch_shapes=[pltpu.VMEM((SUBLANES + L, M_WIDTH), F32), pltpu.VMEM((SUBLANES + L, M_WIDTH), F32),
                        pltpu.VMEM((M_HEADS, M_DH, M_DH), F32), pltpu.VMEM((M_HEADS, 1, M_DH), F32),
                        pltpu.VMEM((M_HEADS, 1, LANES), F32),
                        pltpu.VMEM((L, M_WIDTH), F32), pltpu.VMEM((L, M_WIDTH), F32), pltpu.VMEM((L, LANES), F32)],
        compiler_params=_cparams(("parallel", "arbitrary")),
        name="mlstm",
    )(p32, p32, p32, p32, p32, grow, convbuf, conv_w, conv_b.reshape(1, -1), bl, bs, m_norm_w.reshape(1, -1),
      c0, n0.reshape(B, M_HEADS, 1, M_DH), jnp.broadcast_to(m0[:, :, None, None], (B, M_HEADS, 1, LANES)))


def _score_key(sc):
    bits = lax.bitcast_convert_type(sc, I32)
    return jnp.where(bits < 0, INT_MIN - bits, bits)


DSA_TQ = 512
DSA_TK = 512
DSA_RG = 64


def _dsa_prompt_kernel(qi_tab, kj_tab, qidx_ref, tail_ref, kit_ref, qa_ref, ka_ref, va_ref, o_ref,
                       keys_s, cand_s, cnt_s, thr_s, cut_s, need_s, m_s, l_s, acc_s, *, TQ, TK, topk, pos_bits):
    step = pl.program_id(0)
    qi = qi_tab[step]
    kj = kj_tab[step]
    RG = DSA_RG
    nlc = TK // LANES

    def count_pass(pred):
        def rbody(r, _):
            r0 = pl.multiple_of(r * RG, RG)
            cand = cand_s[pl.ds(r0, RG), :]
            aux = thr_s[pl.ds(r0, RG), :]

            def kb(j, cnt):
                for c in range(nlc):
                    blk = keys_s[j, pl.ds(r0, RG), c * LANES:(c + 1) * LANES]
                    pos = j * TK + c * LANES + lax.broadcasted_iota(I32, (RG, LANES), 1)
                    cnt = cnt + jnp.where(pred(blk, cand, aux, pos), 1.0, 0.0)
                return cnt

            cnt_s[pl.ds(r0, RG), :] = lax.fori_loop(0, qi + 1, kb, jnp.zeros((RG, LANES), F32))
            return 0

        lax.fori_loop(0, TQ // RG, rbody, 0)
        return jnp.sum(cnt_s[...], axis=1, keepdims=True)

    @pl.when(kj == 0)
    def _phase1():
        w = tail_ref[:, T_WI:T_WI + IDX_HEADS] * (IDX_HEADS ** -0.5)
        rowpos = qi * TQ + lax.broadcasted_iota(I32, (TQ, TK), 0)

        def kbody(j, _):
            kt = kit_ref[j].astype(BF16)
            sc = jnp.zeros((TQ, TK), F32)
            for h in range(IDX_HEADS):
                qh = qidx_ref[:, h * LANES:(h + 1) * LANES].astype(BF16)
                s = jnp.dot(qh, kt, preferred_element_type=F32) * (IDX_DIM ** -0.5)
                sc = sc + jnp.maximum(s, 0.0) * w[:, h:h + 1]
            colpos = j * TK + lax.broadcasted_iota(I32, (TQ, TK), 1)
            keys_s[j] = jnp.where(colpos <= rowpos, _score_key(sc), INT_MIN)
            return 0

        lax.fori_loop(0, qi + 1, kbody, 0)

        ge = lambda blk, cand, aux, pos: blk >= cand
        cand_s[...] = jnp.zeros((TQ, LANES), I32)
        cnt = count_pass(ge)
        lo = jnp.where(cnt >= topk, 0, INT_MIN).astype(I32)

        def bit_body(b, lo):
            cand = lo | lax.shift_left(jnp.int32(1), 30 - b)
            cand_s[...] = jnp.broadcast_to(cand, (TQ, LANES))
            cnt = count_pass(ge)
            return jnp.where(cnt >= topk, cand, lo)

        thr = lax.fori_loop(0, 31, bit_body, lo)
        thr_s[...] = jnp.broadcast_to(thr, (TQ, LANES))
        cand_s[...] = thr_s[...]
        n_gt = count_pass(lambda blk, cand, aux, pos: blk > cand)
        n_ge = count_pass(ge)
        need = topk - n_gt
        need_s[...] = jnp.broadcast_to(need, (TQ, LANES))
        cut_s[...] = jnp.full((TQ, LANES), 2 ** 30, I32)
        excess = jnp.max(jnp.where(n_ge - n_gt > need, 1.0, 0.0))

        @pl.when(excess > 0.0)
        def _ties():
            eq_below = lambda blk, cand, aux, pos: (blk == aux) & (pos < cand)

            def tie_body(b, x):
                cand = x + lax.shift_left(jnp.int32(1), pos_bits - 1 - b)
                cand_s[...] = jnp.broadcast_to(cand, (TQ, LANES))
                cnt = count_pass(eq_below)
                return jnp.where(cnt < need, cand, x)

            x = lax.fori_loop(0, pos_bits, tie_body, jnp.zeros((TQ, 1), I32))
            cut_s[...] = jnp.broadcast_to(x, (TQ, LANES))

        m_s[...] = jnp.full(m_s.shape, NEG, F32)
        l_s[...] = jnp.zeros(l_s.shape, F32)
        acc_s[...] = jnp.zeros(acc_s.shape, F32)

    key = keys_s[kj]
    thr = thr_s[:, 0:1]
    cut = cut_s[:, 0:1]
    colpos = kj * TK + lax.broadcasted_iota(I32, (TQ, TK), 1)
    sel = (key > thr) | ((key == thr) & (colpos <= cut))
    sel = sel & (key != INT_MIN)
    bias = jnp.where(sel, 0.0, NEG)
    for h in range(A_HEADS):
        hs = slice(h * A_DH, (h + 1) * A_DH)
        s = lax.dot_general(qa_ref[:, hs], ka_ref[:, hs], (((1,), (1,)), ((), ())), preferred_element_type=F32)
        s = s * (A_DH ** -0.5) + bias
        m_old = m_s[h][:, 0:1]
        m_new = jnp.maximum(m_old, jnp.max(s, axis=1, keepdims=True))
        alpha = jnp.exp(m_old - m_new)
        p = jnp.exp(s - m_new)
        l_s[h] = jnp.broadcast_to(alpha * l_s[h][:, 0:1] + jnp.sum(p, axis=1, keepdims=True), (TQ, LANES))
        acc_s[:, hs] = alpha * acc_s[:, hs] + jnp.dot(p.astype(BF16), va_ref[:, hs], preferred_element_type=F32)
        m_s[h] = jnp.broadcast_to(m_new, (TQ, LANES))

    @pl.when(kj == qi)
    def _fin():
        for h in range(A_HEADS):
            hs = slice(h * A_DH, (h + 1) * A_DH)
            o_ref[:, hs] = (acc_s[:, hs] / l_s[h][:, 0:1]).astype(o_ref.dtype)


def _dsa_prompt(p32, p16, T, topk):
    TQ, TK = DSA_TQ, DSA_TK
    nq = T // TQ
    assert TQ == TK
    qi_tab = np.concatenate([np.full(i + 1, i) for i in range(nq)]).astype(np.int32)
    kj_tab = np.concatenate([np.arange(i + 1) for i in range(nq)]).astype(np.int32)
    ki = p32[:T, C_TAIL + T_KI:C_TAIL + T_KI + IDX_DIM]
    kit = jnp.pad(ki.T, ((0, LANES - IDX_DIM), (0, 0))).reshape(LANES, T // TK, TK).transpose(1, 0, 2)
    kern = functools.partial(_dsa_prompt_kernel, TQ=TQ, TK=TK, topk=topk, pos_bits=int(T - 1).bit_length())
    gs = pltpu.PrefetchScalarGridSpec(
        num_scalar_prefetch=2,
        grid=(len(qi_tab),),
        in_specs=[pl.BlockSpec((TQ, IDX_HEADS * LANES), lambda s, qt, kt: (qt[s], C_QI // (IDX_HEADS * LANES))),
                  pl.BlockSpec((TQ, LANES), lambda s, qt, kt: (qt[s], C_TAIL // LANES)),
                  pl.BlockSpec((T // TK, LANES, TK), lambda s, qt, kt: (0, 0, 0)),
                  pl.BlockSpec((TQ, A_WIDTH), lambda s, qt, kt: (qt[s], C_QA // A_WIDTH)),
                  pl.BlockSpec((TK, A_WIDTH), lambda s, qt, kt: (kt[s], C_KA // A_WIDTH)),
                  pl.BlockSpec((TK, A_WIDTH), lambda s, qt, kt: (kt[s], C_VA // A_WIDTH))],
        out_specs=pl.BlockSpec((TQ, A_WIDTH), lambda s, qt, kt: (qt[s], 0)),
        scratch_shapes=[pltpu.VMEM((T // TK, TQ, TK), I32),
                        pltpu.VMEM((TQ, LANES), I32), pltpu.VMEM((TQ, LANES), F32), pltpu.VMEM((TQ, LANES), I32),
                        pltpu.VMEM((TQ, LANES), I32), pltpu.VMEM((TQ, LANES), F32),
                        pltpu.VMEM((A_HEADS, TQ, LANES), F32), pltpu.VMEM((A_HEADS, TQ, LANES), F32),
                        pltpu.VMEM((TQ, A_WIDTH), F32)])
    return pl.pallas_call(
        kern, grid_spec=gs,
        out_shape=jax.ShapeDtypeStruct((T, A_WIDTH), BF16),
        compiler_params=_cparams(("arbitrary",)),
        name="dsa_prompt",
    )(jnp.asarray(qi_tab), jnp.asarray(kj_tab), p32, p32, kit, p16, p16, p16)


def _idx_scores(q, w, keys):
    s = lax.dot_general(q.astype(BF16), keys.astype(BF16), (((1,), (1,)), ((), ())), preferred_element_type=F32)
    s = jnp.maximum(s * (IDX_DIM ** -0.5), 0.0) * (w * (IDX_HEADS ** -0.5))
    n_tok = q.shape[0] // IDX_HEADS
    return jnp.concatenate([jnp.sum(s[t * IDX_HEADS:(t + 1) * IDX_HEADS], axis=0, keepdims=True) for t in range(n_tok)], axis=0)


def _smp_scores_kernel(pt_ref, q_ref, w_ref, kp_ref, o_ref):
    o_ref[0] = _idx_scores(q_ref[0], w_ref[0], kp_ref[0])


def _smp_scores(page_table, qs, ws, cache_kidx):
    B, n_pages = page_table.shape
    R = qs.shape[1]
    n_tok = R // IDX_HEADS
    gs = pltpu.PrefetchScalarGridSpec(
        num_scalar_prefetch=1,
        grid=(B, n_pages),
        in_specs=[pl.BlockSpec((1, R, IDX_DIM), lambda b, p, pt: (b, 0, 0)),
                  pl.BlockSpec((1, R, 1), lambda b, p, pt: (b, 0, 0)),
                  pl.BlockSpec((1, PAGE_SIZE, IDX_DIM), lambda b, p, pt: (pt[b, p], 0, 0))],
        out_specs=pl.BlockSpec((1, n_tok, PAGE_SIZE), lambda b, p, pt: (b, 0, p)))
    return pl.pallas_call(
        _smp_scores_kernel, grid_spec=gs,
        out_shape=jax.ShapeDtypeStruct((B, n_tok, n_pages * PAGE_SIZE), F32),
        compiler_params=_cparams(("parallel", "arbitrary")),
        name="smp_scores",
    )(page_table, qs, ws, cache_kidx)


def _smp_select_kernel(sc_ref, q_ref, w_ref, kin_ref, bp_ref, bn_ref, *, topk):
    kp = _score_key(sc_ref[0])
    n_tok, P = kp.shape
    s_new = _idx_scores(q_ref[0], w_ref[0], kin_ref[0])
    lane = lax.broadcasted_iota(I32, (n_tok, LANES), 1)
    trow = lax.broadcasted_iota(I32, (n_tok, LANES), 0)
    kn = jnp.where(lane <= trow, _score_key(s_new), INT_MIN)
    pos_p = lax.broadcasted_iota(I32, (n_tok, P), 1)
    pos_n = P + lane

    def count(fp, fn):
        return (jnp.sum(jnp.where(fp, 1.0, 0.0), axis=1, keepdims=True)
                + jnp.sum(jnp.where(fn, 1.0, 0.0), axis=1, keepdims=True))

    lo = jnp.where(count(kp >= 0, kn >= 0) >= topk, 0, INT_MIN).astype(I32)

    def bit_body(b, lo):
        cand = lo | lax.shift_left(jnp.int32(1), 30 - b)
        return jnp.where(count(kp >= cand, kn >= cand) >= topk, cand, lo)

    thr = lax.fori_loop(0, 31, bit_body, lo)
    need = topk - count(kp > thr, kn > thr)
    pos_bits = int(P + LANES - 1).bit_length()

    def tie_body(b, x):
        cand = x + lax.shift_left(jnp.int32(1), pos_bits - 1 - b)
        cnt = count((kp == thr) & (pos_p < cand), (kn == thr) & (pos_n < cand))
        return jnp.where(cnt < need, cand, x)

    cut = lax.fori_loop(0, pos_bits, tie_body, jnp.zeros((n_tok, 1), I32))
    sel_p = (kp > thr) | ((kp == thr) & (pos_p <= cut))
    sel_n = ((kn > thr) | ((kn == thr) & (pos_n <= cut))) & (kn != INT_MIN)
    bp_ref[0] = jnp.where(sel_p, 0.0, NEG)
    bn_ref[0] = jnp.where(sel_n, 0.0, NEG)


def _smp_select(sc, qs, ws, kin, topk):
    B, n_tok, P = sc.shape
    R = qs.shape[1]
    per_b = lambda shape: pl.BlockSpec(shape, lambda b: (b, 0, 0))
    return pl.pallas_call(
        functools.partial(_smp_select_kernel, topk=topk),
        grid=(B,),
        in_specs=[per_b((1, n_tok, P)), per_b((1, R, IDX_DIM)), per_b((1, R, 1)), per_b((1, LANES, IDX_DIM))],
        out_specs=[per_b((1, n_tok, P)), per_b((1, n_tok, LANES))],
        out_shape=[jax.ShapeDtypeStruct((B, n_tok, P), F32), jax.ShapeDtypeStruct((B, n_tok, LANES), F32)],
        compiler_params=_cparams(("parallel",)),
        name="smp_select",
    )(sc, qs, ws, kin)


def _smp_attn_kernel(pt_ref, q_ref, kn_ref, vn_ref, bn_ref, kp_ref, vp_ref, bp_ref, o_ref,
                     qbd_s, kpad_s, vpad_s, m_s, l_s, acc_s, *, n_tok):
    p = pl.program_id(1)
    seg = (lax.broadcasted_iota(I32, (A_HEADS, A_WIDTH), 1) // A_DH) == lax.broadcasted_iota(I32, (A_HEADS, A_WIDTH), 0)

    def attend(k, v, bias):
        n = k.shape[0]
        b32 = jnp.concatenate([jnp.broadcast_to(bias[t:t + 1, :], (A_HEADS, n)) for t in range(n_tok)], axis=0)
        s = lax.dot_general(qbd_s[...].astype(BF16), k, (((1,), (1,)), ((), ())), preferred_element_type=F32)
        s = s * (A_DH ** -0.5) + b32
        m_old = m_s[:, 0:1]
        m_new = jnp.maximum(m_old, jnp.max(s, axis=1, keepdims=True))
        alpha = jnp.exp(m_old - m_new)
        pr = jnp.exp(s - m_new)
        l_s[...] = jnp.broadcast_to(alpha * l_s[:, 0:1] + jnp.sum(pr, axis=1, keepdims=True), l_s.shape)
        acc_s[...] = alpha * acc_s[...] + jnp.dot(pr.astype(BF16), v, preferred_element_type=F32)
        m_s[...] = jnp.broadcast_to(m_new, m_s.shape)

    @pl.when(p == 0)
    def _first():
        q = q_ref[0]
        for t in range(n_tok):
            qbd_s[t * A_HEADS:(t + 1) * A_HEADS, :] = jnp.where(seg, jnp.broadcast_to(q[t:t + 1, :], (A_HEADS, A_WIDTH)), 0.0)
        m_s[...] = jnp.full(m_s.shape, NEG, F32)
        l_s[...] = jnp.zeros(l_s.shape, F32)
        acc_s[...] = jnp.zeros(acc_s.shape, F32)
        kpad_s[...] = jnp.zeros(kpad_s.shape, F32)
        vpad_s[...] = jnp.zeros(vpad_s.shape, F32)
        kpad_s[0:SUBLANES, :] = kn_ref[0]
        vpad_s[0:SUBLANES, :] = vn_ref[0]
        attend(kpad_s[...].astype(BF16), vpad_s[...].astype(BF16), bn_ref[0])

    attend(kp_ref[0].astype(BF16), vp_ref[0].astype(BF16), bp_ref[0])

    @pl.when(p == pl.num_programs(1) - 1)
    def _fin():
        for t in range(n_tok):
            rs = slice(t * A_HEADS, (t + 1) * A_HEADS)
            blk = acc_s[rs, :] / l_s[rs, 0:1]
            o_ref[0, t:t + 1, :] = jnp.sum(jnp.where(seg, blk, 0.0), axis=0, keepdims=True)


def _smp_attn(page_table, ps3, ps8, bn, ck, cv, bp):
    B, n_pages = page_table.shape
    n_tok = ps3.shape[1]
    R = n_tok * A_HEADS
    gs = pltpu.PrefetchScalarGridSpec(
        num_scalar_prefetch=1,
        grid=(B, n_pages),
        in_specs=[pl.BlockSpec((1, n_tok, A_WIDTH), lambda b, p, pt: (b, 0, C_QA // A_WIDTH)),
                  pl.BlockSpec((1, SUBLANES, A_WIDTH), lambda b, p, pt: (b, 0, C_KA // A_WIDTH)),
                  pl.BlockSpec((1, SUBLANES, A_WIDTH), lambda b, p, pt: (b, 0, C_VA // A_WIDTH)),
                  pl.BlockSpec((1, n_tok, LANES), lambda b, p, pt: (b, 0, 0)),
                  pl.BlockSpec((1, PAGE_SIZE, A_WIDTH), lambda b, p, pt: (pt[b, p], 0, 0)),
                  pl.BlockSpec((1, PAGE_SIZE, A_WIDTH), lambda b, p, pt: (pt[b, p], 0, 0)),
                  pl.BlockSpec((1, n_tok, PAGE_SIZE), lambda b, p, pt: (b, 0, p))],
        out_specs=pl.BlockSpec((1, n_tok, A_WIDTH), lambda b, p, pt: (b, 0, 0)),
        scratch_shapes=[pltpu.VMEM((R, A_WIDTH), F32), pltpu.VMEM((LANES, A_WIDTH), F32), pltpu.VMEM((LANES, A_WIDTH), F32),
                        pltpu.VMEM((R, LANES), F32), pltpu.VMEM((R, LANES), F32), pltpu.VMEM((R, A_WIDTH), F32)])
    return pl.pallas_call(
        functools.partial(_smp_attn_kernel, n_tok=n_tok), grid_spec=gs,
        out_shape=jax.ShapeDtypeStruct((B, n_tok, A_WIDTH), F32),
        compiler_params=_cparams(("parallel", "arbitrary")),
        name="smp_attn",
    )(page_table, ps3, ps8, ps8, bn, ck, cv, bp)


def _merge_kernel(hm_ref, ha_ref, wm_ref, wa_ref, gm_ref, ga_ref, o_ref):
    a = jnp.dot(hm_ref[...], wm_ref[...], preferred_element_type=F32)
    b = jnp.dot(ha_ref[...], wa_ref[...], preferred_element_type=F32)
    o_ref[...] = (_sigmoid(gm_ref[...]) * a + _sigmoid(ga_ref[...]) * b).astype(o_ref.dtype)


def _merge(hm, ha, wm, wa, p32, tm):
    m = hm.shape[0]
    tn = PROJ_TN
    return pl.pallas_call(
        _merge_kernel,
        grid=(D_MODEL // tn, m // tm),
        in_specs=[pl.BlockSpec((tm, M_WIDTH), lambda j, i: (i, 0)), pl.BlockSpec((tm, A_WIDTH), lambda j, i: (i, 0)),
                  pl.BlockSpec((M_WIDTH, tn), lambda j, i: (0, j)), pl.BlockSpec((A_WIDTH, tn), lambda j, i: (0, j)),
                  pl.BlockSpec((tm, tn), lambda j, i: (i, C_GM // tn + j)),
                  pl.BlockSpec((tm, tn), lambda j, i: (i, C_GA // tn + j))],
        out_specs=pl.BlockSpec((tm, tn), lambda j, i: (i, j)),
        out_shape=jax.ShapeDtypeStruct((m, D_MODEL), BF16),
        compiler_params=_cparams(("parallel", "parallel")),
        name="merge",
    )(hm, ha, wm, wa, p32, p32)


def _outproj_kernel(mg_ref, w_ref, x_ref, o_ref):
    o_ref[...] = x_ref[...] + jnp.dot(mg_ref[...], w_ref[...], preferred_element_type=F32)


def _outproj(mg, w, x, tm):
    m = mg.shape[0]
    tn = PROJ_TN
    return pl.pallas_call(
        _outproj_kernel,
        grid=(D_MODEL // tn, m // tm),
        in_specs=[pl.BlockSpec((tm, D_MODEL), lambda j, i: (i, 0)), pl.BlockSpec((D_MODEL, tn), lambda j, i: (0, j)),
                  pl.BlockSpec((tm, tn), lambda j, i: (i, j))],
        out_specs=pl.BlockSpec((tm, tn), lambda j, i: (i, j)),
        out_shape=jax.ShapeDtypeStruct((m, D_MODEL), F32),
        compiler_params=_cparams(("parallel", "parallel")),
        name="outproj",
    )(mg, w, x)


MOE_TM = 256


def _router_kernel(x_ref, g_ref, wr_ref, br_ref, xn_ref, r_ref):
    x = x_ref[...]
    y = (x * lax.rsqrt(jnp.mean(x * x, axis=-1, keepdims=True) + RMS_EPS)) * g_ref[...]
    xn_ref[...] = y
    lg = jnp.dot(y.astype(BF16), wr_ref[...], preferred_element_type=F32) + br_ref[...]
    lane = lax.broadcasted_iota(I32, lg.shape, 1).astype(F32)
    far = float(LANES)
    gmask = lane < N_GROUPS
    gl = jnp.where(gmask, lg, NEG)
    mg = jnp.max(gl, axis=1, keepdims=True)
    p_g = 1.0 / jnp.sum(jnp.where(gmask, jnp.exp(gl - mg), 0.0), axis=1, keepdims=True)
    g_sel = jnp.min(jnp.where(gmask & (gl == mg), lane, far), axis=1, keepdims=True)
    e_lo = N_GROUPS + g_sel * EXP_PER_GROUP
    emask = (lane >= e_lo) & (lane < e_lo + EXP_PER_GROUP)
    el = jnp.where(emask, lg, NEG)
    me = jnp.max(el, axis=1, keepdims=True)
    pe = jnp.where(emask, jnp.exp(el - me), 0.0)
    probs = pe / jnp.sum(pe, axis=1, keepdims=True)
    p1 = jnp.max(probs, axis=1, keepdims=True)
    i1 = jnp.min(jnp.where(emask & (probs == p1), lane, far), axis=1, keepdims=True)
    probs2 = jnp.where(lane == i1, -1.0, probs)
    p2 = jnp.max(probs2, axis=1, keepdims=True)
    i2 = jnp.min(jnp.where(emask & (probs2 == p2), lane, far), axis=1, keepdims=True)
    tot = p1 + p2
    vals = [i1 - N_GROUPS, i2 - N_GROUPS, p_g * (p1 / tot), p_g * (p2 / tot)]
    out = jnp.zeros(lg.shape, F32)
    for c, v in enumerate(vals):
        out = jnp.where(lane == c, v, out)
    r_ref[...] = out


def _router(x1, g, wr, br, tm):
    m, d = x1.shape
    return pl.pallas_call(
        _router_kernel,
        grid=(m // tm,),
        in_specs=[pl.BlockSpec((tm, d), lambda i: (i, 0)), pl.BlockSpec((1, d), lambda i: (0, 0)),
                  pl.BlockSpec((d, LANES), lambda i: (0, 0)), pl.BlockSpec((1, LANES), lambda i: (0, 0))],
        out_specs=[pl.BlockSpec((tm, d), lambda i: (i, 0)), pl.BlockSpec((tm, LANES), lambda i: (i, 0))],
        out_shape=[jax.ShapeDtypeStruct((m, d), F32), jax.ShapeDtypeStruct((m, LANES), F32)],
        compiler_params=_cparams(("parallel",)),
        name="router",
    )(x1, g.reshape(1, d), wr, br)


def _row_copy(src_hbm, row, dst, r, sem):
    return pltpu.make_async_copy(src_hbm.at[pl.ds(row, 1), :], dst.at[pl.ds(r, 1), :], sem)


def _expert_kernel(be_ref, na_ref, src_ref, x_hbm, gate_ref, wg_ref, wu_ref, wd_ref, o_ref, xbuf, sem):
    blk = pl.program_id(0)

    @pl.when(blk < na_ref[0])
    def _active():
        base = blk * MOE_TM

        def start(r, _):
            _row_copy(x_hbm, src_ref[base + r], xbuf, r, sem).start()
            return 0

        def wait(r, _):
            _row_copy(x_hbm, 0, xbuf, r, sem).wait()
            return 0

        lax.fori_loop(0, MOE_TM, start, 0)
        lax.fori_loop(0, MOE_TM, wait, 0)
        x = xbuf[...].astype(BF16)
        hg = jnp.dot(x, wg_ref[0].astype(BF16), preferred_element_type=F32)
        hu = jnp.dot(x, wu_ref[0].astype(BF16), preferred_element_type=F32)
        h = (hg * _sigmoid(hg)) * hu * gate_ref[...]
        o_ref[...] = jnp.dot(h.astype(BF16), wd_ref[0].astype(BF16), preferred_element_type=F32)

    @pl.when(blk >= na_ref[0])
    def _idle():
        o_ref[...] = jnp.zeros(o_ref.shape, F32)


def _experts(blk_exp, n_act, src, xn2, gate_sorted, w_gate, w_up, w_down):
    npad = src.shape[0]
    d = xn2.shape[1]
    last = lambda b, be, na, sr: jnp.minimum(b, na[0] - 1)
    gs = pltpu.PrefetchScalarGridSpec(
        num_scalar_prefetch=3,
        grid=(npad // MOE_TM,),
        in_specs=[pl.BlockSpec(memory_space=pl.ANY),
                  pl.BlockSpec((MOE_TM, 1), lambda b, be, na, sr: (last(b, be, na, sr), 0)),
                  pl.BlockSpec((1, d, D_EXPERT), lambda b, be, na, sr: (be[b], 0, 0)),
                  pl.BlockSpec((1, d, D_EXPERT), lambda b, be, na, sr: (be[b], 0, 0)),
                  pl.BlockSpec((1, D_EXPERT, d), lambda b, be, na, sr: (be[b], 0, 0))],
        out_specs=pl.BlockSpec((MOE_TM, d), lambda b, be, na, sr: (b, 0)),
        scratch_shapes=[pltpu.VMEM((MOE_TM, d), F32), pltpu.SemaphoreType.DMA(())])
    return pl.pallas_call(
        _expert_kernel, grid_spec=gs,
        out_shape=jax.ShapeDtypeStruct((npad, d), F32),
        compiler_params=_cparams(("arbitrary",)),
        name="experts",
    )(blk_exp, n_act, src, xn2, gate_sorted, w_gate, w_up, w_down)


def _combine_kernel(pos_ref, ys_hbm, x1_ref, g_ref, o_ref, buf, sem, *, TC, row0):
    base = (row0 + pl.program_id(0) * TC) * TOP_E

    def start(r, _):
        for s in range(TOP_E):
            _row_copy(ys_hbm, pos_ref[base + r * TOP_E + s], buf.at[s], r, sem).start()
        return 0

    def wait(r, _):
        for s in range(TOP_E):
            _row_copy(ys_hbm, 0, buf.at[s], r, sem).wait()
        return 0

    lax.fori_loop(0, TC, start, 0)
    lax.fori_loop(0, TC, wait, 0)
    x = x1_ref[...] + (buf[0] + buf[1])
    y = x * lax.rsqrt(jnp.mean(x * x, axis=-1, keepdims=True) + RMS_EPS)
    o_ref[...] = y * g_ref[...]


def _combine(pos, ys, x1, g, row0, n, TC):
    d = x1.shape[1]
    gs = pltpu.PrefetchScalarGridSpec(
        num_scalar_prefetch=1,
        grid=(n // TC,),
        in_specs=[pl.BlockSpec(memory_space=pl.ANY),
                  pl.BlockSpec((TC, d), lambda i, ps: (row0 // TC + i, 0)),
                  pl.BlockSpec((1, d), lambda i, ps: (0, 0))],
        out_specs=pl.BlockSpec((TC, d), lambda i, ps: (i, 0)),
        scratch_shapes=[pltpu.VMEM((TOP_E, TC, d), F32), pltpu.SemaphoreType.DMA(())])
    return pl.pallas_call(
        functools.partial(_combine_kernel, TC=TC, row0=row0), grid_spec=gs,
        out_shape=jax.ShapeDtypeStruct((n, d), F32),
        compiler_params=_cparams(("arbitrary",)),
        name="combine",
    )(pos, ys, x1, g.reshape(1, d))


def _route_tables(r, npad):
    nt = r.shape[0]
    ef = r[:, 0:TOP_E].astype(I32).reshape(-1)
    gf = r[:, TOP_E:2 * TOP_E].reshape(-1)
    onehot = (ef[:, None] == jnp.arange(N_EXPERTS, dtype=I32)[None, :]).astype(I32)
    csum = jnp.cumsum(onehot, axis=0)
    rank = jnp.sum(onehot * csum, axis=1) - 1
    nblk = (csum[-1] + MOE_TM - 1) // MOE_TM
    blk_end = jnp.cumsum(nblk)
    pos = (blk_end - nblk)[ef] * MOE_TM + rank
    n_act = blk_end[-1:]
    b = jnp.minimum(jnp.arange(npad // MOE_TM, dtype=I32), n_act[0] - 1)
    blk_exp = jnp.minimum(jnp.searchsorted(blk_end, b, side="right"), N_EXPERTS - 1).astype(I32)
    src = jnp.zeros((npad,), I32).at[pos].set(jnp.arange(nt * TOP_E, dtype=I32) // TOP_E)
    gate_sorted = jnp.zeros((npad,), F32).at[pos].set(gf).reshape(npad, 1)
    return blk_exp, n_act.astype(I32), src, gate_sorted, pos.astype(I32)


def kernel(x_prompt, x_sample, cache_k, cache_v, cache_kidx, state_conv, state_C, state_n, state_m, page_table,
           g_attn, w_in, b_gates_m, conv_w, conv_b, m_norm_w, w_proj_m, w_proj_a, w_out, g_ffn,
           w_rg, b_rg, w_re, b_re, w_gate, w_up, w_down, g_final):
    assert x_prompt.shape[0] == 1 and g_attn.shape[0] == 1
    l = 0
    Tp = x_prompt.shape[1]
    Bs, Ts = x_sample.shape[:2]
    Ns = Bs * Ts
    NT = Tp + Ns
    TM_BIG = 1664
    assert NT % TM_BIG == 0 and Ts >= CONV_W - 1 and Ts <= SUBLANES
    n_pool = cache_k.shape[1]
    P = page_table.shape[1] * PAGE_SIZE

    x_all = jnp.concatenate([x_prompt[0], x_sample.reshape(Ns, D_MODEL)], axis=0)
    xn = _rmsnorm(x_all, g_attn[l], BF16, 640)
    p32, p16 = _inproj(xn, _prep_w_in(w_in[l]), TM_BIG)

    gate_cols = slice(C_TAIL + T_IM, C_TAIL + T_IM + 2 * M_HEADS)
    ps3 = p32[Tp:].reshape(Bs, Ts, D_CAT)
    ps8 = jnp.pad(ps3, ((0, 0), (0, SUBLANES - Ts), (0, 0)))

    zero = lambda *s: jnp.zeros(s, F32)
    hm_p, C_p, n_p, m_p = _mlstm(p32[None], p32[:Tp, gate_cols].T[None], zero(1, SUBLANES, 2 * M_WIDTH),
                                 conv_w[l], conv_b[l], b_gates_m[l], m_norm_w[l],
                                 zero(1, M_HEADS, M_DH, M_DH), zero(1, M_HEADS, M_DH), zero(1, M_HEADS),
                                 T=Tp, L=256, RIN=256, valid=256)
    grow_s = jnp.pad(jnp.swapaxes(ps3[:, :, gate_cols], 1, 2), ((0, 0), (0, 0), (0, LANES - Ts)))
    cb_s = jnp.pad(state_conv[l], ((0, 0), (SUBLANES - (CONV_W - 1), 0), (0, 0)))
    hm_s, C_s, n_s, m_s = _mlstm(ps8, grow_s, cb_s, conv_w[l], conv_b[l], b_gates_m[l], m_norm_w[l],
                                 state_C[l], state_n[l], state_m[l], T=SUBLANES, L=LANES, RIN=SUBLANES, valid=Ts)

    ha_p = _dsa_prompt(p32, p16, Tp, min(TOPK_MAX, Tp // 4))
    qs = ps3[:, :, C_QI:C_QI + IDX_HEADS * LANES].reshape(Bs, Ts, IDX_HEADS, LANES)[..., :IDX_DIM].reshape(Bs, Ts * IDX_HEADS, IDX_DIM)
    ws = ps3[:, :, C_TAIL + T_WI:C_TAIL + T_WI + IDX_HEADS].reshape(Bs, Ts * IDX_HEADS, 1)
    kin = jnp.pad(ps3[:, :, C_TAIL + T_KI:C_TAIL + T_KI + IDX_DIM], ((0, 0), (0, LANES - Ts), (0, 0)))
    sc = _smp_scores(page_table, qs, ws, cache_kidx[l])
    bp, bn = _smp_select(sc, qs, ws, kin, min(TOPK_MAX, (P + Ts) // 4))
    ha_s = _smp_attn(page_table, ps3, ps8, bn, cache_k[l].reshape(n_pool, PAGE_SIZE, A_WIDTH),
                     cache_v[l].reshape(n_pool, PAGE_SIZE, A_WIDTH), bp)

    hm_all = jnp.concatenate([hm_p[0], hm_s[:, :Ts].reshape(Ns, M_WIDTH)], axis=0)
    ha_all = jnp.concatenate([ha_p, ha_s.reshape(Ns, A_WIDTH).astype(BF16)], axis=0)
    merged = _merge(hm_all, ha_all, w_proj_m[l].astype(BF16), w_proj_a[l].astype(BF16), p32, TM_BIG)
    x1 = _outproj(merged, w_out[l].astype(BF16), x_all, TM_BIG)

    wr = jnp.pad(jnp.concatenate([w_rg[l], w_re[l]], axis=1), ((0, 0), (0, LANES - N_GROUPS - N_EXPERTS))).astype(BF16)
    br = jnp.pad(jnp.concatenate([b_rg[l], b_re[l]]), (0, LANES - N_GROUPS - N_EXPERTS)).reshape(1, LANES)
    xn2, r = _router(x1, g_ffn[l], wr, br, 640)
    npad = NT * TOP_E + N_EXPERTS * MOE_TM
    blk_exp, n_act, src, gate_sorted, pos = _route_tables(r, npad)
    ys = _experts(blk_exp, n_act, src, xn2, gate_sorted, w_gate[l], w_up[l], w_down[l])
    y_p = _combine(pos, ys, x1, g_final, 0, Tp, 256)
    y_s = _combine(pos, ys, x1, g_final, Tp, Ns, LANES)

    st = lambda a, shape: a.reshape((1,) + shape)
    pp = p32[:Tp]
    return (y_p[None], y_s.reshape(Bs, Ts, D_MODEL),
            st(pp[:, C_KA:C_KA + A_WIDTH], (1, Tp, A_HEADS, A_DH)), st(pp[:, C_VA:C_VA + A_WIDTH], (1, Tp, A_HEADS, A_DH)),
            st(pp[:, C_TAIL + T_KI:C_TAIL + T_KI + IDX_DIM], (1, Tp, IDX_DIM)),
            st(pp[Tp - (CONV_W - 1):, 0:2 * M_WIDTH], (1, CONV_W - 1, 2 * M_WIDTH)),
            st(C_p, (1, M_HEADS, M_DH, M_DH)), st(n_p, (1, M_HEADS, M_DH)), st(m_p[:, :, 0, 0], (1, M_HEADS)),
            st(ps3[:, :, C_KA:C_KA + A_WIDTH], (Bs, Ts, A_HEADS, A_DH)), st(ps3[:, :, C_VA:C_VA + A_WIDTH], (Bs, Ts, A_HEADS, A_DH)),
            st(ps3[:, :, C_TAIL + T_KI:C_TAIL + T_KI + IDX_DIM], (Bs, Ts, IDX_DIM)),
            st(ps3[:, Ts - (CONV_W - 1):, 0:2 * M_WIDTH], (Bs, CONV_W - 1, 2 * M_WIDTH)),
            st(C_s, (Bs, M_HEADS, M_DH, M_DH)), st(n_s, (Bs, M_HEADS, M_DH)), st(m_s[:, :, 0, 0], (Bs, M_HEADS)))
```

```python
import functools

import jax
import jax.numpy as jnp
import numpy as np
from jax import lax
from jax.experimental import pallas as pl
from jax.experimental.pallas import tpu as pltpu

F32 = jnp.float32
BF16 = jnp.bfloat16
I32 = jnp.int32

D_MODEL = 2048
M_WIDTH = D_MODEL // 2
M_HEADS = 4
M_DH = M_WIDTH // M_HEADS
CONV_W = 4
A_WIDTH = D_MODEL // 2
A_DH = 128
A_HEADS = A_WIDTH // A_DH
IDX_HEADS = 8
IDX_DIM = 64
TOPK_MAX = 256
PAGE_SIZE = 128
N_GROUPS = 4
EXP_PER_GROUP = 8
N_EXPERTS = N_GROUPS * EXP_PER_GROUP
TOP_E = 2
D_EXPERT = D_MODEL // 4
RMS_EPS = 1e-6
IN_SIZES = (M_WIDTH, M_WIDTH, M_WIDTH, M_WIDTH, M_HEADS, M_HEADS, A_WIDTH, A_WIDTH, A_WIDTH,
            IDX_HEADS * IDX_DIM, IDX_DIM, IDX_HEADS, D_MODEL, D_MODEL)
IN_SPLITS = tuple(int(s) for s in np.cumsum(IN_SIZES)[:-1])

LANES = 128
SUBLANES = 8
VMEM_LIMIT = 56 * 1024 * 1024

C_QM, C_KM, C_VM, C_OM = 0, 1024, 2048, 3072
C_QA, C_KA, C_VA = 4096, 5120, 6144
C_GM, C_GA = 7168, 9216
C_QI = 11264
C_TAIL = 12288
T_KI, T_WI, T_IM, T_FM = 0, 64, 72, 76
D_CAT = 12800
PROJ_TN = 512

NEG = -1e30
INT_MIN = -2 ** 31


def _cparams(sem):
    return pltpu.CompilerParams(dimension_semantics=sem, vmem_limit_bytes=VMEM_LIMIT)


def _rms_kernel(x_ref, g_ref, o_ref):
    x = x_ref[...]
    y = x * lax.rsqrt(jnp.mean(x * x, axis=-1, keepdims=True) + RMS_EPS)
    o_ref[...] = (y * g_ref[...]).astype(o_ref.dtype)


def _rmsnorm(x, g, out_dtype, tm):
    m, d = x.shape
    return pl.pallas_call(
        _rms_kernel,
        grid=(m // tm,),
        in_specs=[pl.BlockSpec((tm, d), lambda i: (i, 0)), pl.BlockSpec((1, d), lambda i: (0, 0))],
        out_specs=pl.BlockSpec((tm, d), lambda i: (i, 0)),
        out_shape=jax.ShapeDtypeStruct((m, d), out_dtype),
        compiler_params=_cparams(("parallel",)),
        name="rmsnorm",
    )(x, g.reshape(1, d))


def _inproj_kernel(x_ref, w_ref, o32_ref, o16_ref):
    acc = jnp.dot(x_ref[...], w_ref[...], preferred_element_type=F32)
    o32_ref[...] = acc
    o16_ref[...] = acc.astype(BF16)


def _inproj(xn, w_cat, tm):
    m, d = xn.shape
    n = w_cat.shape[1]
    tn = PROJ_TN
    return pl.pallas_call(
        _inproj_kernel,
        grid=(n // tn, m // tm),
        in_specs=[pl.BlockSpec((tm, d), lambda j, i: (i, 0)), pl.BlockSpec((d, tn), lambda j, i: (0, j))],
        out_specs=[pl.BlockSpec((tm, tn), lambda j, i: (i, j)), pl.BlockSpec((tm, tn), lambda j, i: (i, j))],
        out_shape=[jax.ShapeDtypeStruct((m, n), F32), jax.ShapeDtypeStruct((m, n), BF16)],
        compiler_params=_cparams(("parallel", "parallel")),
        name="inproj",
    )(xn, w_cat)


def _prep_w_in(w_in):
    (q_m, k_m, v_m, o_m, i_m, f_m, q_a, k_a, v_a, q_i, k_i, w_i, g_m, g_a) = jnp.split(w_in, IN_SPLITS, axis=-1)
    d = w_in.shape[0]
    q_i = jnp.pad(q_i.reshape(d, IDX_HEADS, IDX_DIM), ((0, 0), (0, 0), (0, LANES - IDX_DIM))).reshape(d, IDX_HEADS * LANES)
    tail = jnp.concatenate([k_i, w_i, i_m, f_m], axis=-1)
    cols = [q_m, k_m, v_m, o_m, q_a, k_a, v_a, g_m, g_a, q_i, tail]
    w = jnp.concatenate(cols, axis=-1)
    return jnp.pad(w, ((0, 0), (0, D_CAT - w.shape[1]))).astype(BF16)


def _sigmoid(x):
    return 1.0 / (1.0 + jnp.exp(-x))


def _log_sigmoid(x):
    return jnp.minimum(x, 0.0) - jnp.log1p(jnp.exp(-jnp.abs(x)))


def _mlstm_kernel(q_ref, k_ref, v_ref, o_ref, tail_ref, grow_ref, cb_ref, convw_ref, convb_ref, bl_ref, bs_ref,
                  nw_ref, c0_ref, n0_ref, m0_ref,
                  h_ref, cout_ref, nout_ref, mout_ref,
                  xq_s, xk_s, c_s, n_s, m_s, vp_s, op_s, tp_s, *, L, RIN, valid):
    c = pl.program_id(1)
    nc = pl.num_programs(1)

    @pl.when(c == 0)
    def _init():
        xq_s[0:SUBLANES, :] = cb_ref[0, :, 0:M_WIDTH]
        xk_s[0:SUBLANES, :] = cb_ref[0, :, M_WIDTH:2 * M_WIDTH]
        c_s[...] = c0_ref[0]
        n_s[...] = n0_ref[0]
        m_s[...] = m0_ref[0]

    if RIN < L:
        zpad = jnp.zeros((L - RIN, M_WIDTH), F32)
        xq_s[SUBLANES + RIN:SUBLANES + L, :] = zpad
        xk_s[SUBLANES + RIN:SUBLANES + L, :] = zpad
        vp_s[RIN:L, :] = zpad
        op_s[RIN:L, :] = zpad
        tp_s[RIN:L, :] = jnp.zeros((L - RIN, LANES), F32)
    xq_s[SUBLANES:SUBLANES + RIN, :] = q_ref[0]
    xk_s[SUBLANES:SUBLANES + RIN, :] = k_ref[0]
    vp_s[0:RIN, :] = v_ref[0]
    op_s[0:RIN, :] = o_ref[0]
    tp_s[0:RIN, :] = tail_ref[0]

    def conv(xs, col0):
        w = convw_ref[:, col0:col0 + M_WIDTH]
        y = convb_ref[:, col0:col0 + M_WIDTH]
        for j in range(CONV_W):
            r0 = SUBLANES - (CONV_W - 1) + j
            y = y + xs[r0:r0 + L, :] * w[j:j + 1, :]
        return y * _sigmoid(y)

    q_all = conv(xq_s, 0) * (M_DH ** -0.5)
    k_all = conv(xk_s, M_WIDTH)
    v_all = vp_s[...]
    tail = tp_s[...] + bl_ref[...]
    grow = grow_ref[0] + bs_ref[...]

    xq_s[0:SUBLANES, :] = xq_s[L:L + SUBLANES, :]
    xk_s[0:SUBLANES, :] = xk_s[L:L + SUBLANES, :]

    tt = lax.broadcasted_iota(I32, (L, L), 0)
    ss = lax.broadcasted_iota(I32, (L, L), 1)
    causal = ss <= tt
    row_ok = lax.broadcasted_iota(I32, (L, 1), 0) < valid
    col_ok = lax.broadcasted_iota(I32, (1, L), 1) < valid

    for h in range(M_HEADS):
        hs = slice(h * M_DH, (h + 1) * M_DH)
        qh = q_all[:, hs]
        kh = k_all[:, hs]
        vh = v_all[:, hs]
        ig_c = jnp.where(row_ok, tail[:, T_IM + h:T_IM + h + 1], NEG)
        lf_c = jnp.where(row_ok, _log_sigmoid(tail[:, T_FM + h:T_FM + h + 1]), 0.0)
        ig_r = jnp.where(col_ok, grow[h:h + 1, :], NEG)
        lf_r = jnp.where(col_ok, _log_sigmoid(grow[M_HEADS + h:M_HEADS + h + 1, :]), 0.0)
        b_c = jnp.sum(jnp.where(causal, lf_r, 0.0), axis=1, keepdims=True)
        b_r = jnp.sum(jnp.where(tt <= ss, lf_c, 0.0), axis=0, keepdims=True)
        dmat = jnp.where(causal, b_c - b_r + ig_r, NEG)
        m_prev = m_s[h][:, 0:1]
        m_t = jnp.maximum(b_c + m_prev, jnp.max(dmat, axis=1, keepdims=True))
        e = jnp.exp(dmat - m_t)
        qb = qh.astype(BF16)
        kb = kh.astype(BF16)
        s = lax.dot_general(qb, kb, (((1,), (1,)), ((), ())), preferred_element_type=F32) * e
        inter = jnp.exp(b_c + m_prev - m_t)
        ch = c_s[h]
        num = jnp.dot(s.astype(BF16), vh.astype(BF16), preferred_element_type=F32) + inter * lax.dot_general(
            qb, ch.astype(BF16), (((1,), (1,)), ((), ())), preferred_element_type=F32)
        nh = n_s[h]
        den = jnp.sum(s, axis=1, keepdims=True) + inter * jnp.sum(qh * nh, axis=1, keepdims=True)
        hh = num / jnp.maximum(jnp.abs(den), jnp.exp(-m_t))
        hh = hh * lax.rsqrt(jnp.mean(hh * hh, axis=1, keepdims=True) + RMS_EPS)
        out = _sigmoid(op_s[:, hs]) * (hh * nw_ref[:, hs])
        h_ref[0, :, hs] = out[0:RIN, :].astype(h_ref.dtype)
        m_new = m_t[L - 1:L, :]
        b_last = b_c[L - 1:L, :]
        w_c = jnp.exp(b_last - b_c + ig_c - m_new)
        decay = jnp.exp(b_last + m_prev - m_new)
        upd = lax.dot_general((w_c * vh).astype(BF16), kb, (((0,), (0,)), ((), ())), preferred_element_type=F32)
        c_s[h] = decay * ch + upd
        n_s[h] = decay * nh + jnp.sum(w_c * kh, axis=0, keepdims=True)
        m_s[h] = jnp.broadcast_to(m_new, (1, LANES))

    @pl.when(c == nc - 1)
    def _fin():
        cout_ref[0] = c_s[...]
        nout_ref[0] = n_s[...]
        mout_ref[0] = m_s[...]


def _mlstm(p32, grow, convbuf, conv_w, conv_b, b_gates, m_norm_w, c0, n0, m0, *, T, L, RIN, valid):
    B = p32.shape[0]
    nc = T // RIN
    bl = jnp.zeros((1, LANES), F32).at[0, T_IM:T_IM + 2 * M_HEADS].set(b_gates)
    bs = jnp.broadcast_to(b_gates[:, None], (2 * M_HEADS, L))
    kern = functools.partial(_mlstm_kernel, L=L, RIN=RIN, valid=valid)
    cblk = lambda col: pl.BlockSpec((1, RIN, M_WIDTH), lambda b, c, col=col: (b, c, col // M_WIDTH))
    const2 = lambda shape: pl.BlockSpec(shape, lambda b, c: (0, 0))
    per_b = lambda shape: pl.BlockSpec(shape, lambda b, c: (b,) + (0,) * (len(shape) - 1))
    return pl.pallas_call(
        kern,
        grid=(B, nc),
        in_specs=[cblk(C_QM), cblk(C_KM), cblk(C_VM), cblk(C_OM),
                  pl.BlockSpec((1, RIN, LANES), lambda b, c: (b, c, C_TAIL // LANES)),
                  pl.BlockSpec((1, 2 * M_HEADS, L), lambda b, c: (b, 0, c)),
                  per_b((1, SUBLANES, 2 * M_WIDTH)),
                  const2((CONV_W, 2 * M_WIDTH)), const2((1, 2 * M_WIDTH)), const2((1, LANES)),
                  const2((2 * M_HEADS, L)), const2((1, M_WIDTH)),
                  per_b((1, M_HEADS, M_DH, M_DH)), per_b((1, M_HEADS, 1, M_DH)), per_b((1, M_HEADS, 1, LANES))],
        out_specs=[pl.BlockSpec((1, RIN, M_WIDTH), lambda b, c: (b, c, 0)),
                   per_b((1, M_HEADS, M_DH, M_DH)), per_b((1, M_HEADS, 1, M_DH)), per_b((1, M_HEADS, 1, LANES))],
        out_shape=[jax.ShapeDtypeStruct((B, T, M_WIDTH), BF16),
                   jax.ShapeDtypeStruct((B, M_HEADS, M_DH, M_DH), F32),
                   jax.ShapeDtypeStruct((B, M_HEADS, 1, M_DH), F32),
                   jax.ShapeDtypeStruct((B, M_HEADS, 1, LANES), F32)],
        scratch_shapes=[pltpu.VMEM((SUBLANES + L, M_WIDTH), F32), pltpu.VMEM((SUBLANES + L, M_WIDTH), F32),
                        pltpu.VMEM((M_HEADS, M_DH, M_DH), F32), pltpu.VMEM((M_HEADS, 1, M_DH), F32),
                        pltpu.VMEM((M_HEADS, 1, LANES), F32),
                        pltpu.VMEM((L, M_WIDTH), F32), pltpu.VMEM((L, M_WIDTH), F32), pltpu.VMEM((L, LANES), F32)],
        compiler_params=_cparams(("parallel", "arbitrary")),
        name="mlstm",
    )(p32, p32, p32, p32, p32, grow, convbuf, conv_w, conv_b.reshape(1, -1), bl, bs, m_norm_w.reshape(1, -1),
      c0, n0.reshape(B, M_HEADS, 1, M_DH), jnp.broadcast_to(m0[:, :, None, None], (B, M_HEADS, 1, LANES)))


def _score_key(sc):
    bits = lax.bitcast_convert_type(sc, I32)
    return jnp.where(bits < 0, INT_MIN - bits, bits)


DSA_TQ = 512
DSA_TK = 512
DSA_RG = 64
DSA_RB = 256
LOG2E = 1.4426950408889634


def _dsa_prompt_kernel(qi_tab, kj_tab, qidx_ref, tail_ref, kit_ref, qa_ref, ka_ref, va_ref, o_ref,
                       keys_s, cand_s, cnt_s, thr_s, thrm_s, cut_s, bias_s, tie_s, m_s, l_s, acc_s, *, TQ, TK, topk, pos_bits):
    step = pl.program_id(0)
    qi = qi_tab[step]
    kj = kj_tab[step]
    RG = DSA_RG
    nlc = TK // LANES

    def count_pass(pred):
        def rbody(r, _):
            r0 = pl.multiple_of(r * RG, RG)
            cand = cand_s[pl.ds(r0, RG), :]
            aux = thr_s[pl.ds(r0, RG), :]

            def kb(j, cnt):
                for c in range(nlc):
                    blk = keys_s[j, pl.ds(r0, RG), c * LANES:(c + 1) * LANES]
                    pos = j * TK + c * LANES + lax.broadcasted_iota(I32, (RG, LANES), 1)
                    cnt = cnt + jnp.where(pred(blk, cand, aux, pos), 1.0, 0.0)
                return cnt

            cnt_s[pl.ds(r0, RG), :] = lax.fori_loop(0, qi + 1, kb, jnp.zeros((RG, LANES), F32))
            return 0

        lax.fori_loop(0, TQ // RG, rbody, 0)
        return jnp.sum(cnt_s[...], axis=1, keepdims=True)

    @pl.when(kj == 0)
    def _phase1():
        w = tail_ref[:, T_WI:T_WI + IDX_HEADS] * (IDX_HEADS ** -0.5)
        rowpos = qi * TQ + lax.broadcasted_iota(I32, (TQ, TK), 0)

        def kbody(j, _):
            kt = kit_ref[j].astype(BF16)
            sc = jnp.zeros((TQ, TK), F32)
            for h in range(IDX_HEADS):
                qh = qidx_ref[:, h * LANES:(h + 1) * LANES].astype(BF16)
                s = jnp.dot(qh, kt, preferred_element_type=F32) * (IDX_DIM ** -0.5)
                sc = sc + jnp.maximum(s, 0.0) * w[:, h:h + 1]
            colpos = j * TK + lax.broadcasted_iota(I32, (TQ, TK), 1)
            keys_s[j] = jnp.where(colpos <= rowpos, _score_key(sc), INT_MIN)
            return 0

        lax.fori_loop(0, qi + 1, kbody, 0)

        ge = lambda blk, cand, aux, pos: blk >= cand
        cand_s[...] = jnp.zeros((TQ, LANES), I32)
        cnt = count_pass(ge)
        lo = jnp.where(cnt >= topk, 0, INT_MIN).astype(I32)

        def bit_body(b, lo):
            cand = lo | lax.shift_left(jnp.int32(1), 30 - b)
            cand_s[...] = jnp.broadcast_to(cand, (TQ, LANES))
            cnt = count_pass(ge)
            return jnp.where(cnt >= topk, cand, lo)

        thr = lax.fori_loop(0, 31, bit_body, lo)
        thr_s[...] = jnp.broadcast_to(thr, (TQ, LANES))
        cand_s[...] = thr_s[...]
        n_gt = count_pass(lambda blk, cand, aux, pos: blk > cand)
        n_ge = count_pass(ge)
        need = topk - n_gt
        short = thr == INT_MIN
        thrm_s[...] = jnp.broadcast_to(jnp.where(short, INT_MIN, thr - 1), (TQ, LANES))
        cut_s[...] = jnp.broadcast_to(jnp.where(short, -1, 2 ** 30).astype(I32), (TQ, LANES))
        surplus = jnp.max(jnp.where((n_ge - n_gt > need) & jnp.logical_not(short), 1.0, 0.0))
        tie_s[0] = (surplus > 0.0).astype(I32)

        @pl.when(surplus > 0.0)
        def _ties():
            eq_below = lambda blk, cand, aux, pos: (blk == aux) & (pos < cand)

            def tie_body(b, x):
                cand = x + lax.shift_left(jnp.int32(1), pos_bits - 1 - b)
                cand_s[...] = jnp.broadcast_to(cand, (TQ, LANES))
                cnt = count_pass(eq_below)
                return jnp.where(cnt < need, cand, x)

            x = lax.fori_loop(0, pos_bits, tie_body, jnp.zeros((TQ, 1), I32))
            cut_s[...] = jnp.broadcast_to(jnp.where(short, -1, x), (TQ, LANES))

        m_s[...] = jnp.full(m_s.shape, NEG, F32)
        l_s[...] = jnp.zeros(l_s.shape, F32)
        acc_s[...] = jnp.zeros(acc_s.shape, F32)

    tile_l = lambda a: jnp.concatenate([a] * nlc, axis=1)

    @pl.when(tie_s[0] == 0)
    def _bias_plain():
        bias_s[...] = jnp.where(keys_s[kj] > tile_l(thrm_s[...]), 0.0, NEG)

    @pl.when(tie_s[0] != 0)
    def _bias_tied():
        key = keys_s[kj]
        thr = tile_l(thr_s[...])
        colpos = kj * TK + lax.broadcasted_iota(I32, (TQ, TK), 1)
        sel = (key > thr) | ((key == thr) & (colpos <= tile_l(cut_s[...])))
        bias_s[...] = jnp.where(sel, 0.0, NEG)

    RB = DSA_RB
    c1 = (A_DH ** -0.5) * LOG2E

    def rb_body(rb, _):
        rows = pl.ds(pl.multiple_of(rb * RB, RB), RB)
        bias = bias_s[rows, :]
        hsl = [slice(h * A_DH, (h + 1) * A_DH) for h in range(A_HEADS)]
        m_old = [m_s[h, rows, :] for h in range(A_HEADS)]
        l_old = [l_s[h, rows, :] for h in range(A_HEADS)]
        a_old = [acc_s[rows, hs] for hs in hsl]
        ts = [lax.dot_general(qa_ref[rows, hs], ka_ref[:, hs], (((1,), (1,)), ((), ())), preferred_element_type=F32) * c1 + bias
              for hs in hsl]
        m_new = [jnp.maximum(m_old[h], jnp.max(ts[h], axis=1, keepdims=True)) for h in range(A_HEADS)]
        l_new, a_new = [], []
        for h in range(A_HEADS):
            alpha = jnp.exp2(m_old[h] - m_new[h])
            p = jnp.exp2(ts[h] - tile_l(m_new[h]))
            psum = p[:, 0:LANES]
            for c in range(1, nlc):
                psum = psum + p[:, c * LANES:(c + 1) * LANES]
            l_new.append(alpha * l_old[h] + psum)
            a_new.append(alpha * a_old[h] + jnp.dot(p.astype(BF16), va_ref[:, hsl[h]], preferred_element_type=F32))
        for h in range(A_HEADS):
            m_s[h, rows, :] = m_new[h]
            l_s[h, rows, :] = l_new[h]
            acc_s[rows, hsl[h]] = a_new[h]
        return 0

    lax.fori_loop(0, TQ // RB, rb_body, 0)

    @pl.when(kj == qi)
    def _fin():
        for h in range(A_HEADS):
            hs = slice(h * A_DH, (h + 1) * A_DH)
            o_ref[:, hs] = (acc_s[:, hs] / jnp.sum(l_s[h], axis=1, keepdims=True)).astype(o_ref.dtype)


def _dsa_prompt(p32, p16, T, topk):
    TQ, TK = DSA_TQ, DSA_TK
    nq = T // TQ
    assert TQ == TK
    qi_tab = np.concatenate([np.full(i + 1, i) for i in range(nq)]).astype(np.int32)
    kj_tab = np.concatenate([np.arange(i + 1) for i in range(nq)]).astype(np.int32)
    ki = p32[:T, C_TAIL + T_KI:C_TAIL + T_KI + IDX_DIM]
    kit = jnp.pad(ki.T, ((0, LANES - IDX_DIM), (0, 0))).reshape(LANES, T // TK, TK).transpose(1, 0, 2)
    kern = functools.partial(_dsa_prompt_kernel, TQ=TQ, TK=TK, topk=topk, pos_bits=int(T - 1).bit_length())
    gs = pltpu.PrefetchScalarGridSpec(
        num_scalar_prefetch=2,
        grid=(len(qi_tab),),
        in_specs=[pl.BlockSpec((TQ, IDX_HEADS * LANES), lambda s, qt, kt: (qt[s], C_QI // (IDX_HEADS * LANES))),
                  pl.BlockSpec((TQ, LANES), lambda s, qt, kt: (qt[s], C_TAIL // LANES)),
                  pl.BlockSpec((T // TK, LANES, TK), lambda s, qt, kt: (0, 0, 0)),
                  pl.BlockSpec((TQ, A_WIDTH), lambda s, qt, kt: (qt[s], C_QA // A_WIDTH)),
                  pl.BlockSpec((TK, A_WIDTH), lambda s, qt, kt: (kt[s], C_KA // A_WIDTH)),
                  pl.BlockSpec((TK, A_WIDTH), lambda s, qt, kt: (kt[s], C_VA // A_WIDTH))],
        out_specs=pl.BlockSpec((TQ, A_WIDTH), lambda s, qt, kt: (qt[s], 0)),
        scratch_shapes=[pltpu.VMEM((T // TK, TQ, TK), I32),
                        pltpu.VMEM((TQ, LANES), I32), pltpu.VMEM((TQ, LANES), F32), pltpu.VMEM((TQ, LANES), I32),
                        pltpu.VMEM((TQ, LANES), I32), pltpu.VMEM((TQ, LANES), I32),
                        pltpu.VMEM((TQ, TK), F32), pltpu.SMEM((1,), I32),
                        pltpu.VMEM((A_HEADS, TQ, LANES), F32), pltpu.VMEM((A_HEADS, TQ, LANES), F32),
                        pltpu.VMEM((TQ, A_WIDTH), F32)])
    return pl.pallas_call(
        kern, grid_spec=gs,
        out_shape=jax.ShapeDtypeStruct((T, A_WIDTH), BF16),
        compiler_params=_cparams(("arbitrary",)),
        name="dsa_prompt",
    )(jnp.asarray(qi_tab), jnp.asarray(kj_tab), p32, p32, kit, p16, p16, p16)


def _idx_scores(q, w, keys):
    s = lax.dot_general(q.astype(BF16), keys.astype(BF16), (((1,), (1,)), ((), ())), preferred_element_type=F32)
    s = jnp.maximum(s * (IDX_DIM ** -0.5), 0.0) * (w * (IDX_HEADS ** -0.5))
    n_tok = q.shape[0] // IDX_HEADS
    return jnp.concatenate([jnp.sum(s[t * IDX_HEADS:(t + 1) * IDX_HEADS], axis=0, keepdims=True) for t in range(n_tok)], axis=0)


def _smp_scores_kernel(pt_ref, q_ref, w_ref, kp_ref, o_ref):
    o_ref[0] = _idx_scores(q_ref[0], w_ref[0], kp_ref[0])


def _smp_scores(page_table, qs, ws, cache_kidx):
    B, n_pages = page_table.shape
    R = qs.shape[1]
    n_tok = R // IDX_HEADS
    gs = pltpu.PrefetchScalarGridSpec(
        num_scalar_prefetch=1,
        grid=(B, n_pages),
        in_specs=[pl.BlockSpec((1, R, IDX_DIM), lambda b, p, pt: (b, 0, 0)),
                  pl.BlockSpec((1, R, 1), lambda b, p, pt: (b, 0, 0)),
                  pl.BlockSpec((1, PAGE_SIZE, IDX_DIM), lambda b, p, pt: (pt[b, p], 0, 0))],
        out_specs=pl.BlockSpec((1, n_tok, PAGE_SIZE), lambda b, p, pt: (b, 0, p)))
    return pl.pallas_call(
        _smp_scores_kernel, grid_spec=gs,
        out_shape=jax.ShapeDtypeStruct((B, n_tok, n_pages * PAGE_SIZE), F32),
        compiler_params=_cparams(("parallel", "arbitrary")),
        name="smp_scores",
    )(page_table, qs, ws, cache_kidx)


def _smp_select_kernel(sc_ref, q_ref, w_ref, kin_ref, bp_ref, bn_ref, *, topk):
    kp = _score_key(sc_ref[0])
    n_tok, P = kp.shape
    s_new = _idx_scores(q_ref[0], w_ref[0], kin_ref[0])
    lane = lax.broadcasted_iota(I32, (n_tok, LANES), 1)
    trow = lax.broadcasted_iota(I32, (n_tok, LANES), 0)
    kn = jnp.where(lane <= trow, _score_key(s_new), INT_MIN)
    pos_p = lax.broadcasted_iota(I32, (n_tok, P), 1)
    pos_n = P + lane

    def count(fp, fn):
        return (jnp.sum(jnp.where(fp, 1.0, 0.0), axis=1, keepdims=True)
                + jnp.sum(jnp.where(fn, 1.0, 0.0), axis=1, keepdims=True))

    lo = jnp.where(count(kp >= 0, kn >= 0) >= topk, 0, INT_MIN).astype(I32)

    def bit_body(b, lo):
        cand = lo | lax.shift_left(jnp.int32(1), 30 - b)
        return jnp.where(count(kp >= cand, kn >= cand) >= topk, cand, lo)

    thr = lax.fori_loop(0, 31, bit_body, lo)
    need = topk - count(kp > thr, kn > thr)
    pos_bits = int(P + LANES - 1).bit_length()

    def tie_body(b, x):
        cand = x + lax.shift_left(jnp.int32(1), pos_bits - 1 - b)
        cnt = count((kp == thr) & (pos_p < cand), (kn == thr) & (pos_n < cand))
        return jnp.where(cnt < need, cand, x)

    cut = lax.fori_loop(0, pos_bits, tie_body, jnp.zeros((n_tok, 1), I32))
    sel_p = (kp > thr) | ((kp == thr) & (pos_p <= cut))
    sel_n = ((kn > thr) | ((kn == thr) & (pos_n <= cut))) & (kn != INT_MIN)
    bp_ref[0] = jnp.where(sel_p, 0.0, NEG)
    bn_ref[0] = jnp.where(sel_n, 0.0, NEG)


def _smp_select(sc, qs, ws, kin, topk):
    B, n_tok, P = sc.shape
    R = qs.shape[1]
    per_b = lambda shape: pl.BlockSpec(shape, lambda b: (b, 0, 0))
    return pl.pallas_call(
        functools.partial(_smp_select_kernel, topk=topk),
        grid=(B,),
        in_specs=[per_b((1, n_tok, P)), per_b((1, R, IDX_DIM)), per_b((1, R, 1)), per_b((1, LANES, IDX_DIM))],
        out_specs=[per_b((1, n_tok, P)), per_b((1, n_tok, LANES))],
        out_shape=[jax.ShapeDtypeStruct((B, n_tok, P), F32), jax.ShapeDtypeStruct((B, n_tok, LANES), F32)],
        compiler_params=_cparams(("parallel",)),
        name="smp_select",
    )(sc, qs, ws, kin)


def _smp_attn_kernel(pt_ref, q_ref, kn_ref, vn_ref, bn_ref, kp_ref, vp_ref, bp_ref, o_ref,
                     qbd_s, kpad_s, vpad_s, m_s, l_s, acc_s, *, n_tok):
    p = pl.program_id(1)
    seg = (lax.broadcasted_iota(I32, (A_HEADS, A_WIDTH), 1) // A_DH) == lax.broadcasted_iota(I32, (A_HEADS, A_WIDTH), 0)

    def attend(k, v, bias):
        n = k.shape[0]
        b32 = jnp.concatenate([jnp.broadcast_to(bias[t:t + 1, :], (A_HEADS, n)) for t in range(n_tok)], axis=0)
        s = lax.dot_general(qbd_s[...].astype(BF16), k, (((1,), (1,)), ((), ())), preferred_element_type=F32)
        s = s * (A_DH ** -0.5) + b32
        m_old = m_s[:, 0:1]
        m_new = jnp.maximum(m_old, jnp.max(s, axis=1, keepdims=True))
        alpha = jnp.exp(m_old - m_new)
        pr = jnp.exp(s - m_new)
        l_s[...] = jnp.broadcast_to(alpha * l_s[:, 0:1] + jnp.sum(pr, axis=1, keepdims=True), l_s.shape)
        acc_s[...] = alpha * acc_s[...] + jnp.dot(pr.astype(BF16), v, preferred_element_type=F32)
        m_s[...] = jnp.broadcast_to(m_new, m_s.shape)

    @pl.when(p == 0)
    def _first():
        q = q_ref[0]
        for t in range(n_tok):
            qbd_s[t * A_HEADS:(t + 1) * A_HEADS, :] = jnp.where(seg, jnp.broadcast_to(q[t:t + 1, :], (A_HEADS, A_WIDTH)), 0.0)
        m_s[...] = jnp.full(m_s.shape, NEG, F32)
        l_s[...] = jnp.zeros(l_s.shape, F32)
        acc_s[...] = jnp.zeros(acc_s.shape, F32)
        kpad_s[...] = jnp.zeros(kpad_s.shape, F32)
        vpad_s[...] = jnp.zeros(vpad_s.shape, F32)
        kpad_s[0:SUBLANES, :] = kn_ref[0]
        vpad_s[0:SUBLANES, :] = vn_ref[0]
        attend(kpad_s[...].astype(BF16), vpad_s[...].astype(BF16), bn_ref[0])

    attend(kp_ref[0].astype(BF16), vp_ref[0].astype(BF16), bp_ref[0])

    @pl.when(p == pl.num_programs(1) - 1)
    def _fin():
        for t in range(n_tok):
            rs = slice(t * A_HEADS, (t + 1) * A_HEADS)
            blk = acc_s[rs, :] / l_s[rs, 0:1]
            o_ref[0, t:t + 1, :] = jnp.sum(jnp.where(seg, blk, 0.0), axis=0, keepdims=True)


def _smp_attn(page_table, ps3, ps8, bn, ck, cv, bp):
    B, n_pages = page_table.shape
    n_tok = ps3.shape[1]
    R = n_tok * A_HEADS
    gs = pltpu.PrefetchScalarGridSpec(
        num_scalar_prefetch=1,
        grid=(B, n_pages),
        in_specs=[pl.BlockSpec((1, n_tok, A_WIDTH), lambda b, p, pt: (b, 0, C_QA // A_WIDTH)),
                  pl.BlockSpec((1, SUBLANES, A_WIDTH), lambda b, p, pt: (b, 0, C_KA // A_WIDTH)),
                  pl.BlockSpec((1, SUBLANES, A_WIDTH), lambda b, p, pt: (b, 0, C_VA // A_WIDTH)),
                  pl.BlockSpec((1, n_tok, LANES), lambda b, p, pt: (b, 0, 0)),
                  pl.BlockSpec((1, PAGE_SIZE, A_WIDTH), lambda b, p, pt: (pt[b, p], 0, 0)),
                  pl.BlockSpec((1, PAGE_SIZE, A_WIDTH), lambda b, p, pt: (pt[b, p], 0, 0)),
                  pl.BlockSpec((1, n_tok, PAGE_SIZE), lambda b, p, pt: (b, 0, p))],
        out_specs=pl.BlockSpec((1, n_tok, A_WIDTH), lambda b, p, pt: (b, 0, 0)),
        scratch_shapes=[pltpu.VMEM((R, A_WIDTH), F32), pltpu.VMEM((LANES, A_WIDTH), F32), pltpu.VMEM((LANES, A_WIDTH), F32),
                        pltpu.VMEM((R, LANES), F32), pltpu.VMEM((R, LANES), F32), pltpu.VMEM((R, A_WIDTH), F32)])
    return pl.pallas_call(
        functools.partial(_smp_attn_kernel, n_tok=n_tok), grid_spec=gs,
        out_shape=jax.ShapeDtypeStruct((B, n_tok, A_WIDTH), F32),
        compiler_params=_cparams(("parallel", "arbitrary")),
        name="smp_attn",
    )(page_table, ps3, ps8, ps8, bn, ck, cv, bp)


SMP_PGS = 16
SMP_PG = 8
SMP_ROWS = SUBLANES
SMP_SEL_B = 16


def _idx_scores_t(q, w, kt):
    s = jnp.dot(q.astype(BF16), kt.astype(BF16), preferred_element_type=F32)
    s = jnp.maximum(s * (IDX_DIM ** -0.5), 0.0) * (w * (IDX_HEADS ** -0.5))
    n_tok = q.shape[0] // IDX_HEADS
    rows = [jnp.sum(s[t * IDX_HEADS:(t + 1) * IDX_HEADS], axis=0, keepdims=True) for t in range(n_tok)]
    rows.append(jnp.zeros((SMP_ROWS - n_tok, s.shape[1]), F32))
    return jnp.concatenate(rows, axis=0)


def _smp_scores2_kernel(pt_ref, q_ref, w_ref, *refs):
    o_ref = refs[-1]
    kt = jnp.concatenate([r[0] for r in refs[:-1]], axis=1)
    o_ref[0] = _idx_scores_t(q_ref[0], w_ref[0], kt)


def _smp_scores2(page_table, qs, ws, kidx_t):
    B, n_pages = page_table.shape
    R = qs.shape[1]
    page = lambda i: pl.BlockSpec((1, IDX_DIM, PAGE_SIZE), lambda b, p, pt: (pt[b, p * SMP_PGS + i], 0, 0))
    gs = pltpu.PrefetchScalarGridSpec(
        num_scalar_prefetch=1,
        grid=(B, n_pages // SMP_PGS),
        in_specs=[pl.BlockSpec((1, R, IDX_DIM), lambda b, p, pt: (b, 0, 0)),
                  pl.BlockSpec((1, R, 1), lambda b, p, pt: (b, 0, 0))] + [page(i) for i in range(SMP_PGS)],
        out_specs=pl.BlockSpec((1, SMP_ROWS, SMP_PGS * PAGE_SIZE), lambda b, p, pt: (b, 0, p)))
    return pl.pallas_call(
        _smp_scores2_kernel, grid_spec=gs,
        out_shape=jax.ShapeDtypeStruct((B, SMP_ROWS, n_pages * PAGE_SIZE), F32),
        compiler_params=_cparams(("parallel", "arbitrary")),
        name="smp_scores",
    )(page_table, qs, ws, *([kidx_t] * SMP_PGS))


def _smp_select2_kernel(sc_ref, q_ref, w_ref, kint_ref, mp_ref, mn_ref, keys_s, cand_s, thr_s, cnt_s,
                        *, n_tok, topk):
    NB = q_ref.shape[0]
    R = NB * SMP_ROWS
    P = sc_ref.shape[1]
    NCH = P // LANES
    RG = DSA_RG
    lane = lax.broadcasted_iota(I32, (R, LANES), 1)
    trow = lax.broadcasted_iota(I32, (R, LANES), 0) % SMP_ROWS
    for c in range(NCH):
        keys_s[c] = _score_key(sc_ref[:, c * LANES:(c + 1) * LANES])
    s_new = jnp.concatenate([_idx_scores_t(q_ref[b], w_ref[b], kint_ref[b]) for b in range(NB)], axis=0)
    keys_s[NCH] = jnp.where(lane <= trow, _score_key(s_new), INT_MIN)

    def count_pass(pred):
        def rbody(r, _):
            rows = pl.ds(pl.multiple_of(r * RG, RG), RG)
            cand = cand_s[rows, :]
            aux = thr_s[rows, :]

            def cb(c, cnt):
                pos = c * LANES + lax.broadcasted_iota(I32, (RG, LANES), 1)
                return cnt + jnp.where(pred(keys_s[c, rows, :], cand, aux, pos), 1.0, 0.0)

            cnt_s[rows, :] = lax.fori_loop(0, NCH + 1, cb, jnp.zeros((RG, LANES), F32))
            return 0

        lax.fori_loop(0, R // RG, rbody, 0)
        return jnp.sum(cnt_s[...], axis=1, keepdims=True)

    ge = lambda blk, cand, aux, pos: blk >= cand
    cand_s[...] = jnp.zeros((R, LANES), I32)
    lo = jnp.where(count_pass(ge) >= topk, 0, INT_MIN).astype(I32)

    def bit_body(b, lo):
        cand = lo | lax.shift_left(jnp.int32(1), 30 - b)
        cand_s[...] = jnp.broadcast_to(cand, (R, LANES))
        return jnp.where(count_pass(ge) >= topk, cand, lo)

    thr = lax.fori_loop(0, 31, bit_body, lo)
    thr_s[...] = jnp.broadcast_to(thr, (R, LANES))
    cand_s[...] = thr_s[...]
    need = topk - count_pass(lambda blk, cand, aux, pos: blk > cand)
    pos_bits = int(P + LANES - 1).bit_length()
    eq_below = lambda blk, cand, aux, pos: (blk == aux) & (pos < cand)

    def tie_body(b, x):
        cand = x + lax.shift_left(jnp.int32(1), pos_bits - 1 - b)
        cand_s[...] = jnp.broadcast_to(cand, (R, LANES))
        return jnp.where(count_pass(eq_below) < need, cand, x)

    cut = jnp.broadcast_to(lax.fori_loop(0, pos_bits, tie_body, jnp.zeros((R, 1), I32)), (R, LANES))
    thr_b = thr_s[...]
    row_ok = trow < n_tok
    for c in range(NCH + 1):
        key = keys_s[c]
        sel = ((key > thr_b) | ((key == thr_b) & (c * LANES + lane <= cut))) & (key != INT_MIN) & row_ok
        if c < NCH:
            mp_ref[:, c * LANES:(c + 1) * LANES] = jnp.where(sel, 1.0, 0.0)
        else:
            mn_ref[...] = jnp.where(sel, 1.0, 0.0)


def _smp_select2(sc, qs, ws, kin_t, n_tok, topk):
    R, P = sc.shape
    B = qs.shape[0]
    NB = SMP_SEL_B
    RS = NB * SMP_ROWS
    Rq = qs.shape[1]
    return pl.pallas_call(
        functools.partial(_smp_select2_kernel, n_tok=n_tok, topk=topk),
        grid=(B // NB,),
        in_specs=[pl.BlockSpec((RS, P), lambda i: (i, 0)),
                  pl.BlockSpec((NB, Rq, IDX_DIM), lambda i: (i, 0, 0)), pl.BlockSpec((NB, Rq, 1), lambda i: (i, 0, 0)),
                  pl.BlockSpec((NB, IDX_DIM, LANES), lambda i: (i, 0, 0))],
        out_specs=[pl.BlockSpec((RS, P), lambda i: (i, 0)), pl.BlockSpec((RS, LANES), lambda i: (i, 0))],
        out_shape=[jax.ShapeDtypeStruct((R, P), F32), jax.ShapeDtypeStruct((R, LANES), F32)],
        scratch_shapes=[pltpu.VMEM((P // LANES + 1, RS, LANES), I32), pltpu.VMEM((RS, LANES), I32),
                        pltpu.VMEM((RS, LANES), I32), pltpu.VMEM((RS, LANES), F32)],
        compiler_params=_cparams(("parallel",)),
        name="smp_select",
    )(sc, qs, ws, kin_t)


def _smp_attn2_kernel(pt_ref, q_ref, kn_ref, vn_ref, mn_ref, mp_ref, *refs, n_tok):
    k_refs = refs[:SMP_PG]
    v_refs = refs[SMP_PG:2 * SMP_PG]
    o_ref, kpad_s, vpad_s, m_s, l_s, acc_s = refs[2 * SMP_PG:]
    p = pl.program_id(1)
    NL = PAGE_SIZE * A_HEADS
    R = n_tok * A_HEADS
    c1 = (A_DH ** -0.5) * LOG2E
    diag = jnp.where(lax.broadcasted_iota(I32, (A_HEADS, NL), 1) % A_HEADS == lax.broadcasted_iota(I32, (A_HEADS, NL), 0), 1.0, 0.0)
    expand = jnp.where(lax.broadcasted_iota(I32, (PAGE_SIZE, NL), 1) // A_HEADS == lax.broadcasted_iota(I32, (PAGE_SIZE, NL), 0),
                       1.0, 0.0).astype(BF16)
    qb = q_ref[0].astype(BF16)

    def attend(k_list, v_list, masks):
        n = len(k_list)
        x = jnp.dot(jnp.concatenate(masks, axis=0).astype(BF16), expand, preferred_element_type=F32)
        ts = []
        for i in range(n):
            s = lax.dot_general(qb, k_list[i], (((1,), (1,)), ((), ())), preferred_element_type=F32)
            ok = jnp.concatenate([jnp.broadcast_to(x[i * SMP_ROWS + t:i * SMP_ROWS + t + 1, :], (A_HEADS, NL)) * diag
                                  for t in range(n_tok)], axis=0)
            ts.append(s * c1 + jnp.where(ok > 0.5, 0.0, NEG))
        m_old = m_s[...]
        m_new = jnp.maximum(m_old, jnp.max(jnp.concatenate(ts, axis=1), axis=1, keepdims=True))
        alpha = jnp.exp2(m_old - m_new)
        acc = alpha * acc_s[...]
        lsum = alpha * l_s[...]
        m_t = jnp.concatenate([m_new] * (NL // LANES), axis=1)
        for i in range(n):
            pr = jnp.exp2(ts[i] - m_t)
            for c in range(NL // LANES):
                lsum = lsum + pr[:, c * LANES:(c + 1) * LANES]
            acc = acc + jnp.dot(pr.astype(BF16), v_list[i], preferred_element_type=F32)
        m_s[...] = m_new
        l_s[...] = lsum
        acc_s[...] = acc

    @pl.when(p == 0)
    def _first():
        m_s[...] = jnp.full(m_s.shape, NEG, F32)
        l_s[...] = jnp.zeros(l_s.shape, F32)
        acc_s[...] = jnp.zeros(acc_s.shape, F32)
        kpad_s[...] = jnp.zeros(kpad_s.shape, F32)
        vpad_s[...] = jnp.zeros(vpad_s.shape, F32)
        kpad_s[0:R, :] = kn_ref[0]
        vpad_s[0:R, :] = vn_ref[0]
        attend([kpad_s[...].astype(BF16)], [vpad_s[...].astype(BF16)], [mn_ref[0]])

    attend([r[0].reshape(NL, A_DH).astype(BF16) for r in k_refs], [r[0].reshape(NL, A_DH).astype(BF16) for r in v_refs],
           [mp_ref[0][:, i * PAGE_SIZE:(i + 1) * PAGE_SIZE] for i in range(SMP_PG)])

    @pl.when(p == pl.num_programs(1) - 1)
    def _fin():
        o_ref[0] = acc_s[...] / jnp.sum(l_s[...], axis=1, keepdims=True)


def _smp_attn2(page_table, q32, kn32, vn32, mn, mp, ck, cv, n_tok):
    B, n_pages = page_table.shape
    R = n_tok * A_HEADS
    NL = PAGE_SIZE * A_HEADS
    per_b = lambda shape: pl.BlockSpec(shape, lambda b, p, pt: (b, 0, 0))
    page = lambda i: pl.BlockSpec((1, PAGE_SIZE, A_HEADS, A_DH), lambda b, p, pt: (pt[b, p * SMP_PG + i], 0, 0, 0))
    gs = pltpu.PrefetchScalarGridSpec(
        num_scalar_prefetch=1,
        grid=(B, n_pages // SMP_PG),
        in_specs=[per_b((1, R, A_DH)), per_b((1, R, A_DH)), per_b((1, R, A_DH)), per_b((1, SMP_ROWS, LANES)),
                  pl.BlockSpec((1, SMP_ROWS, SMP_PG * PAGE_SIZE), lambda b, p, pt: (b, 0, p))]
                 + [page(i) for i in range(SMP_PG)] + [page(i) for i in range(SMP_PG)],
        out_specs=per_b((1, R, A_DH)),
        scratch_shapes=[pltpu.VMEM((NL, A_DH), F32), pltpu.VMEM((NL, A_DH), F32),
                        pltpu.VMEM((R, LANES), F32), pltpu.VMEM((R, LANES), F32), pltpu.VMEM((R, A_DH), F32)])
    return pl.pallas_call(
        functools.partial(_smp_attn2_kernel, n_tok=n_tok), grid_spec=gs,
        out_shape=jax.ShapeDtypeStruct((B, R, A_DH), F32),
        compiler_params=_cparams(("parallel", "arbitrary")),
        name="smp_attn",
    )(page_table, q32, kn32, vn32, mn, mp, *([ck] * SMP_PG), *([cv] * SMP_PG))


def _merge_kernel(hm_ref, ha_ref, wm_ref, wa_ref, gm_ref, ga_ref, o_ref):
    a = jnp.dot(hm_ref[...], wm_ref[...], preferred_element_type=F32)
    b = jnp.dot(ha_ref[...], wa_ref[...], preferred_element_type=F32)
    o_ref[...] = (_sigmoid(gm_ref[...]) * a + _sigmoid(ga_ref[...]) * b).astype(o_ref.dtype)


def _merge(hm, ha, wm, wa, p32, tm):
    m = hm.shape[0]
    tn = PROJ_TN
    return pl.pallas_call(
        _merge_kernel,
        grid=(D_MODEL // tn, m // tm),
        in_specs=[pl.BlockSpec((tm, M_WIDTH), lambda j, i: (i, 0)), pl.BlockSpec((tm, A_WIDTH), lambda j, i: (i, 0)),
                  pl.BlockSpec((M_WIDTH, tn), lambda j, i: (0, j)), pl.BlockSpec((A_WIDTH, tn), lambda j, i: (0, j)),
                  pl.BlockSpec((tm, tn), lambda j, i: (i, C_GM // tn + j)),
                  pl.BlockSpec((tm, tn), lambda j, i: (i, C_GA // tn + j))],
        out_specs=pl.BlockSpec((tm, tn), lambda j, i: (i, j)),
        out_shape=jax.ShapeDtypeStruct((m, D_MODEL), BF16),
        compiler_params=_cparams(("parallel", "parallel")),
        name="merge",
    )(hm, ha, wm, wa, p32, p32)


def _outproj_kernel(mg_ref, w_ref, x_ref, o_ref):
    o_ref[...] = x_ref[...] + jnp.dot(mg_ref[...], w_ref[...], preferred_element_type=F32)


def _outproj(mg, w, x, tm):
    m = mg.shape[0]
    tn = PROJ_TN
    return pl.pallas_call(
        _outproj_kernel,
        grid=(D_MODEL // tn, m // tm),
        in_specs=[pl.BlockSpec((tm, D_MODEL), lambda j, i: (i, 0)), pl.BlockSpec((D_MODEL, tn), lambda j, i: (0, j)),
                  pl.BlockSpec((tm, tn), lambda j, i: (i, j))],
        out_specs=pl.BlockSpec((tm, tn), lambda j, i: (i, j)),
        out_shape=jax.ShapeDtypeStruct((m, D_MODEL), F32),
        compiler_params=_cparams(("parallel", "parallel")),
        name="outproj",
    )(mg, w, x)


MOE_TM = 256


def _router_kernel(x_ref, g_ref, wr_ref, br_ref, xn_ref, r_ref):
    x = x_ref[...]
    y = (x * lax.rsqrt(jnp.mean(x * x, axis=-1, keepdims=True) + RMS_EPS)) * g_ref[...]
    xn_ref[...] = y
    lg = jnp.dot(y.astype(BF16), wr_ref[...], preferred_element_type=F32) + br_ref[...]
    lane = lax.broadcasted_iota(I32, lg.shape, 1).astype(F32)
    far = float(LANES)
    gmask = lane < N_GROUPS
    gl = jnp.where(gmask, lg, NEG)
    mg = jnp.max(gl, axis=1, keepdims=True)
    p_g = 1.0 / jnp.sum(jnp.where(gmask, jnp.exp(gl - mg), 0.0), axis=1, keepdims=True)
    g_sel = jnp.min(jnp.where(gmask & (gl == mg), lane, far), axis=1, keepdims=True)
    e_lo = N_GROUPS + g_sel * EXP_PER_GROUP
    emask = (lane >= e_lo) & (lane < e_lo + EXP_PER_GROUP)
    el = jnp.where(emask, lg, NEG)
    me = jnp.max(el, axis=1, keepdims=True)
    pe = jnp.where(emask, jnp.exp(el - me), 0.0)
    probs = pe / jnp.sum(pe, axis=1, keepdims=True)
    p1 = jnp.max(probs, axis=1, keepdims=True)
    i1 = jnp.min(jnp.where(emask & (probs == p1), lane, far), axis=1, keepdims=True)
    probs2 = jnp.where(lane == i1, -1.0, probs)
    p2 = jnp.max(probs2, axis=1, keepdims=True)
    i2 = jnp.min(jnp.where(emask & (probs2 == p2), lane, far), axis=1, keepdims=True)
    tot = p1 + p2
    vals = [i1 - N_GROUPS, i2 - N_GROUPS, p_g * (p1 / tot), p_g * (p2 / tot)]
    out = jnp.zeros(lg.shape, F32)
    for c, v in enumerate(vals):
        out = jnp.where(lane == c, v, out)
    r_ref[...] = out


def _router(x1, g, wr, br, tm):
    m, d = x1.shape
    return pl.pallas_call(
        _router_kernel,
        grid=(m // tm,),
        in_specs=[pl.BlockSpec((tm, d), lambda i: (i, 0)), pl.BlockSpec((1, d), lambda i: (0, 0)),
                  pl.BlockSpec((d, LANES), lambda i: (0, 0)), pl.BlockSpec((1, LANES), lambda i: (0, 0))],
        out_specs=[pl.BlockSpec((tm, d), lambda i: (i, 0)), pl.BlockSpec((tm, LANES), lambda i: (i, 0))],
        out_shape=[jax.ShapeDtypeStruct((m, d), F32), jax.ShapeDtypeStruct((m, LANES), F32)],
        compiler_params=_cparams(("parallel",)),
        name="router",
    )(x1, g.reshape(1, d), wr, br)


def _row_copy(src_hbm, row, dst, r, sem):
    return pltpu.make_async_copy(src_hbm.at[pl.ds(row, 1), :], dst.at[pl.ds(r, 1), :], sem)


def _expert_kernel(be_ref, na_ref, src_ref, x_hbm, gate_ref, wg_ref, wu_ref, wd_ref, o_ref, xbuf, sem):
    blk = pl.program_id(0)
    slot = blk % 2

    def gather(b, s):
        def start(r, _):
            _row_copy(x_hbm, src_ref[b * MOE_TM + r], xbuf.at[s], r, sem.at[s]).start()
            return 0

        lax.fori_loop(0, MOE_TM, start, 0, unroll=8)

    @pl.when(blk == 0)
    def _prime():
        gather(0, 0)

    @pl.when(blk + 1 < na_ref[0])
    def _prefetch():
        gather(blk + 1, 1 - slot)

    @pl.when(blk < na_ref[0])
    def _active():
        def wait(r, _):
            _row_copy(x_hbm, 0, xbuf.at[slot], r, sem.at[slot]).wait()
            return 0

        lax.fori_loop(0, MOE_TM, wait, 0, unroll=8)
        x = xbuf[slot].astype(BF16)
        hg = jnp.dot(x, wg_ref[0].astype(BF16), preferred_element_type=F32)
        hu = jnp.dot(x, wu_ref[0].astype(BF16), preferred_element_type=F32)
        h = (hg * _sigmoid(hg)) * hu * gate_ref[...]
        o_ref[...] = jnp.dot(h.astype(BF16), wd_ref[0].astype(BF16), preferred_element_type=F32)

    @pl.when(blk >= na_ref[0])
    def _idle():
        o_ref[...] = jnp.zeros(o_ref.shape, F32)


def _experts(blk_exp, n_act, src, xn2, gate_sorted, w_gate, w_up, w_down):
    npad = src.shape[0]
    d = xn2.shape[1]
    last = lambda b, be, na, sr: jnp.minimum(b, na[0] - 1)
    gs = pltpu.PrefetchScalarGridSpec(
        num_scalar_prefetch=3,
        grid=(npad // MOE_TM,),
        in_specs=[pl.BlockSpec(memory_space=pl.ANY),
                  pl.BlockSpec((MOE_TM, 1), lambda b, be, na, sr: (last(b, be, na, sr), 0)),
                  pl.BlockSpec((1, d, D_EXPERT), lambda b, be, na, sr: (be[b], 0, 0)),
                  pl.BlockSpec((1, d, D_EXPERT), lambda b, be, na, sr: (be[b], 0, 0)),
                  pl.BlockSpec((1, D_EXPERT, d), lambda b, be, na, sr: (be[b], 0, 0))],
        out_specs=pl.BlockSpec((MOE_TM, d), lambda b, be, na, sr: (b, 0)),
        scratch_shapes=[pltpu.VMEM((2, MOE_TM, d), F32), pltpu.SemaphoreType.DMA((2,))])
    return pl.pallas_call(
        _expert_kernel, grid_spec=gs,
        out_shape=jax.ShapeDtypeStruct((npad, d), F32),
        compiler_params=_cparams(("arbitrary",)),
        name="experts",
    )(blk_exp, n_act, src, xn2, gate_sorted, w_gate, w_up, w_down)


def _combine_kernel(pos_ref, ys_hbm, x1_ref, g_ref, o_ref, buf, sem, *, TC, row0):
    base = (row0 + pl.program_id(0) * TC) * TOP_E

    def start(r, _):
        for s in range(TOP_E):
            _row_copy(ys_hbm, pos_ref[base + r * TOP_E + s], buf.at[s], r, sem).start()
        return 0

    def wait(r, _):
        for s in range(TOP_E):
            _row_copy(ys_hbm, 0, buf.at[s], r, sem).wait()
        return 0

    lax.fori_loop(0, TC, start, 0, unroll=8)
    lax.fori_loop(0, TC, wait, 0, unroll=8)
    x = x1_ref[...] + (buf[0] + buf[1])
    y = x * lax.rsqrt(jnp.mean(x * x, axis=-1, keepdims=True) + RMS_EPS)
    o_ref[...] = y * g_ref[...]


def _combine(pos, ys, x1, g, row0, n, TC):
    d = x1.shape[1]
    gs = pltpu.PrefetchScalarGridSpec(
        num_scalar_prefetch=1,
        grid=(n // TC,),
        in_specs=[pl.BlockSpec(memory_space=pl.ANY),
                  pl.BlockSpec((TC, d), lambda i, ps: (row0 // TC + i, 0)),
                  pl.BlockSpec((1, d), lambda i, ps: (0, 0))],
        out_specs=pl.BlockSpec((TC, d), lambda i, ps: (i, 0)),
        scratch_shapes=[pltpu.VMEM((TOP_E, TC, d), F32), pltpu.SemaphoreType.DMA(())])
    return pl.pallas_call(
        functools.partial(_combine_kernel, TC=TC, row0=row0), grid_spec=gs,
        out_shape=jax.ShapeDtypeStruct((n, d), F32),
        compiler_params=_cparams(("arbitrary",)),
        name="combine",
    )(pos, ys, x1, g.reshape(1, d))


def _route_tables(r, npad):
    nt = r.shape[0]
    ef = r[:, 0:TOP_E].astype(I32).reshape(-1)
    gf = r[:, TOP_E:2 * TOP_E].reshape(-1)
    onehot = (ef[:, None] == jnp.arange(N_EXPERTS, dtype=I32)[None, :]).astype(I32)
    csum = jnp.cumsum(onehot, axis=0)
    rank = jnp.sum(onehot * csum, axis=1) - 1
    nblk = (csum[-1] + MOE_TM - 1) // MOE_TM
    blk_end = jnp.cumsum(nblk)
    pos = (blk_end - nblk)[ef] * MOE_TM + rank
    n_act = blk_end[-1:]
    b = jnp.minimum(jnp.arange(npad // MOE_TM, dtype=I32), n_act[0] - 1)
    blk_exp = jnp.minimum(jnp.sum((blk_end[None, :] <= b[:, None]).astype(I32), axis=1), N_EXPERTS - 1)
    src = jnp.zeros((npad,), I32).at[pos].set(jnp.arange(nt * TOP_E, dtype=I32) // TOP_E)
    gate_sorted = jnp.zeros((npad,), F32).at[pos].set(gf).reshape(npad, 1)
    return blk_exp, n_act.astype(I32), src, gate_sorted, pos.astype(I32)


def kernel(x_prompt, x_sample, cache_k, cache_v, cache_kidx, state_conv, state_C, state_n, state_m, page_table,
           g_attn, w_in, b_gates_m, conv_w, conv_b, m_norm_w, w_proj_m, w_proj_a, w_out, g_ffn,
           w_rg, b_rg, w_re, b_re, w_gate, w_up, w_down, g_final):
    assert x_prompt.shape[0] == 1 and g_attn.shape[0] == 1
    l = 0
    Tp = x_prompt.shape[1]
    Bs, Ts = x_sample.shape[:2]
    Ns = Bs * Ts
    NT = Tp + Ns
    TM_BIG = 1664
    assert NT % TM_BIG == 0 and Ts >= CONV_W - 1 and Ts <= SUBLANES
    P = page_table.shape[1] * PAGE_SIZE

    x_all = jnp.concatenate([x_prompt[0], x_sample.reshape(Ns, D_MODEL)], axis=0)
    xn = _rmsnorm(x_all, g_attn[l], BF16, 640)
    p32, p16 = _inproj(xn, _prep_w_in(w_in[l]), TM_BIG)

    gate_cols = slice(C_TAIL + T_IM, C_TAIL + T_IM + 2 * M_HEADS)
    ps3 = p32[Tp:].reshape(Bs, Ts, D_CAT)
    ps8 = jnp.pad(ps3, ((0, 0), (0, SUBLANES - Ts), (0, 0)))

    zero = lambda *s: jnp.zeros(s, F32)
    hm_p, C_p, n_p, m_p = _mlstm(p32[None], p32[:Tp, gate_cols].T[None], zero(1, SUBLANES, 2 * M_WIDTH),
                                 conv_w[l], conv_b[l], b_gates_m[l], m_norm_w[l],
                                 zero(1, M_HEADS, M_DH, M_DH), zero(1, M_HEADS, M_DH), zero(1, M_HEADS),
                                 T=Tp, L=256, RIN=256, valid=256)
    grow_s = jnp.pad(jnp.swapaxes(ps3[:, :, gate_cols], 1, 2), ((0, 0), (0, 0), (0, LANES - Ts)))
    cb_s = jnp.pad(state_conv[l], ((0, 0), (SUBLANES - (CONV_W - 1), 0), (0, 0)))
    hm_s, C_s, n_s, m_s = _mlstm(ps8, grow_s, cb_s, conv_w[l], conv_b[l], b_gates_m[l], m_norm_w[l],
                                 state_C[l], state_n[l], state_m[l], T=SUBLANES, L=LANES, RIN=SUBLANES, valid=Ts)

    ha_p = _dsa_prompt(p32, p16, Tp, min(TOPK_MAX, Tp // 4))
    qs = ps3[:, :, C_QI:C_QI + IDX_HEADS * LANES].reshape(Bs, Ts, IDX_HEADS, LANES)[..., :IDX_DIM].reshape(Bs, Ts * IDX_HEADS, IDX_DIM)
    ws = ps3[:, :, C_TAIL + T_WI:C_TAIL + T_WI + IDX_HEADS].reshape(Bs, Ts * IDX_HEADS, 1)
    kin_t = jnp.pad(jnp.swapaxes(ps3[:, :, C_TAIL + T_KI:C_TAIL + T_KI + IDX_DIM], 1, 2), ((0, 0), (0, 0), (0, LANES - Ts)))
    sc = _smp_scores2(page_table, qs, ws, jnp.swapaxes(cache_kidx[l], 1, 2))
    mp, mn = _smp_select2(sc.reshape(Bs * SMP_ROWS, P), qs, ws, kin_t, Ts, min(TOPK_MAX, (P + Ts) // 4))
    rows_th = lambda c0: ps3[:, :, c0:c0 + A_WIDTH].reshape(Bs, Ts * A_HEADS, A_DH)
    ha_s = _smp_attn2(page_table, rows_th(C_QA), rows_th(C_KA), rows_th(C_VA), mn.reshape(Bs, SMP_ROWS, LANES),
                      mp.reshape(Bs, SMP_ROWS, P), cache_k[l], cache_v[l], Ts)

    hm_all = jnp.concatenate([hm_p[0], hm_s[:, :Ts].reshape(Ns, M_WIDTH)], axis=0)
    ha_all = jnp.concatenate([ha_p, ha_s.reshape(Ns, A_WIDTH).astype(BF16)], axis=0)
    merged = _merge(hm_all, ha_all, w_proj_m[l].astype(BF16), w_proj_a[l].astype(BF16), p32, TM_BIG)
    x1 = _outproj(merged, w_out[l].astype(BF16), x_all, TM_BIG)

    wr = jnp.pad(jnp.concatenate([w_rg[l], w_re[l]], axis=1), ((0, 0), (0, LANES - N_GROUPS - N_EXPERTS))).astype(BF16)
    br = jnp.pad(jnp.concatenate([b_rg[l], b_re[l]]), (0, LANES - N_GROUPS - N_EXPERTS)).reshape(1, LANES)
    xn2, r = _router(x1, g_ffn[l], wr, br, 640)
    npad = NT * TOP_E + N_EXPERTS * MOE_TM
    blk_exp, n_act, src, gate_sorted, pos = _route_tables(r, npad)
    ys = _experts(blk_exp, n_act, src, xn2, gate_sorted, w_gate[l], w_up[l], w_down[l])
    y_p = _combine(pos, ys, x1, g_final, 0, Tp, 256)
    y_s = _combine(pos, ys, x1, g_final, Tp, Ns, LANES)

    st = lambda a, shape: a.reshape((1,) + shape)
    pp = p32[:Tp]
    return (y_p[None], y_s.reshape(Bs, Ts, D_MODEL),
            st(pp[:, C_KA:C_KA + A_WIDTH], (1, Tp, A_HEADS, A_DH)), st(pp[:, C_VA:C_VA + A_WIDTH], (1, Tp, A_HEADS, A_DH)),
            st(pp[:, C_TAIL + T_KI:C_TAIL + T_KI + IDX_DIM], (1, Tp, IDX_DIM)),
            st(pp[Tp - (CONV_W - 1):, 0:2 * M_WIDTH], (1, CONV_W - 1, 2 * M_WIDTH)),
            st(C_p, (1, M_HEADS, M_DH, M_DH)), st(n_p, (1, M_HEADS, M_DH)), st(m_p[:, :, 0, 0], (1, M_HEADS)),
            st(ps3[:, :, C_KA:C_KA + A_WIDTH], (Bs, Ts, A_HEADS, A_DH)), st(ps3[:, :, C_VA:C_VA + A_WIDTH], (Bs, Ts, A_HEADS, A_DH)),
            st(ps3[:, :, C_TAIL + T_KI:C_TAIL + T_KI + IDX_DIM], (Bs, Ts, IDX_DIM)),
            st(ps3[:, Ts - (CONV_W - 1):, 0:2 * M_WIDTH], (Bs, CONV_W - 1, 2 * M_WIDTH)),
            st(C_s, (Bs, M_HEADS, M_DH, M_DH)), st(n_s, (Bs, M_HEADS, M_DH)), st(m_s[:, :, 0, 0], (Bs, M_HEADS)))
```

```python
import functools

import jax
import jax.numpy as jnp
import numpy as np
from jax import lax
from jax.experimental import pallas as pl
from jax.experimental.pallas import tpu as pltpu

F32 = jnp.float32
BF16 = jnp.bfloat16
I32 = jnp.int32

D_MODEL = 2048
M_WIDTH = D_MODEL // 2
M_HEADS = 4
M_DH = M_WIDTH // M_HEADS
CONV_W = 4
A_WIDTH = D_MODEL // 2
A_DH = 128
A_HEADS = A_WIDTH // A_DH
IDX_HEADS = 8
IDX_DIM = 64
TOPK_MAX = 256
PAGE_SIZE = 128
N_GROUPS = 4
EXP_PER_GROUP = 8
N_EXPERTS = N_GROUPS * EXP_PER_GROUP
TOP_E = 2
D_EXPERT = D_MODEL // 4
RMS_EPS = 1e-6
IN_SIZES = (M_WIDTH, M_WIDTH, M_WIDTH, M_WIDTH, M_HEADS, M_HEADS, A_WIDTH, A_WIDTH, A_WIDTH,
            IDX_HEADS * IDX_DIM, IDX_DIM, IDX_HEADS, D_MODEL, D_MODEL)
IN_SPLITS = tuple(int(s) for s in np.cumsum(IN_SIZES)[:-1])

LANES = 128
SUBLANES = 8
VMEM_LIMIT = 56 * 1024 * 1024

C_QM, C_KM, C_VM, C_OM = 0, 1024, 2048, 3072
C_QA, C_KA, C_VA = 4096, 5120, 6144
C_GM, C_GA = 7168, 9216
C_QI = 11264
C_TAIL = 12288
T_KI, T_WI, T_IM, T_FM = 0, 64, 72, 76
D_CAT = 12800
PROJ_TN = 512

NEG = -1e30
INT_MIN = -2 ** 31


def _cparams(sem):
    return pltpu.CompilerParams(dimension_semantics=sem, vmem_limit_bytes=VMEM_LIMIT)


def _rms_kernel(x_ref, g_ref, o_ref):
    x = x_ref[...]
    y = x * lax.rsqrt(jnp.mean(x * x, axis=-1, keepdims=True) + RMS_EPS)
    o_ref[...] = (y * g_ref[...]).astype(o_ref.dtype)


def _rmsnorm(x, g, out_dtype, tm):
    m, d = x.shape
    return pl.pallas_call(
        _rms_kernel,
        grid=(m // tm,),
        in_specs=[pl.BlockSpec((tm, d), lambda i: (i, 0)), pl.BlockSpec((1, d), lambda i: (0, 0))],
        out_specs=pl.BlockSpec((tm, d), lambda i: (i, 0)),
        out_shape=jax.ShapeDtypeStruct((m, d), out_dtype),
        compiler_params=_cparams(("parallel",)),
        name="rmsnorm",
    )(x, g.reshape(1, d))


def _inproj_kernel(x_ref, wt_ref, o32_ref, o16_ref):
    acc = lax.dot_general(x_ref[...], wt_ref[...], (((1,), (1,)), ((), ())), preferred_element_type=F32)
    o32_ref[...] = acc
    o16_ref[...] = acc.astype(BF16)


def _inproj(xn, w_cat_t, tm):
    m, d = xn.shape
    n = w_cat_t.shape[0]
    tn = PROJ_TN
    return pl.pallas_call(
        _inproj_kernel,
        grid=(m // tm, n // tn),
        in_specs=[pl.BlockSpec((tm, d), lambda i, j: (i, 0)), pl.BlockSpec((tn, d), lambda i, j: (j, 0))],
        out_specs=[pl.BlockSpec((tm, tn), lambda i, j: (i, j)), pl.BlockSpec((tm, tn), lambda i, j: (i, j))],
        out_shape=[jax.ShapeDtypeStruct((m, n), F32), jax.ShapeDtypeStruct((m, n), BF16)],
        compiler_params=_cparams(("parallel", "parallel")),
        name="inproj",
    )(xn, w_cat_t)


def _prep_w_in(w_in):
    (q_m, k_m, v_m, o_m, i_m, f_m, q_a, k_a, v_a, q_i, k_i, w_i, g_m, g_a) = jnp.split(w_in.T, IN_SPLITS, axis=0)
    d = w_in.shape[0]
    q_i = jnp.pad(q_i.reshape(IDX_HEADS, IDX_DIM, d), ((0, 0), (0, LANES - IDX_DIM), (0, 0))).reshape(IDX_HEADS * LANES, d)
    rows = [q_m, k_m, v_m, o_m, q_a, k_a, v_a, g_m, g_a, q_i, k_i, w_i, i_m, f_m]
    w = jnp.concatenate(rows, axis=0)
    return jnp.pad(w, ((0, D_CAT - w.shape[0]), (0, 0))).astype(BF16)


def _sigmoid(x):
    return 1.0 / (1.0 + jnp.exp(-x))


def _log_sigmoid(x):
    return jnp.minimum(x, 0.0) - jnp.log1p(jnp.exp(-jnp.abs(x)))


def _mlstm_kernel(q_ref, k_ref, v_ref, o_ref, tail_ref, grow_ref, cb_ref, convw_ref, convb_ref, bl_ref, bs_ref,
                  nw_ref, c0_ref, n0_ref, m0_ref,
                  h_ref, cout_ref, nout_ref, mout_ref,
                  xq_s, xk_s, c_s, n_s, m_s, vp_s, op_s, tp_s, *, L, RIN, valid):
    c = pl.program_id(1)
    nc = pl.num_programs(1)

    @pl.when(c == 0)
    def _init():
        xq_s[0:SUBLANES, :] = cb_ref[0, :, 0:M_WIDTH]
        xk_s[0:SUBLANES, :] = cb_ref[0, :, M_WIDTH:2 * M_WIDTH]
        c_s[...] = c0_ref[0]
        n_s[...] = n0_ref[0]
        m_s[...] = m0_ref[0]

    if RIN < L:
        zpad = jnp.zeros((L - RIN, M_WIDTH), F32)
        xq_s[SUBLANES + RIN:SUBLANES + L, :] = zpad
        xk_s[SUBLANES + RIN:SUBLANES + L, :] = zpad
        vp_s[RIN:L, :] = zpad
        op_s[RIN:L, :] = zpad
        tp_s[RIN:L, :] = jnp.zeros((L - RIN, LANES), F32)
    xq_s[SUBLANES:SUBLANES + RIN, :] = q_ref[0]
    xk_s[SUBLANES:SUBLANES + RIN, :] = k_ref[0]
    vp_s[0:RIN, :] = v_ref[0]
    op_s[0:RIN, :] = o_ref[0]
    tp_s[0:RIN, :] = tail_ref[0]

    def conv(xs, col0):
        w = convw_ref[:, col0:col0 + M_WIDTH]
        y = convb_ref[:, col0:col0 + M_WIDTH]
        for j in range(CONV_W):
            r0 = SUBLANES - (CONV_W - 1) + j
            y = y + xs[r0:r0 + L, :] * w[j:j + 1, :]
        return y * _sigmoid(y)

    q_all = conv(xq_s, 0) * (M_DH ** -0.5)
    k_all = conv(xk_s, M_WIDTH)
    v_all = vp_s[...]
    tail = tp_s[...] + bl_ref[...]
    grow = grow_ref[0] + bs_ref[...]

    xq_s[0:SUBLANES, :] = xq_s[L:L + SUBLANES, :]
    xk_s[0:SUBLANES, :] = xk_s[L:L + SUBLANES, :]

    tt = lax.broadcasted_iota(I32, (L, L), 0)
    ss = lax.broadcasted_iota(I32, (L, L), 1)
    causal = ss <= tt
    row_ok = lax.broadcasted_iota(I32, (L, 1), 0) < valid
    col_ok = lax.broadcasted_iota(I32, (1, L), 1) < valid

    for h in range(M_HEADS):
        hs = slice(h * M_DH, (h + 1) * M_DH)
        qh = q_all[:, hs]
        kh = k_all[:, hs]
        vh = v_all[:, hs]
        ig_c = jnp.where(row_ok, tail[:, T_IM + h:T_IM + h + 1], NEG)
        lf_c = jnp.where(row_ok, _log_sigmoid(tail[:, T_FM + h:T_FM + h + 1]), 0.0)
        ig_r = jnp.where(col_ok, grow[h:h + 1, :], NEG)
        lf_r = jnp.where(col_ok, _log_sigmoid(grow[M_HEADS + h:M_HEADS + h + 1, :]), 0.0)
        b_c = jnp.sum(jnp.where(causal, lf_r, 0.0), axis=1, keepdims=True)
        b_r = jnp.sum(jnp.where(tt <= ss, lf_c, 0.0), axis=0, keepdims=True)
        dmat = jnp.where(causal, b_c - b_r + ig_r, NEG)
        m_prev = m_s[h][:, 0:1]
        m_t = jnp.maximum(b_c + m_prev, jnp.max(dmat, axis=1, keepdims=True))
        e = jnp.exp(dmat - m_t)
        qb = qh.astype(BF16)
        kb = kh.astype(BF16)
        s = lax.dot_general(qb, kb, (((1,), (1,)), ((), ())), preferred_element_type=F32) * e
        inter = jnp.exp(b_c + m_prev - m_t)
        ch = c_s[h]
        num = jnp.dot(s.astype(BF16), vh.astype(BF16), preferred_element_type=F32) + inter * lax.dot_general(
            qb, ch.astype(BF16), (((1,), (1,)), ((), ())), preferred_element_type=F32)
        nh = n_s[h]
        den = jnp.sum(s, axis=1, keepdims=True) + inter * jnp.sum(qh * nh, axis=1, keepdims=True)
        hh = num / jnp.maximum(jnp.abs(den), jnp.exp(-m_t))
        hh = hh * lax.rsqrt(jnp.mean(hh * hh, axis=1, keepdims=True) + RMS_EPS)
        out = _sigmoid(op_s[:, hs]) * (hh * nw_ref[:, hs])
        h_ref[0, :, hs] = out[0:RIN, :].astype(h_ref.dtype)
        m_new = m_t[L - 1:L, :]
        b_last = b_c[L - 1:L, :]
        w_c = jnp.exp(b_last - b_c + ig_c - m_new)
        decay = jnp.exp(b_last + m_prev - m_new)
        upd = lax.dot_general((w_c * vh).astype(BF16), kb, (((0,), (0,)), ((), ())), preferred_element_type=F32)
        c_s[h] = decay * ch + upd
        n_s[h] = decay * nh + jnp.sum(w_c * kh, axis=0, keepdims=True)
        m_s[h] = jnp.broadcast_to(m_new, (1, LANES))

    @pl.when(c == nc - 1)
    def _fin():
        cout_ref[0] = c_s[...]
        nout_ref[0] = n_s[...]
        mout_ref[0] = m_s[...]


def _mlstm(p32, grow, convbuf, conv_w, conv_b, b_gates, m_norm_w, c0, n0, m0, *, T, L, RIN, valid):
    B = p32.shape[0]
    nc = T // RIN
    bl = jnp.zeros((1, LANES), F32).at[0, T_IM:T_IM + 2 * M_HEADS].set(b_gates)
    bs = jnp.broadcast_to(b_gates[:, None], (2 * M_HEADS, L))
    kern = functools.partial(_mlstm_kernel, L=L, RIN=RIN, valid=valid)
    cblk = lambda col: pl.BlockSpec((1, RIN, M_WIDTH), lambda b, c, col=col: (b, c, col // M_WIDTH))
    const2 = lambda shape: pl.BlockSpec(shape, lambda b, c: (0, 0))
    per_b = lambda shape: pl.BlockSpec(shape, lambda b, c: (b,) + (0,) * (len(shape) - 1))
    return pl.pallas_call(
        kern,
        grid=(B, nc),
        in_specs=[cblk(C_QM), cblk(C_KM), cblk(C_VM), cblk(C_OM),
                  pl.BlockSpec((1, RIN, LANES), lambda b, c: (b, c, C_TAIL // LANES)),
                  pl.BlockSpec((1, 2 * M_HEADS, L), lambda b, c: (b, 0, c)),
                  per_b((1, SUBLANES, 2 * M_WIDTH)),
                  const2((CONV_W, 2 * M_WIDTH)), const2((1, 2 * M_WIDTH)), const2((1, LANES)),
                  const2((2 * M_HEADS, L)), const2((1, M_WIDTH)),
                  per_b((1, M_HEADS, M_DH, M_DH)), per_b((1, M_HEADS, 1, M_DH)), per_b((1, M_HEADS, 1, LANES))],
        out_specs=[pl.BlockSpec((1, RIN, M_WIDTH), lambda b, c: (b, c, 0)),
                   per_b((1, M_HEADS, M_DH, M_DH)), per_b((1, M_HEADS, 1, M_DH)), per_b((1, M_HEADS, 1, LANES))],
        out_shape=[jax.ShapeDtypeStruct((B, T, M_WIDTH), BF16),
                   jax.ShapeDtypeStruct((B, M_HEADS, M_DH, M_DH), F32),
                   jax.ShapeDtypeStruct((B, M_HEADS, 1, M_DH), F32),
                   jax.ShapeDtypeStruct((B, M_HEADS, 1, LANES), F32)],
        scratch_shapes=[pltpu.VMEM((SUBLANES + L, M_WIDTH), F32), pltpu.VMEM((SUBLANES + L, M_WIDTH), F32),
                        pltpu.VMEM((M_HEADS, M_DH, M_DH), F32), pltpu.VMEM((M_HEADS, 1, M_DH), F32),
                        pltpu.VMEM((M_HEADS, 1, LANES), F32),
                        pltpu.VMEM((L, M_WIDTH), F32), pltpu.VMEM((L, M_WIDTH), F32), pltpu.VMEM((L, LANES), F32)],
        compiler_params=_cparams(("parallel", "arbitrary")),
        name="mlstm",
    )(p32, p32, p32, p32, p32, grow, convbuf, conv_w, conv_b.reshape(1, -1), bl, bs, m_norm_w.reshape(1, -1),
      c0, n0.reshape(B, M_HEADS, 1, M_DH), jnp.broadcast_to(m0[:, :, None, None], (B, M_HEADS, 1, LANES)))


def _score_key(sc):
    bits = lax.bitcast_convert_type(sc, I32)
    return jnp.where(bits < 0, INT_MIN - bits, bits)


DSA_TQ = 512
DSA_TK = 512
DSA_RG = 64
DSA_RB = 256
LOG2E = 1.4426950408889634


def _dsa_prompt_kernel(qi_tab, kj_tab, qidx_ref, tail_ref, kit_ref, qa_ref, ka_ref, va_ref, o_ref,
                       keys_s, cand_s, cnt_s, thr_s, thrm_s, cut_s, bias_s, tie_s, wrep_s, m_s, l_s, acc_s,
                       *, TQ, TK, topk, pos_bits):
    step = pl.program_id(0)
    qi = qi_tab[step]
    kj = kj_tab[step]
    RG = DSA_RG
    nlc = TK // LANES

    def count_pass(pred):
        def rbody(r, _):
            r0 = pl.multiple_of(r * RG, RG)
            cand = cand_s[pl.ds(r0, RG), :]
            aux = thr_s[pl.ds(r0, RG), :]

            def kb(j, cnt):
                for c in range(nlc):
                    blk = keys_s[j, pl.ds(r0, RG), c * LANES:(c + 1) * LANES]
                    pos = j * TK + c * LANES + lax.broadcasted_iota(I32, (RG, LANES), 1)
                    cnt = cnt + jnp.where(pred(blk, cand, aux, pos), 1.0, 0.0)
                return cnt

            cnt_s[pl.ds(r0, RG), :] = lax.fori_loop(0, qi + 1, kb, jnp.zeros((RG, LANES), F32))
            return 0

        lax.fori_loop(0, TQ // RG, rbody, 0)
        return jnp.sum(cnt_s[...], axis=1, keepdims=True)

    @pl.when(kj == 0)
    def _phase1():
        w = tail_ref[:, T_WI:T_WI + IDX_HEADS] * (IDX_HEADS ** -0.5) * (IDX_DIM ** -0.5)
        for h in range(IDX_HEADS):
            wrep_s[h] = jnp.broadcast_to(w[:, h:h + 1], (TQ, LANES))
        rowpos = qi * TQ + lax.broadcasted_iota(I32, (TQ, TK), 0)

        def kbody(j, _):
            kt = kit_ref[j].astype(BF16)
            sc = jnp.zeros((TQ, TK), F32)
            for h in range(IDX_HEADS):
                qh = qidx_ref[:, h * LANES:(h + 1) * LANES].astype(BF16)
                s = jnp.dot(qh, kt, preferred_element_type=F32)
                sc = sc + jnp.maximum(s, 0.0) * jnp.concatenate([wrep_s[h]] * nlc, axis=1)
            colpos = j * TK + lax.broadcasted_iota(I32, (TQ, TK), 1)
            keys_s[j] = jnp.where(colpos <= rowpos, _score_key(sc), INT_MIN)
            return 0

        lax.fori_loop(0, qi + 1, kbody, 0)

        ge = lambda blk, cand, aux, pos: blk >= cand
        cand_s[...] = jnp.zeros((TQ, LANES), I32)
        cnt = count_pass(ge)
        lo = jnp.where(cnt >= topk, 0, INT_MIN).astype(I32)

        def bit_body(b, lo):
            cand = lo | lax.shift_left(jnp.int32(1), 30 - b)
            cand_s[...] = jnp.broadcast_to(cand, (TQ, LANES))
            cnt = count_pass(ge)
            return jnp.where(cnt >= topk, cand, lo)

        thr = lax.fori_loop(0, 31, bit_body, lo)
        thr_s[...] = jnp.broadcast_to(thr, (TQ, LANES))
        cand_s[...] = thr_s[...]
        n_gt = count_pass(lambda blk, cand, aux, pos: blk > cand)
        n_ge = count_pass(ge)
        need = topk - n_gt
        short = thr == INT_MIN
        thrm_s[...] = jnp.broadcast_to(jnp.where(short, INT_MIN, thr - 1), (TQ, LANES))
        cut_s[...] = jnp.broadcast_to(jnp.where(short, -1, 2 ** 30).astype(I32), (TQ, LANES))
        surplus = jnp.max(jnp.where((n_ge - n_gt > need) & jnp.logical_not(short), 1.0, 0.0))
        tie_s[0] = (surplus > 0.0).astype(I32)

        @pl.when(surplus > 0.0)
        def _ties():
            eq_below = lambda blk, cand, aux, pos: (blk == aux) & (pos < cand)

            def tie_body(b, x):
                cand = x + lax.shift_left(jnp.int32(1), pos_bits - 1 - b)
                cand_s[...] = jnp.broadcast_to(cand, (TQ, LANES))
                cnt = count_pass(eq_below)
                return jnp.where(cnt < need, cand, x)

            x = lax.fori_loop(0, pos_bits, tie_body, jnp.zeros((TQ, 1), I32))
            cut_s[...] = jnp.broadcast_to(jnp.where(short, -1, x), (TQ, LANES))

        m_s[...] = jnp.full(m_s.shape, NEG, F32)
        l_s[...] = jnp.zeros(l_s.shape, F32)
        acc_s[...] = jnp.zeros(acc_s.shape, F32)

    tile_l = lambda a: jnp.concatenate([a] * nlc, axis=1)

    @pl.when(tie_s[0] == 0)
    def _bias_plain():
        bias_s[...] = jnp.where(keys_s[kj] > tile_l(thrm_s[...]), 0.0, NEG)

    @pl.when(tie_s[0] != 0)
    def _bias_tied():
        key = keys_s[kj]
        thr = tile_l(thr_s[...])
        colpos = kj * TK + lax.broadcasted_iota(I32, (TQ, TK), 1)
        sel = (key > thr) | ((key == thr) & (colpos <= tile_l(cut_s[...])))
        bias_s[...] = jnp.where(sel, 0.0, NEG)

    RB = DSA_RB
    c1 = (A_DH ** -0.5) * LOG2E

    def rb_body(rb, _):
        rows = pl.ds(pl.multiple_of(rb * RB, RB), RB)
        bias = bias_s[rows, :]
        hsl = [slice(h * A_DH, (h + 1) * A_DH) for h in range(A_HEADS)]
        m_old = [m_s[h, rows, :] for h in range(A_HEADS)]
        l_old = [l_s[h, rows, :] for h in range(A_HEADS)]
        a_old = [acc_s[rows, hs] for hs in hsl]
        ts = [lax.dot_general(qa_ref[rows, hs], ka_ref[:, hs], (((1,), (1,)), ((), ())), preferred_element_type=F32) * c1 + bias
              for hs in hsl]
        m_new = [jnp.maximum(m_old[h], jnp.max(ts[h], axis=1, keepdims=True)) for h in range(A_HEADS)]
        l_new, a_new = [], []
        for h in range(A_HEADS):
            alpha = jnp.exp2(m_old[h] - m_new[h])
            p = jnp.exp2(ts[h] - tile_l(m_new[h]))
            psum = p[:, 0:LANES]
            for c in range(1, nlc):
                psum = psum + p[:, c * LANES:(c + 1) * LANES]
            l_new.append(alpha * l_old[h] + psum)
            a_new.append(alpha * a_old[h] + jnp.dot(p.astype(BF16), va_ref[:, hsl[h]], preferred_element_type=F32))
        for h in range(A_HEADS):
            m_s[h, rows, :] = m_new[h]
            l_s[h, rows, :] = l_new[h]
            acc_s[rows, hsl[h]] = a_new[h]
        return 0

    lax.fori_loop(0, TQ // RB, rb_body, 0)

    @pl.when(kj == qi)
    def _fin():
        for h in range(A_HEADS):
            hs = slice(h * A_DH, (h + 1) * A_DH)
            o_ref[:, hs] = (acc_s[:, hs] / jnp.sum(l_s[h], axis=1, keepdims=True)).astype(o_ref.dtype)


def _dsa_prompt(p32, p16, T, topk):
    TQ, TK = DSA_TQ, DSA_TK
    nq = T // TQ
    assert TQ == TK
    qi_tab = np.concatenate([np.full(i + 1, i) for i in range(nq)]).astype(np.int32)
    kj_tab = np.concatenate([np.arange(i + 1) for i in range(nq)]).astype(np.int32)
    ki = p32[:T, C_TAIL + T_KI:C_TAIL + T_KI + IDX_DIM]
    kit = jnp.pad(ki.T, ((0, LANES - IDX_DIM), (0, 0))).reshape(LANES, T // TK, TK).transpose(1, 0, 2)
    kern = functools.partial(_dsa_prompt_kernel, TQ=TQ, TK=TK, topk=topk, pos_bits=int(T - 1).bit_length())
    gs = pltpu.PrefetchScalarGridSpec(
        num_scalar_prefetch=2,
        grid=(len(qi_tab),),
        in_specs=[pl.BlockSpec((TQ, IDX_HEADS * LANES), lambda s, qt, kt: (qt[s], C_QI // (IDX_HEADS * LANES))),
                  pl.BlockSpec((TQ, LANES), lambda s, qt, kt: (qt[s], C_TAIL // LANES)),
                  pl.BlockSpec((T // TK, LANES, TK), lambda s, qt, kt: (0, 0, 0)),
                  pl.BlockSpec((TQ, A_WIDTH), lambda s, qt, kt: (qt[s], C_QA // A_WIDTH)),
                  pl.BlockSpec((TK, A_WIDTH), lambda s, qt, kt: (kt[s], C_KA // A_WIDTH)),
                  pl.BlockSpec((TK, A_WIDTH), lambda s, qt, kt: (kt[s], C_VA // A_WIDTH))],
        out_specs=pl.BlockSpec((TQ, A_WIDTH), lambda s, qt, kt: (qt[s], 0)),
        scratch_shapes=[pltpu.VMEM((T // TK, TQ, TK), I32),
                        pltpu.VMEM((TQ, LANES), I32), pltpu.VMEM((TQ, LANES), F32), pltpu.VMEM((TQ, LANES), I32),
                        pltpu.VMEM((TQ, LANES), I32), pltpu.VMEM((TQ, LANES), I32),
                        pltpu.VMEM((TQ, TK), F32), pltpu.SMEM((1,), I32), pltpu.VMEM((IDX_HEADS, TQ, LANES), F32),
                        pltpu.VMEM((A_HEADS, TQ, LANES), F32), pltpu.VMEM((A_HEADS, TQ, LANES), F32),
                        pltpu.VMEM((TQ, A_WIDTH), F32)])
    return pl.pallas_call(
        kern, grid_spec=gs,
        out_shape=jax.ShapeDtypeStruct((T, A_WIDTH), BF16),
        compiler_params=_cparams(("arbitrary",)),
        name="dsa_prompt",
    )(jnp.asarray(qi_tab), jnp.asarray(kj_tab), p32, p32, kit, p16, p16, p16)


SMP_PGS = 16
SMP_PG = 8
SMP_ROWS = SUBLANES
SMP_SEL_B = 16


def _idx_scores_t(q, w, kt):
    s = jnp.dot(q.astype(BF16), kt.astype(BF16), preferred_element_type=F32)
    s = jnp.maximum(s * (IDX_DIM ** -0.5), 0.0) * (w * (IDX_HEADS ** -0.5))
    n_tok = q.shape[0] // IDX_HEADS
    rows = [jnp.sum(s[t * IDX_HEADS:(t + 1) * IDX_HEADS], axis=0, keepdims=True) for t in range(n_tok)]
    rows.append(jnp.zeros((SMP_ROWS - n_tok, s.shape[1]), F32))
    return jnp.concatenate(rows, axis=0)


def _smp_scores2_kernel(pt_ref, q_ref, w_ref, *refs):
    o_ref = refs[-1]
    kt = jnp.concatenate([r[0] for r in refs[:-1]], axis=1)
    o_ref[0] = _idx_scores_t(q_ref[0], w_ref[0], kt)


def _smp_scores2(page_table, qs, ws, kidx_t):
    B, n_pages = page_table.shape
    R = qs.shape[1]
    page = lambda i: pl.BlockSpec((1, IDX_DIM, PAGE_SIZE), lambda b, p, pt: (pt[b, p * SMP_PGS + i], 0, 0))
    gs = pltpu.PrefetchScalarGridSpec(
        num_scalar_prefetch=1,
        grid=(B, n_pages // SMP_PGS),
        in_specs=[pl.BlockSpec((1, R, IDX_DIM), lambda b, p, pt: (b, 0, 0)),
                  pl.BlockSpec((1, R, 1), lambda b, p, pt: (b, 0, 0))] + [page(i) for i in range(SMP_PGS)],
        out_specs=pl.BlockSpec((1, SMP_ROWS, SMP_PGS * PAGE_SIZE), lambda b, p, pt: (b, 0, p)))
    return pl.pallas_call(
        _smp_scores2_kernel, grid_spec=gs,
        out_shape=jax.ShapeDtypeStruct((B, SMP_ROWS, n_pages * PAGE_SIZE), F32),
        compiler_params=_cparams(("parallel", "arbitrary")),
        name="smp_scores",
    )(page_table, qs, ws, *([kidx_t] * SMP_PGS))


def _smp_select2_kernel(sc_ref, q_ref, w_ref, kint_ref, mp_ref, mn_ref, keys_s, cand_s, thr_s, cnt_s,
                        *, n_tok, topk):
    NB = q_ref.shape[0]
    R = NB * SMP_ROWS
    P = sc_ref.shape[1]
    NCH = P // LANES
    RG = DSA_RG
    lane = lax.broadcasted_iota(I32, (R, LANES), 1)
    trow = lax.broadcasted_iota(I32, (R, LANES), 0) % SMP_ROWS
    for c in range(NCH):
        keys_s[c] = _score_key(sc_ref[:, c * LANES:(c + 1) * LANES])
    s_new = jnp.concatenate([_idx_scores_t(q_ref[b], w_ref[b], kint_ref[b]) for b in range(NB)], axis=0)
    keys_s[NCH] = jnp.where(lane <= trow, _score_key(s_new), INT_MIN)

    def count_pass(pred):
        def rbody(r, _):
            rows = pl.ds(pl.multiple_of(r * RG, RG), RG)
            cand = cand_s[rows, :]
            aux = thr_s[rows, :]

            def cb(c, cnt):
                pos = c * LANES + lax.broadcasted_iota(I32, (RG, LANES), 1)
                return cnt + jnp.where(pred(keys_s[c, rows, :], cand, aux, pos), 1.0, 0.0)

            cnt_s[rows, :] = lax.fori_loop(0, NCH + 1, cb, jnp.zeros((RG, LANES), F32))
            return 0

        lax.fori_loop(0, R // RG, rbody, 0)
        return jnp.sum(cnt_s[...], axis=1, keepdims=True)

    ge = lambda blk, cand, aux, pos: blk >= cand
    cand_s[...] = jnp.zeros((R, LANES), I32)
    lo = jnp.where(count_pass(ge) >= topk, 0, INT_MIN).astype(I32)

    def bit_body(b, lo):
        cand = lo | lax.shift_left(jnp.int32(1), 30 - b)
        cand_s[...] = jnp.broadcast_to(cand, (R, LANES))
        return jnp.where(count_pass(ge) >= topk, cand, lo)

    thr = lax.fori_loop(0, 31, bit_body, lo)
    thr_s[...] = jnp.broadcast_to(thr, (R, LANES))
    cand_s[...] = thr_s[...]
    need = topk - count_pass(lambda blk, cand, aux, pos: blk > cand)
    pos_bits = int(P + LANES - 1).bit_length()
    eq_below = lambda blk, cand, aux, pos: (blk == aux) & (pos < cand)

    def tie_body(b, x):
        cand = x + lax.shift_left(jnp.int32(1), pos_bits - 1 - b)
        cand_s[...] = jnp.broadcast_to(cand, (R, LANES))
        return jnp.where(count_pass(eq_below) < need, cand, x)

    cut = jnp.broadcast_to(lax.fori_loop(0, pos_bits, tie_body, jnp.zeros((R, 1), I32)), (R, LANES))
    thr_b = thr_s[...]
    row_ok = trow < n_tok
    for c in range(NCH + 1):
        key = keys_s[c]
        sel = ((key > thr_b) | ((key == thr_b) & (c * LANES + lane <= cut))) & (key != INT_MIN) & row_ok
        if c < NCH:
            mp_ref[:, c * LANES:(c + 1) * LANES] = jnp.where(sel, 1.0, 0.0)
        else:
            mn_ref[...] = jnp.where(sel, 1.0, 0.0)


def _smp_select2(sc, qs, ws, kin_t, n_tok, topk):
    R, P = sc.shape
    B = qs.shape[0]
    NB = SMP_SEL_B
    RS = NB * SMP_ROWS
    Rq = qs.shape[1]
    return pl.pallas_call(
        functools.partial(_smp_select2_kernel, n_tok=n_tok, topk=topk),
        grid=(B // NB,),
        in_specs=[pl.BlockSpec((RS, P), lambda i: (i, 0)),
                  pl.BlockSpec((NB, Rq, IDX_DIM), lambda i: (i, 0, 0)), pl.BlockSpec((NB, Rq, 1), lambda i: (i, 0, 0)),
                  pl.BlockSpec((NB, IDX_DIM, LANES), lambda i: (i, 0, 0))],
        out_specs=[pl.BlockSpec((RS, P), lambda i: (i, 0)), pl.BlockSpec((RS, LANES), lambda i: (i, 0))],
        out_shape=[jax.ShapeDtypeStruct((R, P), F32), jax.ShapeDtypeStruct((R, LANES), F32)],
        scratch_shapes=[pltpu.VMEM((P // LANES + 1, RS, LANES), I32), pltpu.VMEM((RS, LANES), I32),
                        pltpu.VMEM((RS, LANES), I32), pltpu.VMEM((RS, LANES), F32)],
        compiler_params=_cparams(("parallel",)),
        name="smp_select",
    )(sc, qs, ws, kin_t)


def _smp_attn2_kernel(pt_ref, q_ref, kn_ref, vn_ref, mn_ref, mp_ref, *refs, n_tok):
    k_refs = refs[:SMP_PG]
    v_refs = refs[SMP_PG:2 * SMP_PG]
    o_ref, kpad_s, vpad_s, m_s, l_s, acc_s = refs[2 * SMP_PG:]
    p = pl.program_id(1)
    NL = PAGE_SIZE * A_HEADS
    R = n_tok * A_HEADS
    c1 = (A_DH ** -0.5) * LOG2E
    diag = jnp.where(lax.broadcasted_iota(I32, (A_HEADS, NL), 1) % A_HEADS == lax.broadcasted_iota(I32, (A_HEADS, NL), 0), 1.0, 0.0)
    expand = jnp.where(lax.broadcasted_iota(I32, (PAGE_SIZE, NL), 1) // A_HEADS == lax.broadcasted_iota(I32, (PAGE_SIZE, NL), 0),
                       1.0, 0.0).astype(BF16)
    qb = q_ref[0].astype(BF16)

    def attend(k_list, v_list, masks):
        n = len(k_list)
        x = jnp.dot(jnp.concatenate(masks, axis=0).astype(BF16), expand, preferred_element_type=F32)
        ts = []
        for i in range(n):
            s = lax.dot_general(qb, k_list[i], (((1,), (1,)), ((), ())), preferred_element_type=F32)
            ok = jnp.concatenate([jnp.broadcast_to(x[i * SMP_ROWS + t:i * SMP_ROWS + t + 1, :], (A_HEADS, NL)) * diag
                                  for t in range(n_tok)], axis=0)
            ts.append(s * c1 + jnp.where(ok > 0.5, 0.0, NEG))
        m_old = m_s[...]
        m_new = jnp.maximum(m_old, jnp.max(jnp.concatenate(ts, axis=1), axis=1, keepdims=True))
        alpha = jnp.exp2(m_old - m_new)
        acc = alpha * acc_s[...]
        lsum = alpha * l_s[...]
        m_t = jnp.concatenate([m_new] * (NL // LANES), axis=1)
        for i in range(n):
            pr = jnp.exp2(ts[i] - m_t)
            for c in range(NL // LANES):
                lsum = lsum + pr[:, c * LANES:(c + 1) * LANES]
            acc = acc + jnp.dot(pr.astype(BF16), v_list[i], preferred_element_type=F32)
        m_s[...] = m_new
        l_s[...] = lsum
        acc_s[...] = acc

    @pl.when(p == 0)
    def _first():
        m_s[...] = jnp.full(m_s.shape, NEG, F32)
        l_s[...] = jnp.zeros(l_s.shape, F32)
        acc_s[...] = jnp.zeros(acc_s.shape, F32)
        kpad_s[...] = jnp.zeros(kpad_s.shape, F32)
        vpad_s[...] = jnp.zeros(vpad_s.shape, F32)
        kpad_s[0:R, :] = kn_ref[0]
        vpad_s[0:R, :] = vn_ref[0]
        attend([kpad_s[...].astype(BF16)], [vpad_s[...].astype(BF16)], [mn_ref[0]])

    attend([r[0].reshape(NL, A_DH).astype(BF16) for r in k_refs], [r[0].reshape(NL, A_DH).astype(BF16) for r in v_refs],
           [mp_ref[0][:, i * PAGE_SIZE:(i + 1) * PAGE_SIZE] for i in range(SMP_PG)])

    @pl.when(p == pl.num_programs(1) - 1)
    def _fin():
        o_ref[0] = acc_s[...] / jnp.sum(l_s[...], axis=1, keepdims=True)


def _smp_attn2(page_table, q32, kn32, vn32, mn, mp, ck, cv, n_tok):
    B, n_pages = page_table.shape
    R = n_tok * A_HEADS
    NL = PAGE_SIZE * A_HEADS
    per_b = lambda shape: pl.BlockSpec(shape, lambda b, p, pt: (b, 0, 0))
    page = lambda i: pl.BlockSpec((1, PAGE_SIZE, A_HEADS, A_DH), lambda b, p, pt: (pt[b, p * SMP_PG + i], 0, 0, 0))
    gs = pltpu.PrefetchScalarGridSpec(
        num_scalar_prefetch=1,
        grid=(B, n_pages // SMP_PG),
        in_specs=[per_b((1, R, A_DH)), per_b((1, R, A_DH)), per_b((1, R, A_DH)), per_b((1, SMP_ROWS, LANES)),
                  pl.BlockSpec((1, SMP_ROWS, SMP_PG * PAGE_SIZE), lambda b, p, pt: (b, 0, p))]
                 + [page(i) for i in range(SMP_PG)] + [page(i) for i in range(SMP_PG)],
        out_specs=per_b((1, R, A_DH)),
        scratch_shapes=[pltpu.VMEM((NL, A_DH), F32), pltpu.VMEM((NL, A_DH), F32),
                        pltpu.VMEM((R, LANES), F32), pltpu.VMEM((R, LANES), F32), pltpu.VMEM((R, A_DH), F32)])
    return pl.pallas_call(
        functools.partial(_smp_attn2_kernel, n_tok=n_tok), grid_spec=gs,
        out_shape=jax.ShapeDtypeStruct((B, R, A_DH), F32),
        compiler_params=_cparams(("parallel", "arbitrary")),
        name="smp_attn",
    )(page_table, q32, kn32, vn32, mn, mp, *([ck] * SMP_PG), *([cv] * SMP_PG))


def _merge_kernel(hm_ref, ha_ref, wm_ref, wa_ref, gm_ref, ga_ref, o_ref):
    a = jnp.dot(hm_ref[...], wm_ref[...], preferred_element_type=F32)
    b = jnp.dot(ha_ref[...], wa_ref[...], preferred_element_type=F32)
    o_ref[...] = (_sigmoid(gm_ref[...]) * a + _sigmoid(ga_ref[...]) * b).astype(o_ref.dtype)


def _merge(hm, ha, wm, wa, p32, tm):
    m = hm.shape[0]
    tn = PROJ_TN
    return pl.pallas_call(
        _merge_kernel,
        grid=(D_MODEL // tn, m // tm),
        in_specs=[pl.BlockSpec((tm, M_WIDTH), lambda j, i: (i, 0)), pl.BlockSpec((tm, A_WIDTH), lambda j, i: (i, 0)),
                  pl.BlockSpec((M_WIDTH, tn), lambda j, i: (0, j)), pl.BlockSpec((A_WIDTH, tn), lambda j, i: (0, j)),
                  pl.BlockSpec((tm, tn), lambda j, i: (i, C_GM // tn + j)),
                  pl.BlockSpec((tm, tn), lambda j, i: (i, C_GA // tn + j))],
        out_specs=pl.BlockSpec((tm, tn), lambda j, i: (i, j)),
        out_shape=jax.ShapeDtypeStruct((m, D_MODEL), BF16),
        compiler_params=_cparams(("parallel", "parallel")),
        name="merge",
    )(hm, ha, wm, wa, p32, p32)


def _outproj_kernel(mg_ref, w_ref, x_ref, o_ref):
    o_ref[...] = x_ref[...] + jnp.dot(mg_ref[...], w_ref[...], preferred_element_type=F32)


def _outproj(mg, w, x, tm):
    m = mg.shape[0]
    tn = PROJ_TN
    return pl.pallas_call(
        _outproj_kernel,
        grid=(D_MODEL // tn, m // tm),
        in_specs=[pl.BlockSpec((tm, D_MODEL), lambda j, i: (i, 0)), pl.BlockSpec((D_MODEL, tn), lambda j, i: (0, j)),
                  pl.BlockSpec((tm, tn), lambda j, i: (i, j))],
        out_specs=pl.BlockSpec((tm, tn), lambda j, i: (i, j)),
        out_shape=jax.ShapeDtypeStruct((m, D_MODEL), F32),
        compiler_params=_cparams(("parallel", "parallel")),
        name="outproj",
    )(mg, w, x)


MOE_TM = 256


def _router_kernel(x_ref, g_ref, wr_ref, br_ref, xn_ref, r_ref):
    x = x_ref[...]
    y = (x * lax.rsqrt(jnp.mean(x * x, axis=-1, keepdims=True) + RMS_EPS)) * g_ref[...]
    xn_ref[...] = y
    lg = jnp.dot(y.astype(BF16), wr_ref[...], preferred_element_type=F32) + br_ref[...]
    lane = lax.broadcasted_iota(I32, lg.shape, 1).astype(F32)
    far = float(LANES)
    gmask = lane < N_GROUPS
    gl = jnp.where(gmask, lg, NEG)
    mg = jnp.max(gl, axis=1, keepdims=True)
    p_g = 1.0 / jnp.sum(jnp.where(gmask, jnp.exp(gl - mg), 0.0), axis=1, keepdims=True)
    g_sel = jnp.min(jnp.where(gmask & (gl == mg), lane, far), axis=1, keepdims=True)
    e_lo = N_GROUPS + g_sel * EXP_PER_GROUP
    emask = (lane >= e_lo) & (lane < e_lo + EXP_PER_GROUP)
    el = jnp.where(emask, lg, NEG)
    me = jnp.max(el, axis=1, keepdims=True)
    pe = jnp.where(emask, jnp.exp(el - me), 0.0)
    probs = pe / jnp.sum(pe, axis=1, keepdims=True)
    p1 = jnp.max(probs, axis=1, keepdims=True)
    i1 = jnp.min(jnp.where(emask & (probs == p1), lane, far), axis=1, keepdims=True)
    probs2 = jnp.where(lane == i1, -1.0, probs)
    p2 = jnp.max(probs2, axis=1, keepdims=True)
    i2 = jnp.min(jnp.where(emask & (probs2 == p2), lane, far), axis=1, keepdims=True)
    tot = p1 + p2
    vals = [i1 - N_GROUPS, i2 - N_GROUPS, p_g * (p1 / tot), p_g * (p2 / tot)]
    out = jnp.zeros(lg.shape, F32)
    for c, v in enumerate(vals):
        out = jnp.where(lane == c, v, out)
    r_ref[...] = out


def _router(x1, g, wr, br, tm):
    m, d = x1.shape
    return pl.pallas_call(
        _router_kernel,
        grid=(m // tm,),
        in_specs=[pl.BlockSpec((tm, d), lambda i: (i, 0)), pl.BlockSpec((1, d), lambda i: (0, 0)),
                  pl.BlockSpec((d, LANES), lambda i: (0, 0)), pl.BlockSpec((1, LANES), lambda i: (0, 0))],
        out_specs=[pl.BlockSpec((tm, d), lambda i: (i, 0)), pl.BlockSpec((tm, LANES), lambda i: (i, 0))],
        out_shape=[jax.ShapeDtypeStruct((m, d), F32), jax.ShapeDtypeStruct((m, LANES), F32)],
        compiler_params=_cparams(("parallel",)),
        name="router",
    )(x1, g.reshape(1, d), wr, br)


def _row_copy(src_hbm, row, dst, r, sem):
    return pltpu.make_async_copy(src_hbm.at[pl.ds(row, 1), :], dst.at[pl.ds(r, 1), :], sem)


def _expert_kernel(be_ref, na_ref, src_ref, x_hbm, wg_ref, wu_ref, wd_ref, o_ref, xbuf, sem):
    blk = pl.program_id(0)
    slot = blk % 2

    def gather(b, s):
        def start(r, _):
            _row_copy(x_hbm, src_ref[b * MOE_TM + r], xbuf.at[s], r, sem.at[s]).start()
            return 0

        lax.fori_loop(0, MOE_TM, start, 0, unroll=8)

    @pl.when(blk == 0)
    def _prime():
        gather(0, 0)

    @pl.when(blk + 1 < na_ref[0])
    def _prefetch():
        gather(blk + 1, 1 - slot)

    @pl.when(blk < na_ref[0])
    def _active():
        def wait(r, _):
            _row_copy(x_hbm, 0, xbuf.at[slot], r, sem.at[slot]).wait()
            return 0

        lax.fori_loop(0, MOE_TM, wait, 0, unroll=8)
        x = xbuf[slot].astype(BF16)
        hg = jnp.dot(x, wg_ref[0].astype(BF16), preferred_element_type=F32)
        hu = jnp.dot(x, wu_ref[0].astype(BF16), preferred_element_type=F32)
        h = (hg * _sigmoid(hg)) * hu
        o_ref[...] = jnp.dot(h.astype(BF16), wd_ref[0].astype(BF16), preferred_element_type=F32)

    @pl.when(blk >= na_ref[0])
    def _idle():
        o_ref[...] = jnp.zeros(o_ref.shape, F32)


def _experts(blk_exp, n_act, src, xn2, w_gate, w_up, w_down):
    npad = src.shape[0]
    d = xn2.shape[1]
    gs = pltpu.PrefetchScalarGridSpec(
        num_scalar_prefetch=3,
        grid=(npad // MOE_TM,),
        in_specs=[pl.BlockSpec(memory_space=pl.ANY),
                  pl.BlockSpec((1, d, D_EXPERT), lambda b, be, na, sr: (be[b], 0, 0)),
                  pl.BlockSpec((1, d, D_EXPERT), lambda b, be, na, sr: (be[b], 0, 0)),
                  pl.BlockSpec((1, D_EXPERT, d), lambda b, be, na, sr: (be[b], 0, 0))],
        out_specs=pl.BlockSpec((MOE_TM, d), lambda b, be, na, sr: (b, 0)),
        scratch_shapes=[pltpu.VMEM((2, MOE_TM, d), F32), pltpu.SemaphoreType.DMA((2,))])
    return pl.pallas_call(
        _expert_kernel, grid_spec=gs,
        out_shape=jax.ShapeDtypeStruct((npad, d), F32),
        compiler_params=_cparams(("arbitrary",)),
        name="experts",
    )(blk_exp, n_act, src, xn2, w_gate, w_up, w_down)


def _combine_kernel(pos_ref, ys_hbm, x1_ref, r_ref, g_ref, o_ref, buf, sem, *, TC, row0):
    base = (row0 + pl.program_id(0) * TC) * TOP_E

    def start(r, _):
        for s in range(TOP_E):
            _row_copy(ys_hbm, pos_ref[base + r * TOP_E + s], buf.at[s], r, sem).start()
        return 0

    def wait(r, _):
        for s in range(TOP_E):
            _row_copy(ys_hbm, 0, buf.at[s], r, sem).wait()
        return 0

    lax.fori_loop(0, TC, start, 0, unroll=8)
    lax.fori_loop(0, TC, wait, 0, unroll=8)
    gates = r_ref[:, TOP_E:2 * TOP_E]
    x = x1_ref[...]
    for s in range(TOP_E):
        x = x + gates[:, s:s + 1] * buf[s]
    y = x * lax.rsqrt(jnp.mean(x * x, axis=-1, keepdims=True) + RMS_EPS)
    o_ref[...] = y * g_ref[...]


def _combine(pos, ys, x1, r, g, row0, n, TC):
    d = x1.shape[1]
    gs = pltpu.PrefetchScalarGridSpec(
        num_scalar_prefetch=1,
        grid=(n // TC,),
        in_specs=[pl.BlockSpec(memory_space=pl.ANY),
                  pl.BlockSpec((TC, d), lambda i, ps: (row0 // TC + i, 0)),
                  pl.BlockSpec((TC, LANES), lambda i, ps: (row0 // TC + i, 0)),
                  pl.BlockSpec((1, d), lambda i, ps: (0, 0))],
        out_specs=pl.BlockSpec((TC, d), lambda i, ps: (i, 0)),
        scratch_shapes=[pltpu.VMEM((TOP_E, TC, d), F32), pltpu.SemaphoreType.DMA(())])
    return pl.pallas_call(
        functools.partial(_combine_kernel, TC=TC, row0=row0), grid_spec=gs,
        out_shape=jax.ShapeDtypeStruct((n, d), F32),
        compiler_params=_cparams(("arbitrary",)),
        name="combine",
    )(pos, ys, x1, r, g.reshape(1, d))


def _route_tables(r, npad):
    nt = r.shape[0]
    ef = r[:, 0:TOP_E].astype(I32).reshape(-1)
    onehot = (ef[:, None] == jnp.arange(N_EXPERTS, dtype=I32)[None, :]).astype(I32)
    csum = jnp.cumsum(onehot, axis=0)
    rank = jnp.sum(onehot * csum, axis=1) - 1
    nblk = (csum[-1] + MOE_TM - 1) // MOE_TM
    blk_end = jnp.cumsum(nblk)
    pos = (blk_end - nblk)[ef] * MOE_TM + rank
    n_act = blk_end[-1:]
    b = jnp.minimum(jnp.arange(npad // MOE_TM, dtype=I32), n_act[0] - 1)
    blk_exp = jnp.minimum(jnp.sum((blk_end[None, :] <= b[:, None]).astype(I32), axis=1), N_EXPERTS - 1)
    src = jnp.zeros((npad,), I32).at[pos].set(jnp.arange(nt * TOP_E, dtype=I32) // TOP_E)
    return blk_exp, n_act.astype(I32), src, pos.astype(I32)


def kernel(x_prompt, x_sample, cache_k, cache_v, cache_kidx, state_conv, state_C, state_n, state_m, page_table,
           g_attn, w_in, b_gates_m, conv_w, conv_b, m_norm_w, w_proj_m, w_proj_a, w_out, g_ffn,
           w_rg, b_rg, w_re, b_re, w_gate, w_up, w_down, g_final):
    assert x_prompt.shape[0] == 1 and g_attn.shape[0] == 1
    l = 0
    Tp = x_prompt.shape[1]
    Bs, Ts = x_sample.shape[:2]
    Ns = Bs * Ts
    NT = Tp + Ns
    TM_BIG = 1664
    assert NT % TM_BIG == 0 and Ts >= CONV_W - 1 and Ts <= SUBLANES
    P = page_table.shape[1] * PAGE_SIZE

    x_all = jnp.concatenate([x_prompt[0], x_sample.reshape(Ns, D_MODEL)], axis=0)
    xn = _rmsnorm(x_all, g_attn[l], BF16, 640)
    p32, p16 = _inproj(xn, _prep_w_in(w_in[l]), TM_BIG)

    gate_cols = slice(C_TAIL + T_IM, C_TAIL + T_IM + 2 * M_HEADS)
    ps3 = p32[Tp:].reshape(Bs, Ts, D_CAT)
    ps8 = jnp.pad(ps3, ((0, 0), (0, SUBLANES - Ts), (0, 0)))

    zero = lambda *s: jnp.zeros(s, F32)
    hm_p, C_p, n_p, m_p = _mlstm(p32[None], p32[:Tp, gate_cols].T[None], zero(1, SUBLANES, 2 * M_WIDTH),
                                 conv_w[l], conv_b[l], b_gates_m[l], m_norm_w[l],
                                 zero(1, M_HEADS, M_DH, M_DH), zero(1, M_HEADS, M_DH), zero(1, M_HEADS),
                                 T=Tp, L=256, RIN=256, valid=256)
    grow_s = jnp.pad(jnp.swapaxes(ps3[:, :, gate_cols], 1, 2), ((0, 0), (0, 0), (0, LANES - Ts)))
    cb_s = jnp.pad(state_conv[l], ((0, 0), (SUBLANES - (CONV_W - 1), 0), (0, 0)))
    hm_s, C_s, n_s, m_s = _mlstm(ps8, grow_s, cb_s, conv_w[l], conv_b[l], b_gates_m[l], m_norm_w[l],
                                 state_C[l], state_n[l], state_m[l], T=SUBLANES, L=LANES, RIN=SUBLANES, valid=Ts)

    ha_p = _dsa_prompt(p32, p16, Tp, min(TOPK_MAX, Tp // 4))
    qs = ps3[:, :, C_QI:C_QI + IDX_HEADS * LANES].reshape(Bs, Ts, IDX_HEADS, LANES)[..., :IDX_DIM].reshape(Bs, Ts * IDX_HEADS, IDX_DIM)
    ws = ps3[:, :, C_TAIL + T_WI:C_TAIL + T_WI + IDX_HEADS].reshape(Bs, Ts * IDX_HEADS, 1)
    kin_t = jnp.pad(jnp.swapaxes(ps3[:, :, C_TAIL + T_KI:C_TAIL + T_KI + IDX_DIM], 1, 2), ((0, 0), (0, 0), (0, LANES - Ts)))
    sc = _smp_scores2(page_table, qs, ws, jnp.swapaxes(cache_kidx[l], 1, 2))
    mp, mn = _smp_select2(sc.reshape(Bs * SMP_ROWS, P), qs, ws, kin_t, Ts, min(TOPK_MAX, (P + Ts) // 4))
    rows_th = lambda c0: ps3[:, :, c0:c0 + A_WIDTH].reshape(Bs, Ts * A_HEADS, A_DH)
    ha_s = _smp_attn2(page_table, rows_th(C_QA), rows_th(C_KA), rows_th(C_VA), mn.reshape(Bs, SMP_ROWS, LANES),
                      mp.reshape(Bs, SMP_ROWS, P), cache_k[l], cache_v[l], Ts)

    hm_all = jnp.concatenate([hm_p[0], hm_s[:, :Ts].reshape(Ns, M_WIDTH)], axis=0)
    ha_all = jnp.concatenate([ha_p, ha_s.reshape(Ns, A_WIDTH).astype(BF16)], axis=0)
    merged = _merge(hm_all, ha_all, w_proj_m[l].astype(BF16), w_proj_a[l].astype(BF16), p32, TM_BIG)
    x1 = _outproj(merged, w_out[l].astype(BF16), x_all, TM_BIG)

    wr = jnp.pad(jnp.concatenate([w_rg[l], w_re[l]], axis=1), ((0, 0), (0, LANES - N_GROUPS - N_EXPERTS))).astype(BF16)
    br = jnp.pad(jnp.concatenate([b_rg[l], b_re[l]]), (0, LANES - N_GROUPS - N_EXPERTS)).reshape(1, LANES)
    xn2, r = _router(x1, g_ffn[l], wr, br, 640)
    npad = NT * TOP_E + N_EXPERTS * MOE_TM
    blk_exp, n_act, src, pos = _route_tables(r, npad)
    ys = _experts(blk_exp, n_act, src, xn2, w_gate[l], w_up[l], w_down[l])
    y_p = _combine(pos, ys, x1, r, g_final, 0, Tp, 256)
    y_s = _combine(pos, ys, x1, r, g_final, Tp, Ns, LANES)

    st = lambda a, shape: a.reshape((1,) + shape)
    pp = p32[:Tp]
    return (y_p[None], y_s.reshape(Bs, Ts, D_MODEL),
            st(pp[:, C_KA:C_KA + A_WIDTH], (1, Tp, A_HEADS, A_DH)), st(pp[:, C_VA:C_VA + A_WIDTH], (1, Tp, A_HEADS, A_DH)),
            st(pp[:, C_TAIL + T_KI:C_TAIL + T_KI + IDX_DIM], (1, Tp, IDX_DIM)),
            st(pp[Tp - (CONV_W - 1):, 0:2 * M_WIDTH], (1, CONV_W - 1, 2 * M_WIDTH)),
            st(C_p, (1, M_HEADS, M_DH, M_DH)), st(n_p, (1, M_HEADS, M_DH)), st(m_p[:, :, 0, 0], (1, M_HEADS)),
            st(ps3[:, :, C_KA:C_KA + A_WIDTH], (Bs, Ts, A_HEADS, A_DH)), st(ps3[:, :, C_VA:C_VA + A_WIDTH], (Bs, Ts, A_HEADS, A_DH)),
            st(ps3[:, :, C_TAIL + T_KI:C_TAIL + T_KI + IDX_DIM], (Bs, Ts, IDX_DIM)),
            st(ps3[:, Ts - (CONV_W - 1):, 0:2 * M_WIDTH], (Bs, CONV_W - 1, 2 * M_WIDTH)),
            st(C_s, (Bs, M_HEADS, M_DH, M_DH)), st(n_s, (Bs, M_HEADS, M_DH)), st(m_s[:, :, 0, 0], (Bs, M_HEADS)))
```

```python
import functools

import jax
import jax.numpy as jnp
import numpy as np
from jax import lax
from jax.experimental import pallas as pl
from jax.experimental.pallas import tpu as pltpu

F32 = jnp.float32
BF16 = jnp.bfloat16
I32 = jnp.int32

D_MODEL = 2048
M_WIDTH = D_MODEL // 2
M_HEADS = 4
M_DH = M_WIDTH // M_HEADS
CONV_W = 4
A_WIDTH = D_MODEL // 2
A_DH = 128
A_HEADS = A_WIDTH // A_DH
IDX_HEADS = 8
IDX_DIM = 64
TOPK_MAX = 256
PAGE_SIZE = 128
N_GROUPS = 4
EXP_PER_GROUP = 8
N_EXPERTS = N_GROUPS * EXP_PER_GROUP
TOP_E = 2
D_EXPERT = D_MODEL // 4
RMS_EPS = 1e-6
IN_SIZES = (M_WIDTH, M_WIDTH, M_WIDTH, M_WIDTH, M_HEADS, M_HEADS, A_WIDTH, A_WIDTH, A_WIDTH,
            IDX_HEADS * IDX_DIM, IDX_DIM, IDX_HEADS, D_MODEL, D_MODEL)
IN_SPLITS = tuple(int(s) for s in np.cumsum(IN_SIZES)[:-1])

LANES = 128
SUBLANES = 8
VMEM_LIMIT = 56 * 1024 * 1024

C_QM, C_KM, C_VM, C_OM = 0, 1024, 2048, 3072
C_QA, C_KA, C_VA = 4096, 5120, 6144
C_GM, C_GA = 7168, 9216
C_QI = 11264
C_TAIL = 12288
T_KI, T_WI, T_IM, T_FM = 0, 64, 72, 76
D_CAT = 12800
PROJ_TN = 512

NEG = -1e30
INT_MIN = -2 ** 31


def _cparams(sem):
    return pltpu.CompilerParams(dimension_semantics=sem, vmem_limit_bytes=VMEM_LIMIT)


def _rms_kernel(x_ref, g_ref, o_ref):
    x = x_ref[...]
    y = x * lax.rsqrt(jnp.mean(x * x, axis=-1, keepdims=True) + RMS_EPS)
    o_ref[...] = (y * g_ref[...]).astype(o_ref.dtype)


def _rmsnorm(x, g, out_dtype, tm):
    m, d = x.shape
    return pl.pallas_call(
        _rms_kernel,
        grid=(m // tm,),
        in_specs=[pl.BlockSpec((tm, d), lambda i: (i, 0)), pl.BlockSpec((1, d), lambda i: (0, 0))],
        out_specs=pl.BlockSpec((tm, d), lambda i: (i, 0)),
        out_shape=jax.ShapeDtypeStruct((m, d), out_dtype),
        compiler_params=_cparams(("parallel",)),
        name="rmsnorm",
    )(x, g.reshape(1, d))


def _inproj_kernel(x_ref, wt_ref, o32_ref, o16_ref):
    acc = lax.dot_general(x_ref[...], wt_ref[...], (((1,), (1,)), ((), ())), preferred_element_type=F32)
    o32_ref[...] = acc
    o16_ref[...] = acc.astype(BF16)


def _inproj(xn, w_cat_t, tm):
    m, d = xn.shape
    n = w_cat_t.shape[0]
    tn = PROJ_TN
    return pl.pallas_call(
        _inproj_kernel,
        grid=(m // tm, n // tn),
        in_specs=[pl.BlockSpec((tm, d), lambda i, j: (i, 0)), pl.BlockSpec((tn, d), lambda i, j: (j, 0))],
        out_specs=[pl.BlockSpec((tm, tn), lambda i, j: (i, j)), pl.BlockSpec((tm, tn), lambda i, j: (i, j))],
        out_shape=[jax.ShapeDtypeStruct((m, n), F32), jax.ShapeDtypeStruct((m, n), BF16)],
        compiler_params=_cparams(("parallel", "parallel")),
        name="inproj",
    )(xn, w_cat_t)


def _prep_w_in(w_in):
    (q_m, k_m, v_m, o_m, i_m, f_m, q_a, k_a, v_a, q_i, k_i, w_i, g_m, g_a) = jnp.split(w_in.T, IN_SPLITS, axis=0)
    d = w_in.shape[0]
    q_i = jnp.pad(q_i.reshape(IDX_HEADS, IDX_DIM, d), ((0, 0), (0, LANES - IDX_DIM), (0, 0))).reshape(IDX_HEADS * LANES, d)
    rows = [q_m, k_m, v_m, o_m, q_a, k_a, v_a, g_m, g_a, q_i, k_i, w_i, i_m, f_m]
    w = jnp.concatenate(rows, axis=0)
    return jnp.pad(w, ((0, D_CAT - w.shape[0]), (0, 0))).astype(BF16)


def _sigmoid(x):
    return 1.0 / (1.0 + jnp.exp(-x))


def _log_sigmoid(x):
    return jnp.minimum(x, 0.0) - jnp.log1p(jnp.exp(-jnp.abs(x)))


def _mlstm_kernel(q_ref, k_ref, v_ref, o_ref, tail_ref, grow_ref, cb_ref, convw_ref, convb_ref, bl_ref, bs_ref,
                  nw_ref, c0_ref, n0_ref, m0_ref,
                  h_ref, cout_ref, nout_ref, mout_ref,
                  xq_s, xk_s, c_s, n_s, m_s, vp_s, op_s, tp_s, *, L, RIN, valid):
    c = pl.program_id(1)
    nc = pl.num_programs(1)

    @pl.when(c == 0)
    def _init():
        xq_s[0:SUBLANES, :] = cb_ref[0, :, 0:M_WIDTH]
        xk_s[0:SUBLANES, :] = cb_ref[0, :, M_WIDTH:2 * M_WIDTH]
        c_s[...] = c0_ref[0]
        n_s[...] = n0_ref[0]
        m_s[...] = m0_ref[0]

    if RIN < L:
        zpad = jnp.zeros((L - RIN, M_WIDTH), F32)
        xq_s[SUBLANES + RIN:SUBLANES + L, :] = zpad
        xk_s[SUBLANES + RIN:SUBLANES + L, :] = zpad
        vp_s[RIN:L, :] = zpad
        op_s[RIN:L, :] = zpad
        tp_s[RIN:L, :] = jnp.zeros((L - RIN, LANES), F32)
    xq_s[SUBLANES:SUBLANES + RIN, :] = q_ref[0]
    xk_s[SUBLANES:SUBLANES + RIN, :] = k_ref[0]
    vp_s[0:RIN, :] = v_ref[0]
    op_s[0:RIN, :] = o_ref[0]
    tp_s[0:RIN, :] = tail_ref[0]

    def conv(xs, col0):
        w = convw_ref[:, col0:col0 + M_WIDTH]
        y = convb_ref[:, col0:col0 + M_WIDTH]
        for j in range(CONV_W):
            r0 = SUBLANES - (CONV_W - 1) + j
            y = y + xs[r0:r0 + L, :] * w[j:j + 1, :]
        return y * _sigmoid(y)

    q_all = conv(xq_s, 0) * (M_DH ** -0.5)
    k_all = conv(xk_s, M_WIDTH)
    v_all = vp_s[...]
    tail = tp_s[...] + bl_ref[...]
    grow = grow_ref[0] + bs_ref[...]

    xq_s[0:SUBLANES, :] = xq_s[L:L + SUBLANES, :]
    xk_s[0:SUBLANES, :] = xk_s[L:L + SUBLANES, :]

    tt = lax.broadcasted_iota(I32, (L, L), 0)
    ss = lax.broadcasted_iota(I32, (L, L), 1)
    causal = ss <= tt
    row_ok = lax.broadcasted_iota(I32, (L, 1), 0) < valid
    col_ok = lax.broadcasted_iota(I32, (1, L), 1) < valid

    for h in range(M_HEADS):
        hs = slice(h * M_DH, (h + 1) * M_DH)
        qh = q_all[:, hs]
        kh = k_all[:, hs]
        vh = v_all[:, hs]
        ig_c = jnp.where(row_ok, tail[:, T_IM + h:T_IM + h + 1], NEG)
        lf_c = jnp.where(row_ok, _log_sigmoid(tail[:, T_FM + h:T_FM + h + 1]), 0.0)
        ig_r = jnp.where(col_ok, grow[h:h + 1, :], NEG)
        lf_r = jnp.where(col_ok, _log_sigmoid(grow[M_HEADS + h:M_HEADS + h + 1, :]), 0.0)
        b_c = jnp.sum(jnp.where(causal, lf_r, 0.0), axis=1, keepdims=True)
        b_r = jnp.sum(jnp.where(tt <= ss, lf_c, 0.0), axis=0, keepdims=True)
        dmat = jnp.where(causal, b_c - b_r + ig_r, NEG)
        m_prev = m_s[h][:, 0:1]
        m_t = jnp.maximum(b_c + m_prev, jnp.max(dmat, axis=1, keepdims=True))
        e = jnp.exp(dmat - m_t)
        qb = qh.astype(BF16)
        kb = kh.astype(BF16)
        s = lax.dot_general(qb, kb, (((1,), (1,)), ((), ())), preferred_element_type=F32) * e
        inter = jnp.exp(b_c + m_prev - m_t)
        ch = c_s[h]
        num = jnp.dot(s.astype(BF16), vh.astype(BF16), preferred_element_type=F32) + inter * lax.dot_general(
            qb, ch.astype(BF16), (((1,), (1,)), ((), ())), preferred_element_type=F32)
        nh = n_s[h]
        den = jnp.sum(s, axis=1, keepdims=True) + inter * jnp.sum(qh * nh, axis=1, keepdims=True)
        hh = num / jnp.maximum(jnp.abs(den), jnp.exp(-m_t))
        hh = hh * lax.rsqrt(jnp.mean(hh * hh, axis=1, keepdims=True) + RMS_EPS)
        out = _sigmoid(op_s[:, hs]) * (hh * nw_ref[:, hs])
        h_ref[0, :, hs] = out[0:RIN, :].astype(h_ref.dtype)
        m_new = m_t[L - 1:L, :]
        b_last = b_c[L - 1:L, :]
        w_c = jnp.exp(b_last - b_c + ig_c - m_new)
        decay = jnp.exp(b_last + m_prev - m_new)
        upd = lax.dot_general((w_c * vh).astype(BF16), kb, (((0,), (0,)), ((), ())), preferred_element_type=F32)
        c_s[h] = decay * ch + upd
        n_s[h] = decay * nh + jnp.sum(w_c * kh, axis=0, keepdims=True)
        m_s[h] = jnp.broadcast_to(m_new, (1, LANES))

    @pl.when(c == nc - 1)
    def _fin():
        cout_ref[0] = c_s[...]
        nout_ref[0] = n_s[...]
        mout_ref[0] = m_s[...]


def _mlstm(p32, grow, convbuf, conv_w, conv_b, b_gates, m_norm_w, c0, n0, m0, *, T, L, RIN, valid):
    B = p32.shape[0]
    nc = T // RIN
    bl = jnp.zeros((1, LANES), F32).at[0, T_IM:T_IM + 2 * M_HEADS].set(b_gates)
    bs = jnp.broadcast_to(b_gates[:, None], (2 * M_HEADS, L))
    kern = functools.partial(_mlstm_kernel, L=L, RIN=RIN, valid=valid)
    cblk = lambda col: pl.BlockSpec((1, RIN, M_WIDTH), lambda b, c, col=col: (b, c, col // M_WIDTH))
    const2 = lambda shape: pl.BlockSpec(shape, lambda b, c: (0, 0))
    per_b = lambda shape: pl.BlockSpec(shape, lambda b, c: (b,) + (0,) * (len(shape) - 1))
    return pl.pallas_call(
        kern,
        grid=(B, nc),
        in_specs=[cblk(C_QM), cblk(C_KM), cblk(C_VM), cblk(C_OM),
                  pl.BlockSpec((1, RIN, LANES), lambda b, c: (b, c, C_TAIL // LANES)),
                  pl.BlockSpec((1, 2 * M_HEADS, L), lambda b, c: (b, 0, c)),
                  per_b((1, SUBLANES, 2 * M_WIDTH)),
                  const2((CONV_W, 2 * M_WIDTH)), const2((1, 2 * M_WIDTH)), const2((1, LANES)),
                  const2((2 * M_HEADS, L)), const2((1, M_WIDTH)),
                  per_b((1, M_HEADS, M_DH, M_DH)), per_b((1, M_HEADS, 1, M_DH)), per_b((1, M_HEADS, 1, LANES))],
        out_specs=[pl.BlockSpec((1, RIN, M_WIDTH), lambda b, c: (b, c, 0)),
                   per_b((1, M_HEADS, M_DH, M_DH)), per_b((1, M_HEADS, 1, M_DH)), per_b((1, M_HEADS, 1, LANES))],
        out_shape=[jax.ShapeDtypeStruct((B, T, M_WIDTH), BF16),
                   jax.ShapeDtypeStruct((B, M_HEADS, M_DH, M_DH), F32),
                   jax.ShapeDtypeStruct((B, M_HEADS, 1, M_DH), F32),
                   jax.ShapeDtypeStruct((B, M_HEADS, 1, LANES), F32)],
        scratch_shapes=[pltpu.VMEM((SUBLANES + L, M_WIDTH), F32), pltpu.VMEM((SUBLANES + L, M_WIDTH), F32),
                        pltpu.VMEM((M_HEADS, M_DH, M_DH), F32), pltpu.VMEM((M_HEADS, 1, M_DH), F32),
                        pltpu.VMEM((M_HEADS, 1, LANES), F32),
                        pltpu.VMEM((L, M_WIDTH), F32), pltpu.VMEM((L, M_WIDTH), F32), pltpu.VMEM((L, LANES), F32)],
        compiler_params=_cparams(("parallel", "arbitrary")),
        name="mlstm",
    )(p32, p32, p32, p32, p32, grow, convbuf, conv_w, conv_b.reshape(1, -1), bl, bs, m_norm_w.reshape(1, -1),
      c0, n0.reshape(B, M_HEADS, 1, M_DH), jnp.broadcast_to(m0[:, :, None, None], (B, M_HEADS, 1, LANES)))


def _score_key(sc):
    bits = lax.bitcast_convert_type(sc, I32)
    return jnp.where(bits < 0, INT_MIN - bits, bits)


DSA_TQ = 512
DSA_TK = 512
DSA_RG = 64
DSA_RB = 256
LOG2E = 1.4426950408889634


def _dsa_prompt_kernel(qi_tab, kj_tab, qidx_ref, tail_ref, kit_ref, qa_ref, ka_ref, va_ref, o_ref,
                       keys_s, cand_s, cnt_s, thr_s, thrm_s, cut_s, bias_s, tie_s, wrep_s, m_s, l_s, acc_s,
                       *, TQ, TK, topk, pos_bits):
    step = pl.program_id(0)
    qi = qi_tab[step]
    kj = kj_tab[step]
    RG = DSA_RG
    nlc = TK // LANES

    def count_pass(pred):
        def rbody(r, _):
            r0 = pl.multiple_of(r * RG, RG)
            cand = cand_s[pl.ds(r0, RG), :]
            aux = thr_s[pl.ds(r0, RG), :]

            def kb(j, cnt):
                for c in range(nlc):
                    blk = keys_s[j, pl.ds(r0, RG), c * LANES:(c + 1) * LANES]
                    pos = j * TK + c * LANES + lax.broadcasted_iota(I32, (RG, LANES), 1)
                    cnt = cnt + jnp.where(pred(blk, cand, aux, pos), 1.0, 0.0)
                return cnt

            cnt_s[pl.ds(r0, RG), :] = lax.fori_loop(0, qi + 1, kb, jnp.zeros((RG, LANES), F32))
            return 0

        lax.fori_loop(0, TQ // RG, rbody, 0)
        return jnp.sum(cnt_s[...], axis=1, keepdims=True)

    @pl.when(kj == 0)
    def _phase1():
        w = tail_ref[:, T_WI:T_WI + IDX_HEADS] * (IDX_HEADS ** -0.5) * (IDX_DIM ** -0.5)
        for h in range(IDX_HEADS):
            wrep_s[h] = jnp.broadcast_to(w[:, h:h + 1], (TQ, LANES))
        rowpos = qi * TQ + lax.broadcasted_iota(I32, (TQ, TK), 0)

        def kbody(j, _):
            kt = kit_ref[j].astype(BF16)
            sc = jnp.zeros((TQ, TK), F32)
            for h in range(IDX_HEADS):
                qh = qidx_ref[:, h * LANES:(h + 1) * LANES].astype(BF16)
                s = jnp.dot(qh, kt, preferred_element_type=F32)
                sc = sc + jnp.maximum(s, 0.0) * jnp.concatenate([wrep_s[h]] * nlc, axis=1)
            colpos = j * TK + lax.broadcasted_iota(I32, (TQ, TK), 1)
            keys_s[j] = jnp.where(colpos <= rowpos, _score_key(sc), INT_MIN)
            return 0

        lax.fori_loop(0, qi + 1, kbody, 0)

        ge = lambda blk, cand, aux, pos: blk >= cand
        cand_s[...] = jnp.zeros((TQ, LANES), I32)
        cnt = count_pass(ge)
        lo = jnp.where(cnt >= topk, 0, INT_MIN).astype(I32)

        def bit_body(b, carry):
            lo, n_lo = carry
            cand = lo | lax.shift_left(jnp.int32(1), 30 - b)
            cand_s[...] = jnp.broadcast_to(cand, (TQ, LANES))
            cnt = count_pass(ge)
            take = cnt >= topk
            return jnp.where(take, cand, lo), jnp.where(take, cnt, n_lo)

        thr, n_ge = lax.fori_loop(0, 31, bit_body, (lo, cnt))
        thr_s[...] = jnp.broadcast_to(thr, (TQ, LANES))
        short = thr == INT_MIN
        thrm_s[...] = jnp.broadcast_to(jnp.where(short, INT_MIN, thr - 1), (TQ, LANES))
        cut_s[...] = jnp.broadcast_to(jnp.where(short, -1, 2 ** 30).astype(I32), (TQ, LANES))
        surplus = jnp.max(jnp.where((n_ge > topk) & jnp.logical_not(short), 1.0, 0.0))
        tie_s[0] = (surplus > 0.0).astype(I32)

        @pl.when(surplus > 0.0)
        def _ties():
            cand_s[...] = thr_s[...]
            need = topk - count_pass(lambda blk, cand, aux, pos: blk > cand)
            eq_below = lambda blk, cand, aux, pos: (blk == aux) & (pos < cand)

            def tie_body(b, x):
                cand = x + lax.shift_left(jnp.int32(1), pos_bits - 1 - b)
                cand_s[...] = jnp.broadcast_to(cand, (TQ, LANES))
                cnt = count_pass(eq_below)
                return jnp.where(cnt < need, cand, x)

            x = lax.fori_loop(0, pos_bits, tie_body, jnp.zeros((TQ, 1), I32))
            cut_s[...] = jnp.broadcast_to(jnp.where(short, -1, x), (TQ, LANES))

        m_s[...] = jnp.full(m_s.shape, NEG, F32)
        l_s[...] = jnp.zeros(l_s.shape, F32)
        acc_s[...] = jnp.zeros(acc_s.shape, F32)

    tile_l = lambda a: jnp.concatenate([a] * nlc, axis=1)

    @pl.when(tie_s[0] == 0)
    def _bias_plain():
        bias_s[...] = jnp.where(keys_s[kj] > tile_l(thrm_s[...]), 0.0, NEG)

    @pl.when(tie_s[0] != 0)
    def _bias_tied():
        key = keys_s[kj]
        thr = tile_l(thr_s[...])
        colpos = kj * TK + lax.broadcasted_iota(I32, (TQ, TK), 1)
        sel = (key > thr) | ((key == thr) & (colpos <= tile_l(cut_s[...])))
        bias_s[...] = jnp.where(sel, 0.0, NEG)

    RB = DSA_RB
    c1 = (A_DH ** -0.5) * LOG2E

    def rb_body(rb, _):
        rows = pl.ds(pl.multiple_of(rb * RB, RB), RB)
        bias = bias_s[rows, :]
        hsl = [slice(h * A_DH, (h + 1) * A_DH) for h in range(A_HEADS)]
        m_old = [m_s[h, rows, :] for h in range(A_HEADS)]
        l_old = [l_s[h, rows, :] for h in range(A_HEADS)]
        a_old = [acc_s[rows, hs] for hs in hsl]
        ts = [lax.dot_general(qa_ref[rows, hs], ka_ref[:, hs], (((1,), (1,)), ((), ())), preferred_element_type=F32) * c1 + bias
              for hs in hsl]
        m_new = [jnp.maximum(m_old[h], jnp.max(ts[h], axis=1, keepdims=True)) for h in range(A_HEADS)]
        l_new, a_new = [], []
        for h in range(A_HEADS):
            alpha = jnp.exp2(m_old[h] - m_new[h])
            p = jnp.exp2(ts[h] - tile_l(m_new[h]))
            psum = p[:, 0:LANES]
            for c in range(1, nlc):
                psum = psum + p[:, c * LANES:(c + 1) * LANES]
            l_new.append(alpha * l_old[h] + psum)
            a_new.append(alpha * a_old[h] + jnp.dot(p.astype(BF16), va_ref[:, hsl[h]], preferred_element_type=F32))
        for h in range(A_HEADS):
            m_s[h, rows, :] = m_new[h]
            l_s[h, rows, :] = l_new[h]
            acc_s[rows, hsl[h]] = a_new[h]
        return 0

    lax.fori_loop(0, TQ // RB, rb_body, 0)

    @pl.when(kj == qi)
    def _fin():
        for h in range(A_HEADS):
            hs = slice(h * A_DH, (h + 1) * A_DH)
            o_ref[:, hs] = (acc_s[:, hs] / jnp.sum(l_s[h], axis=1, keepdims=True)).astype(o_ref.dtype)


def _dsa_prompt(p32, p16, T, topk):
    TQ, TK = DSA_TQ, DSA_TK
    nq = T // TQ
    assert TQ == TK
    qi_tab = np.concatenate([np.full(i + 1, i) for i in range(nq)]).astype(np.int32)
    kj_tab = np.concatenate([np.arange(i + 1) for i in range(nq)]).astype(np.int32)
    ki = p32[:T, C_TAIL + T_KI:C_TAIL + T_KI + IDX_DIM]
    kit = jnp.pad(ki.T, ((0, LANES - IDX_DIM), (0, 0))).reshape(LANES, T // TK, TK).transpose(1, 0, 2)
    kern = functools.partial(_dsa_prompt_kernel, TQ=TQ, TK=TK, topk=topk, pos_bits=int(T - 1).bit_length())
    gs = pltpu.PrefetchScalarGridSpec(
        num_scalar_prefetch=2,
        grid=(len(qi_tab),),
        in_specs=[pl.BlockSpec((TQ, IDX_HEADS * LANES), lambda s, qt, kt: (qt[s], C_QI // (IDX_HEADS * LANES))),
                  pl.BlockSpec((TQ, LANES), lambda s, qt, kt: (qt[s], C_TAIL // LANES)),
                  pl.BlockSpec((T // TK, LANES, TK), lambda s, qt, kt: (0, 0, 0)),
                  pl.BlockSpec((TQ, A_WIDTH), lambda s, qt, kt: (qt[s], C_QA // A_WIDTH)),
                  pl.BlockSpec((TK, A_WIDTH), lambda s, qt, kt: (kt[s], C_KA // A_WIDTH)),
                  pl.BlockSpec((TK, A_WIDTH), lambda s, qt, kt: (kt[s], C_VA // A_WIDTH))],
        out_specs=pl.BlockSpec((TQ, A_WIDTH), lambda s, qt, kt: (qt[s], 0)),
        scratch_shapes=[pltpu.VMEM((T // TK, TQ, TK), I32),
                        pltpu.VMEM((TQ, LANES), I32), pltpu.VMEM((TQ, LANES), F32), pltpu.VMEM((TQ, LANES), I32),
                        pltpu.VMEM((TQ, LANES), I32), pltpu.VMEM((TQ, LANES), I32),
                        pltpu.VMEM((TQ, TK), F32), pltpu.SMEM((1,), I32), pltpu.VMEM((IDX_HEADS, TQ, LANES), F32),
                        pltpu.VMEM((A_HEADS, TQ, LANES), F32), pltpu.VMEM((A_HEADS, TQ, LANES), F32),
                        pltpu.VMEM((TQ, A_WIDTH), F32)])
    return pl.pallas_call(
        kern, grid_spec=gs,
        out_shape=jax.ShapeDtypeStruct((T, A_WIDTH), BF16),
        compiler_params=_cparams(("arbitrary",)),
        name="dsa_prompt",
    )(jnp.asarray(qi_tab), jnp.asarray(kj_tab), p32, p32, kit, p16, p16, p16)


SMP_PGS = 32
SMP_PG = 8
SMP_ROWS = SUBLANES
SMP_SEL_B = 16


def _idx_scores_t(q, w, kt):
    s = jnp.dot(q.astype(BF16), kt.astype(BF16), preferred_element_type=F32)
    s = jnp.maximum(s * (IDX_DIM ** -0.5), 0.0) * (w * (IDX_HEADS ** -0.5))
    n_tok = q.shape[0] // IDX_HEADS
    rows = [jnp.sum(s[t * IDX_HEADS:(t + 1) * IDX_HEADS], axis=0, keepdims=True) for t in range(n_tok)]
    rows.append(jnp.zeros((SMP_ROWS - n_tok, s.shape[1]), F32))
    return jnp.concatenate(rows, axis=0)


def _smp_scores2_kernel(pt_ref, q_ref, w_ref, *refs):
    o_ref = refs[-1]
    kt = jnp.concatenate([r[0] for r in refs[:-1]], axis=1)
    o_ref[0] = _idx_scores_t(q_ref[0], w_ref[0], kt)


def _smp_scores2(page_table, qs, ws, kidx_t):
    B, n_pages = page_table.shape
    R = qs.shape[1]
    page = lambda i: pl.BlockSpec((1, IDX_DIM, PAGE_SIZE), lambda b, p, pt: (pt[b, p * SMP_PGS + i], 0, 0))
    gs = pltpu.PrefetchScalarGridSpec(
        num_scalar_prefetch=1,
        grid=(B, n_pages // SMP_PGS),
        in_specs=[pl.BlockSpec((1, R, IDX_DIM), lambda b, p, pt: (b, 0, 0)),
                  pl.BlockSpec((1, R, 1), lambda b, p, pt: (b, 0, 0))] + [page(i) for i in range(SMP_PGS)],
        out_specs=pl.BlockSpec((1, SMP_ROWS, SMP_PGS * PAGE_SIZE), lambda b, p, pt: (b, 0, p)))
    return pl.pallas_call(
        _smp_scores2_kernel, grid_spec=gs,
        out_shape=jax.ShapeDtypeStruct((B, SMP_ROWS, n_pages * PAGE_SIZE), F32),
        compiler_params=_cparams(("parallel", "arbitrary")),
        name="smp_scores",
    )(page_table, qs, ws, *([kidx_t] * SMP_PGS))


def _smp_select2_kernel(sc_ref, q_ref, w_ref, kint_ref, mp_ref, mn_ref, keys_s, cand_s, thr_s, cut_s, cnt_s,
                        *, n_tok, topk):
    NB = q_ref.shape[0]
    R = NB * SMP_ROWS
    P = sc_ref.shape[1]
    NCH = P // LANES
    RG = DSA_RG
    lane = lax.broadcasted_iota(I32, (R, LANES), 1)
    trow = lax.broadcasted_iota(I32, (R, LANES), 0) % SMP_ROWS
    for c in range(NCH):
        keys_s[c] = _score_key(sc_ref[:, c * LANES:(c + 1) * LANES])
    s_new = jnp.concatenate([_idx_scores_t(q_ref[b], w_ref[b], kint_ref[b]) for b in range(NB)], axis=0)
    keys_s[NCH] = jnp.where(lane <= trow, _score_key(s_new), INT_MIN)

    def count_pass(pred):
        def rbody(r, _):
            rows = pl.ds(pl.multiple_of(r * RG, RG), RG)
            cand = cand_s[rows, :]
            aux = thr_s[rows, :]

            def cb(c, cnt):
                pos = c * LANES + lax.broadcasted_iota(I32, (RG, LANES), 1)
                return cnt + jnp.where(pred(keys_s[c, rows, :], cand, aux, pos), 1.0, 0.0)

            cnt_s[rows, :] = lax.fori_loop(0, NCH + 1, cb, jnp.zeros((RG, LANES), F32))
            return 0

        lax.fori_loop(0, R // RG, rbody, 0)
        return jnp.sum(cnt_s[...], axis=1, keepdims=True)

    ge = lambda blk, cand, aux, pos: blk >= cand
    cand_s[...] = jnp.zeros((R, LANES), I32)
    cnt0 = count_pass(ge)
    lo = jnp.where(cnt0 >= topk, 0, INT_MIN).astype(I32)

    def bit_body(b, carry):
        lo, n_lo = carry
        cand = lo | lax.shift_left(jnp.int32(1), 30 - b)
        cand_s[...] = jnp.broadcast_to(cand, (R, LANES))
        cnt = count_pass(ge)
        take = cnt >= topk
        return jnp.where(take, cand, lo), jnp.where(take, cnt, n_lo)

    thr, n_ge = lax.fori_loop(0, 31, bit_body, (lo, cnt0))
    thr_s[...] = jnp.broadcast_to(thr, (R, LANES))
    cut_s[...] = jnp.full((R, LANES), 2 ** 30, I32)
    real_row = lax.broadcasted_iota(I32, (R, 1), 0) % SMP_ROWS < n_tok
    surplus = jnp.max(jnp.where((n_ge > topk) & (thr != INT_MIN) & real_row, 1.0, 0.0))

    @pl.when(surplus > 0.0)
    def _ties():
        cand_s[...] = thr_s[...]
        need = topk - count_pass(lambda blk, cand, aux, pos: blk > cand)
        pos_bits = int(P + LANES - 1).bit_length()
        eq_below = lambda blk, cand, aux, pos: (blk == aux) & (pos < cand)

        def tie_body(b, x):
            cand = x + lax.shift_left(jnp.int32(1), pos_bits - 1 - b)
            cand_s[...] = jnp.broadcast_to(cand, (R, LANES))
            return jnp.where(count_pass(eq_below) < need, cand, x)

        cut_s[...] = jnp.broadcast_to(lax.fori_loop(0, pos_bits, tie_body, jnp.zeros((R, 1), I32)), (R, LANES))

    cut = cut_s[...]
    thr_b = thr_s[...]
    row_ok = trow < n_tok
    for c in range(NCH + 1):
        key = keys_s[c]
        sel = ((key > thr_b) | ((key == thr_b) & (c * LANES + lane <= cut))) & (key != INT_MIN) & row_ok
        if c < NCH:
            mp_ref[:, c * LANES:(c + 1) * LANES] = jnp.where(sel, 1.0, 0.0)
        else:
            mn_ref[...] = jnp.where(sel, 1.0, 0.0)


def _smp_select2(sc, qs, ws, kin_t, n_tok, topk):
    R, P = sc.shape
    B = qs.shape[0]
    NB = SMP_SEL_B
    RS = NB * SMP_ROWS
    Rq = qs.shape[1]
    return pl.pallas_call(
        functools.partial(_smp_select2_kernel, n_tok=n_tok, topk=topk),
        grid=(B // NB,),
        in_specs=[pl.BlockSpec((RS, P), lambda i: (i, 0)),
                  pl.BlockSpec((NB, Rq, IDX_DIM), lambda i: (i, 0, 0)), pl.BlockSpec((NB, Rq, 1), lambda i: (i, 0, 0)),
                  pl.BlockSpec((NB, IDX_DIM, LANES), lambda i: (i, 0, 0))],
        out_specs=[pl.BlockSpec((RS, P), lambda i: (i, 0)), pl.BlockSpec((RS, LANES), lambda i: (i, 0))],
        out_shape=[jax.ShapeDtypeStruct((R, P), F32), jax.ShapeDtypeStruct((R, LANES), F32)],
        scratch_shapes=[pltpu.VMEM((P // LANES + 1, RS, LANES), I32), pltpu.VMEM((RS, LANES), I32),
                        pltpu.VMEM((RS, LANES), I32), pltpu.VMEM((RS, LANES), I32), pltpu.VMEM((RS, LANES), F32)],
        compiler_params=_cparams(("parallel",)),
        name="smp_select",
    )(sc, qs, ws, kin_t)


def _smp_attn2_kernel(pt_ref, q_ref, kn_ref, vn_ref, mn_ref, mp_ref, *refs, n_tok):
    k_refs = refs[:SMP_PG]
    v_refs = refs[SMP_PG:2 * SMP_PG]
    o_ref, kpad_s, vpad_s, m_s, l_s, acc_s = refs[2 * SMP_PG:]
    p = pl.program_id(1)
    NL = PAGE_SIZE * A_HEADS
    R = n_tok * A_HEADS
    c1 = (A_DH ** -0.5) * LOG2E
    diag = jnp.where(lax.broadcasted_iota(I32, (A_HEADS, NL), 1) % A_HEADS == lax.broadcasted_iota(I32, (A_HEADS, NL), 0), 1.0, 0.0)
    expand = jnp.where(lax.broadcasted_iota(I32, (PAGE_SIZE, NL), 1) // A_HEADS == lax.broadcasted_iota(I32, (PAGE_SIZE, NL), 0),
                       1.0, 0.0).astype(BF16)
    qb = q_ref[0].astype(BF16)

    def attend(k_list, v_list, masks):
        n = len(k_list)
        x = jnp.dot(jnp.concatenate(masks, axis=0).astype(BF16), expand, preferred_element_type=F32)
        ts = []
        for i in range(n):
            s = lax.dot_general(qb, k_list[i], (((1,), (1,)), ((), ())), preferred_element_type=F32)
            ok = jnp.concatenate([jnp.broadcast_to(x[i * SMP_ROWS + t:i * SMP_ROWS + t + 1, :], (A_HEADS, NL)) * diag
                                  for t in range(n_tok)], axis=0)
            ts.append(s * c1 + jnp.where(ok > 0.5, 0.0, NEG))
        m_old = m_s[...]
        m_new = jnp.maximum(m_old, jnp.max(jnp.concatenate(ts, axis=1), axis=1, keepdims=True))
        alpha = jnp.exp2(m_old - m_new)
        acc = alpha * acc_s[...]
        lsum = alpha * l_s[...]
        m_t = jnp.concatenate([m_new] * (NL // LANES), axis=1)
        for i in range(n):
            pr = jnp.exp2(ts[i] - m_t)
            for c in range(NL // LANES):
                lsum = lsum + pr[:, c * LANES:(c + 1) * LANES]
            acc = acc + jnp.dot(pr.astype(BF16), v_list[i], preferred_element_type=F32)
        m_s[...] = m_new
        l_s[...] = lsum
        acc_s[...] = acc

    @pl.when(p == 0)
    def _first():
        m_s[...] = jnp.full(m_s.shape, NEG, F32)
        l_s[...] = jnp.zeros(l_s.shape, F32)
        acc_s[...] = jnp.zeros(acc_s.shape, F32)
        kpad_s[...] = jnp.zeros(kpad_s.shape, F32)
        vpad_s[...] = jnp.zeros(vpad_s.shape, F32)
        kpad_s[0:R, :] = kn_ref[0]
        vpad_s[0:R, :] = vn_ref[0]
        attend([kpad_s[...].astype(BF16)], [vpad_s[...].astype(BF16)], [mn_ref[0]])

    attend([r[0].reshape(NL, A_DH).astype(BF16) for r in k_refs], [r[0].reshape(NL, A_DH).astype(BF16) for r in v_refs],
           [mp_ref[0][:, i * PAGE_SIZE:(i + 1) * PAGE_SIZE] for i in range(SMP_PG)])

    @pl.when(p == pl.num_programs(1) - 1)
    def _fin():
        o_ref[0] = acc_s[...] / jnp.sum(l_s[...], axis=1, keepdims=True)


def _smp_attn2(page_table, q32, kn32, vn32, mn, mp, ck, cv, n_tok):
    B, n_pages = page_table.shape
    R = n_tok * A_HEADS
    NL = PAGE_SIZE * A_HEADS
    per_b = lambda shape: pl.BlockSpec(shape, lambda b, p, pt: (b, 0, 0))
    page = lambda i: pl.BlockSpec((1, PAGE_SIZE, A_HEADS, A_DH), lambda b, p, pt: (pt[b, p * SMP_PG + i], 0, 0, 0))
    gs = pltpu.PrefetchScalarGridSpec(
        num_scalar_prefetch=1,
        grid=(B, n_pages // SMP_PG),
        in_specs=[per_b((1, R, A_DH)), per_b((1, R, A_DH)), per_b((1, R, A_DH)), per_b((1, SMP_ROWS, LANES)),
                  pl.BlockSpec((1, SMP_ROWS, SMP_PG * PAGE_SIZE), lambda b, p, pt: (b, 0, p))]
                 + [page(i) for i in range(SMP_PG)] + [page(i) for i in range(SMP_PG)],
        out_specs=per_b((1, R, A_DH)),
        scratch_shapes=[pltpu.VMEM((NL, A_DH), F32), pltpu.VMEM((NL, A_DH), F32),
                        pltpu.VMEM((R, LANES), F32), pltpu.VMEM((R, LANES), F32), pltpu.VMEM((R, A_DH), F32)])
    return pl.pallas_call(
        functools.partial(_smp_attn2_kernel, n_tok=n_tok), grid_spec=gs,
        out_shape=jax.ShapeDtypeStruct((B, R, A_DH), F32),
        compiler_params=_cparams(("parallel", "arbitrary")),
        name="smp_attn",
    )(page_table, q32, kn32, vn32, mn, mp, *([ck] * SMP_PG), *([cv] * SMP_PG))


def _merge_kernel(hm_ref, ha_ref, wm_ref, wa_ref, gm_ref, ga_ref, o_ref):
    a = jnp.dot(hm_ref[...], wm_ref[...], preferred_element_type=F32)
    b = jnp.dot(ha_ref[...], wa_ref[...], preferred_element_type=F32)
    o_ref[...] = (_sigmoid(gm_ref[...]) * a + _sigmoid(ga_ref[...]) * b).astype(o_ref.dtype)


def _merge(hm, ha, wm, wa, p32, tm):
    m = hm.shape[0]
    tn = PROJ_TN
    return pl.pallas_call(
        _merge_kernel,
        grid=(D_MODEL // tn, m // tm),
        in_specs=[pl.BlockSpec((tm, M_WIDTH), lambda j, i: (i, 0)), pl.BlockSpec((tm, A_WIDTH), lambda j, i: (i, 0)),
                  pl.BlockSpec((M_WIDTH, tn), lambda j, i: (0, j)), pl.BlockSpec((A_WIDTH, tn), lambda j, i: (0, j)),
                  pl.BlockSpec((tm, tn), lambda j, i: (i, C_GM // tn + j)),
                  pl.BlockSpec((tm, tn), lambda j, i: (i, C_GA // tn + j))],
        out_specs=pl.BlockSpec((tm, tn), lambda j, i: (i, j)),
        out_shape=jax.ShapeDtypeStruct((m, D_MODEL), BF16),
        compiler_params=_cparams(("parallel", "parallel")),
        name="merge",
    )(hm, ha, wm, wa, p32, p32)


def _outproj_kernel(mg_ref, w_ref, x_ref, o_ref):
    o_ref[...] = x_ref[...] + jnp.dot(mg_ref[...], w_ref[...], preferred_element_type=F32)


def _outproj(mg, w, x, tm):
    m = mg.shape[0]
    tn = PROJ_TN
    return pl.pallas_call(
        _outproj_kernel,
        grid=(D_MODEL // tn, m // tm),
        in_specs=[pl.BlockSpec((tm, D_MODEL), lambda j, i: (i, 0)), pl.BlockSpec((D_MODEL, tn), lambda j, i: (0, j)),
                  pl.BlockSpec((tm, tn), lambda j, i: (i, j))],
        out_specs=pl.BlockSpec((tm, tn), lambda j, i: (i, j)),
        out_shape=jax.ShapeDtypeStruct((m, D_MODEL), F32),
        compiler_params=_cparams(("parallel", "parallel")),
        name="outproj",
    )(mg, w, x)


MOE_TM = 256
MOE_NBUF = 3


def _router_kernel(x_ref, g_ref, wr_ref, br_ref, xn_ref, r_ref):
    x = x_ref[...]
    y = (x * lax.rsqrt(jnp.mean(x * x, axis=-1, keepdims=True) + RMS_EPS)) * g_ref[...]
    xn_ref[...] = y
    lg = jnp.dot(y.astype(BF16), wr_ref[...], preferred_element_type=F32) + br_ref[...]
    lane = lax.broadcasted_iota(I32, lg.shape, 1).astype(F32)
    far = float(LANES)
    gmask = lane < N_GROUPS
    gl = jnp.where(gmask, lg, NEG)
    mg = jnp.max(gl, axis=1, keepdims=True)
    p_g = 1.0 / jnp.sum(jnp.where(gmask, jnp.exp(gl - mg), 0.0), axis=1, keepdims=True)
    g_sel = jnp.min(jnp.where(gmask & (gl == mg), lane, far), axis=1, keepdims=True)
    e_lo = N_GROUPS + g_sel * EXP_PER_GROUP
    emask = (lane >= e_lo) & (lane < e_lo + EXP_PER_GROUP)
    el = jnp.where(emask, lg, NEG)
    me = jnp.max(el, axis=1, keepdims=True)
    pe = jnp.where(emask, jnp.exp(el - me), 0.0)
    probs = pe / jnp.sum(pe, axis=1, keepdims=True)
    p1 = jnp.max(probs, axis=1, keepdims=True)
    i1 = jnp.min(jnp.where(emask & (probs == p1), lane, far), axis=1, keepdims=True)
    probs2 = jnp.where(lane == i1, -1.0, probs)
    p2 = jnp.max(probs2, axis=1, keepdims=True)
    i2 = jnp.min(jnp.where(emask & (probs2 == p2), lane, far), axis=1, keepdims=True)
    tot = p1 + p2
    vals = [i1 - N_GROUPS, i2 - N_GROUPS, p_g * (p1 / tot), p_g * (p2 / tot)]
    out = jnp.zeros(lg.shape, F32)
    for c, v in enumerate(vals):
        out = jnp.where(lane == c, v, out)
    r_ref[...] = out


def _router(x1, g, wr, br, tm):
    m, d = x1.shape
    return pl.pallas_call(
        _router_kernel,
        grid=(m // tm,),
        in_specs=[pl.BlockSpec((tm, d), lambda i: (i, 0)), pl.BlockSpec((1, d), lambda i: (0, 0)),
                  pl.BlockSpec((d, LANES), lambda i: (0, 0)), pl.BlockSpec((1, LANES), lambda i: (0, 0))],
        out_specs=[pl.BlockSpec((tm, d), lambda i: (i, 0)), pl.BlockSpec((tm, LANES), lambda i: (i, 0))],
        out_shape=[jax.ShapeDtypeStruct((m, d), F32), jax.ShapeDtypeStruct((m, LANES), F32)],
        compiler_params=_cparams(("parallel",)),
        name="router",
    )(x1, g.reshape(1, d), wr, br)


def _row_copy(src_hbm, row, dst, r, sem):
    return pltpu.make_async_copy(src_hbm.at[pl.ds(row, 1), :], dst.at[pl.ds(r, 1), :], sem)


def _expert_kernel(be_ref, na_ref, src_ref, x_hbm, wg_ref, wu_ref, wd_ref, o_ref, xbuf, sem):
    blk = pl.program_id(0)
    slot = blk % MOE_NBUF
    ahead = MOE_NBUF - 1

    def gather(b):
        s = b % MOE_NBUF

        def start(r, _):
            _row_copy(x_hbm, src_ref[b * MOE_TM + r], xbuf.at[s], r, sem.at[s]).start()
            return 0

        lax.fori_loop(0, MOE_TM, start, 0, unroll=8)

    @pl.when(blk == 0)
    def _prime():
        for b in range(ahead):
            @pl.when(b < na_ref[0])
            def _():
                gather(b)

    @pl.when(blk + ahead < na_ref[0])
    def _prefetch():
        gather(blk + ahead)

    @pl.when(blk < na_ref[0])
    def _active():
        def wait(r, _):
            _row_copy(x_hbm, 0, xbuf.at[slot], r, sem.at[slot]).wait()
            return 0

        lax.fori_loop(0, MOE_TM, wait, 0, unroll=8)
        x = xbuf[slot].astype(BF16)
        hg = jnp.dot(x, wg_ref[0].astype(BF16), preferred_element_type=F32)
        hu = jnp.dot(x, wu_ref[0].astype(BF16), preferred_element_type=F32)
        h = (hg * _sigmoid(hg)) * hu
        o_ref[...] = jnp.dot(h.astype(BF16), wd_ref[0].astype(BF16), preferred_element_type=F32)

    @pl.when(blk >= na_ref[0])
    def _idle():
        o_ref[...] = jnp.zeros(o_ref.shape, F32)


def _experts(blk_exp, n_act, src, xn2, w_gate, w_up, w_down):
    npad = src.shape[0]
    d = xn2.shape[1]
    gs = pltpu.PrefetchScalarGridSpec(
        num_scalar_prefetch=3,
        grid=(npad // MOE_TM,),
        in_specs=[pl.BlockSpec(memory_space=pl.ANY),
                  pl.BlockSpec((1, d, D_EXPERT), lambda b, be, na, sr: (be[b], 0, 0)),
                  pl.BlockSpec((1, d, D_EXPERT), lambda b, be, na, sr: (be[b], 0, 0)),
                  pl.BlockSpec((1, D_EXPERT, d), lambda b, be, na, sr: (be[b], 0, 0))],
        out_specs=pl.BlockSpec((MOE_TM, d), lambda b, be, na, sr: (b, 0)),
        scratch_shapes=[pltpu.VMEM((MOE_NBUF, MOE_TM, d), F32), pltpu.SemaphoreType.DMA((MOE_NBUF,))])
    return pl.pallas_call(
        _expert_kernel, grid_spec=gs,
        out_shape=jax.ShapeDtypeStruct((npad, d), F32),
        compiler_params=_cparams(("arbitrary",)),
        name="experts",
    )(blk_exp, n_act, src, xn2, w_gate, w_up, w_down)


def _combine_kernel(pos_ref, ys_hbm, x1_ref, r_ref, g_ref, o_ref, buf, sem, *, TC, row0):
    base = (row0 + pl.program_id(0) * TC) * TOP_E

    def start(r, _):
        for s in range(TOP_E):
            _row_copy(ys_hbm, pos_ref[base + r * TOP_E + s], buf.at[s], r, sem).start()
        return 0

    def wait(r, _):
        for s in range(TOP_E):
            _row_copy(ys_hbm, 0, buf.at[s], r, sem).wait()
        return 0

    lax.fori_loop(0, TC, start, 0, unroll=8)
    lax.fori_loop(0, TC, wait, 0, unroll=8)
    gates = r_ref[:, TOP_E:2 * TOP_E]
    x = x1_ref[...]
    for s in range(TOP_E):
        x = x + gates[:, s:s + 1] * buf[s]
    y = x * lax.rsqrt(jnp.mean(x * x, axis=-1, keepdims=True) + RMS_EPS)
    o_ref[...] = y * g_ref[...]


def _combine(pos, ys, x1, r, g, row0, n, TC):
    d = x1.shape[1]
    gs = pltpu.PrefetchScalarGridSpec(
        num_scalar_prefetch=1,
        grid=(n // TC,),
        in_specs=[pl.BlockSpec(memory_space=pl.ANY),
                  pl.BlockSpec((TC, d), lambda i, ps: (row0 // TC + i, 0)),
                  pl.BlockSpec((TC, LANES), lambda i, ps: (row0 // TC + i, 0)),
                  pl.BlockSpec((1, d), lambda i, ps: (0, 0))],
        out_specs=pl.BlockSpec((TC, d), lambda i, ps: (i, 0)),
        scratch_shapes=[pltpu.VMEM((TOP_E, TC, d), F32), pltpu.SemaphoreType.DMA(())])
    return pl.pallas_call(
        functools.partial(_combine_kernel, TC=TC, row0=row0), grid_spec=gs,
        out_shape=jax.ShapeDtypeStruct((n, d), F32),
        compiler_params=_cparams(("arbitrary",)),
        name="combine",
    )(pos, ys, x1, r, g.reshape(1, d))


def _route_tables(r, npad):
    nt = r.shape[0]
    ef = r[:, 0:TOP_E].astype(I32).reshape(-1)
    onehot = (ef[:, None] == jnp.arange(N_EXPERTS, dtype=I32)[None, :]).astype(I32)
    csum = jnp.cumsum(onehot, axis=0)
    rank = jnp.sum(onehot * csum, axis=1) - 1
    nblk = (csum[-1] + MOE_TM - 1) // MOE_TM
    blk_end = jnp.cumsum(nblk)
    pos = (blk_end - nblk)[ef] * MOE_TM + rank
    n_act = blk_end[-1:]
    b = jnp.minimum(jnp.arange(npad // MOE_TM, dtype=I32), n_act[0] - 1)
    blk_exp = jnp.minimum(jnp.sum((blk_end[None, :] <= b[:, None]).astype(I32), axis=1), N_EXPERTS - 1)
    src = jnp.zeros((npad,), I32).at[pos].set(jnp.arange(nt * TOP_E, dtype=I32) // TOP_E)
    return blk_exp, n_act.astype(I32), src, pos.astype(I32)


def kernel(x_prompt, x_sample, cache_k, cache_v, cache_kidx, state_conv, state_C, state_n, state_m, page_table,
           g_attn, w_in, b_gates_m, conv_w, conv_b, m_norm_w, w_proj_m, w_proj_a, w_out, g_ffn,
           w_rg, b_rg, w_re, b_re, w_gate, w_up, w_down, g_final):
    assert x_prompt.shape[0] == 1 and g_attn.shape[0] == 1
    l = 0
    Tp = x_prompt.shape[1]
    Bs, Ts = x_sample.shape[:2]
    Ns = Bs * SMP_ROWS
    NT = Tp + Ns
    TM_BIG = 1408
    TM_ROW = 768
    assert NT % TM_BIG == 0 and NT % TM_ROW == 0 and Ts >= CONV_W - 1 and Ts <= SMP_ROWS
    P = page_table.shape[1] * PAGE_SIZE
    pad_rows = lambda a: jnp.pad(a, ((0, 0), (0, SMP_ROWS - Ts), (0, 0)))

    x_all = jnp.concatenate([x_prompt[0], pad_rows(x_sample).reshape(Ns, D_MODEL)], axis=0)
    xn = _rmsnorm(x_all, g_attn[l], BF16, TM_ROW)
    p32, p16 = _inproj(xn, _prep_w_in(w_in[l]), TM_BIG)

    gate_cols = slice(C_TAIL + T_IM, C_TAIL + T_IM + 2 * M_HEADS)
    ps8 = p32[Tp:].reshape(Bs, SMP_ROWS, D_CAT)
    ps3 = ps8[:, :Ts]

    zero = lambda *s: jnp.zeros(s, F32)
    hm_p, C_p, n_p, m_p = _mlstm(p32[None], p32[:Tp, gate_cols].T[None], zero(1, SUBLANES, 2 * M_WIDTH),
                                 conv_w[l], conv_b[l], b_gates_m[l], m_norm_w[l],
                                 zero(1, M_HEADS, M_DH, M_DH), zero(1, M_HEADS, M_DH), zero(1, M_HEADS),
                                 T=Tp, L=256, RIN=256, valid=256)
    grow_s = jnp.pad(jnp.swapaxes(ps3[:, :, gate_cols], 1, 2), ((0, 0), (0, 0), (0, LANES - Ts)))
    cb_s = jnp.pad(state_conv[l], ((0, 0), (SUBLANES - (CONV_W - 1), 0), (0, 0)))
    hm_s, C_s, n_s, m_s = _mlstm(ps8, grow_s, cb_s, conv_w[l], conv_b[l], b_gates_m[l], m_norm_w[l],
                                 state_C[l], state_n[l], state_m[l], T=SUBLANES, L=LANES, RIN=SUBLANES, valid=Ts)

    ha_p = _dsa_prompt(p32, p16, Tp, min(TOPK_MAX, Tp // 4))
    qs = ps3[:, :, C_QI:C_QI + IDX_HEADS * LANES].reshape(Bs, Ts, IDX_HEADS, LANES)[..., :IDX_DIM].reshape(Bs, Ts * IDX_HEADS, IDX_DIM)
    ws = ps3[:, :, C_TAIL + T_WI:C_TAIL + T_WI + IDX_HEADS].reshape(Bs, Ts * IDX_HEADS, 1)
    kin_t = jnp.pad(jnp.swapaxes(ps3[:, :, C_TAIL + T_KI:C_TAIL + T_KI + IDX_DIM], 1, 2), ((0, 0), (0, 0), (0, LANES - Ts)))
    sc = _smp_scores2(page_table, qs, ws, jnp.swapaxes(cache_kidx[l], 1, 2))
    mp, mn = _smp_select2(sc.reshape(Bs * SMP_ROWS, P), qs, ws, kin_t, Ts, min(TOPK_MAX, (P + Ts) // 4))
    rows_th = lambda c0: ps3[:, :, c0:c0 + A_WIDTH].reshape(Bs, Ts * A_HEADS, A_DH)
    ha_s = _smp_attn2(page_table, rows_th(C_QA), rows_th(C_KA), rows_th(C_VA), mn.reshape(Bs, SMP_ROWS, LANES),
                      mp.reshape(Bs, SMP_ROWS, P), cache_k[l], cache_v[l], Ts)

    hm_all = jnp.concatenate([hm_p[0], hm_s.reshape(Ns, M_WIDTH)], axis=0)
    ha_all = jnp.concatenate([ha_p, pad_rows(ha_s.reshape(Bs, Ts, A_WIDTH)).reshape(Ns, A_WIDTH).astype(BF16)], axis=0)
    merged = _merge(hm_all, ha_all, w_proj_m[l].astype(BF16), w_proj_a[l].astype(BF16), p32, TM_BIG)
    x1 = _outproj(merged, w_out[l].astype(BF16), x_all, TM_BIG)

    wr = jnp.pad(jnp.concatenate([w_rg[l], w_re[l]], axis=1), ((0, 0), (0, LANES - N_GROUPS - N_EXPERTS))).astype(BF16)
    br = jnp.pad(jnp.concatenate([b_rg[l], b_re[l]]), (0, LANES - N_GROUPS - N_EXPERTS)).reshape(1, LANES)
    xn2, r = _router(x1, g_ffn[l], wr, br, TM_ROW)
    npad = NT * TOP_E + N_EXPERTS * MOE_TM
    blk_exp, n_act, src, pos = _route_tables(r, npad)
    ys = _experts(blk_exp, n_act, src, xn2, w_gate[l], w_up[l], w_down[l])
    y_p = _combine(pos, ys, x1, r, g_final, 0, Tp, 256)
    y_s = _combine(pos, ys, x1, r, g_final, Tp, Ns, 256)

    st = lambda a, shape: a.reshape((1,) + shape)
    pp = p32[:Tp]
    return (y_p[None], y_s.reshape(Bs, SMP_ROWS, D_MODEL)[:, :Ts],
            st(pp[:, C_KA:C_KA + A_WIDTH], (1, Tp, A_HEADS, A_DH)), st(pp[:, C_VA:C_VA + A_WIDTH], (1, Tp, A_HEADS, A_DH)),
            st(pp[:, C_TAIL + T_KI:C_TAIL + T_KI + IDX_DIM], (1, Tp, IDX_DIM)),
            st(pp[Tp - (CONV_W - 1):, 0:2 * M_WIDTH], (1, CONV_W - 1, 2 * M_WIDTH)),
            st(C_p, (1, M_HEADS, M_DH, M_DH)), st(n_p, (1, M_HEADS, M_DH)), st(m_p[:, :, 0, 0], (1, M_HEADS)),
            st(ps3[:, :, C_KA:C_KA + A_WIDTH], (Bs, Ts, A_HEADS, A_DH)), st(ps3[:, :, C_VA:C_VA + A_WIDTH], (Bs, Ts, A_HEADS, A_DH)),
            st(ps3[:, :, C_TAIL + T_KI:C_TAIL + T_KI + IDX_DIM], (Bs, Ts, IDX_DIM)),
            st(ps3[:, Ts - (CONV_W - 1):, 0:2 * M_WIDTH], (Bs, CONV_W - 1, 2 * M_WIDTH)),
            st(C_s, (Bs, M_HEADS, M_DH, M_DH)), st(n_s, (Bs, M_HEADS, M_DH)), st(m_s[:, :, 0, 0], (Bs, M_HEADS)))
```

```python
import functools

import jax
import jax.numpy as jnp
import numpy as np
from jax import lax
from jax.experimental import pallas as pl
from jax.experimental.pallas import tpu as pltpu

F32 = jnp.float32
BF16 = jnp.bfloat16
I32 = jnp.int32

D_MODEL = 2048
M_WIDTH = D_MODEL // 2
M_HEADS = 4
M_DH = M_WIDTH // M_HEADS
CONV_W = 4
A_WIDTH = D_MODEL // 2
A_DH = 128
A_HEADS = A_WIDTH // A_DH
IDX_HEADS = 8
IDX_DIM = 64
TOPK_MAX = 256
PAGE_SIZE = 128
N_GROUPS = 4
EXP_PER_GROUP = 8
N_EXPERTS = N_GROUPS * EXP_PER_GROUP
TOP_E = 2
D_EXPERT = D_MODEL // 4
RMS_EPS = 1e-6
IN_SIZES = (M_WIDTH, M_WIDTH, M_WIDTH, M_WIDTH, M_HEADS, M_HEADS, A_WIDTH, A_WIDTH, A_WIDTH,
            IDX_HEADS * IDX_DIM, IDX_DIM, IDX_HEADS, D_MODEL, D_MODEL)
IN_SPLITS = tuple(int(s) for s in np.cumsum(IN_SIZES)[:-1])

LANES = 128
SUBLANES = 8
VMEM_LIMIT = 56 * 1024 * 1024

C_QM, C_KM, C_VM, C_OM = 0, 1024, 2048, 3072
C_QA, C_KA, C_VA = 4096, 5120, 6144
C_GM, C_GA = 7168, 9216
C_QI = 11264
C_TAIL = 12288
T_KI, T_WI, T_IM, T_FM = 0, 64, 72, 76
D_CAT = 12800
PROJ_TN = 512

NEG = -1e30
INT_MIN = -2 ** 31


def _cparams(sem):
    return pltpu.CompilerParams(dimension_semantics=sem, vmem_limit_bytes=VMEM_LIMIT)


def _rms_kernel(x_ref, g_ref, o_ref):
    x = x_ref[...]
    y = x * lax.rsqrt(jnp.mean(x * x, axis=-1, keepdims=True) + RMS_EPS)
    o_ref[...] = (y * g_ref[...]).astype(o_ref.dtype)


def _rmsnorm(x, g, out_dtype, tm):
    m, d = x.shape
    return pl.pallas_call(
        _rms_kernel,
        grid=(m // tm,),
        in_specs=[pl.BlockSpec((tm, d), lambda i: (i, 0)), pl.BlockSpec((1, d), lambda i: (0, 0))],
        out_specs=pl.BlockSpec((tm, d), lambda i: (i, 0)),
        out_shape=jax.ShapeDtypeStruct((m, d), out_dtype),
        compiler_params=_cparams(("parallel",)),
        name="rmsnorm",
    )(x, g.reshape(1, d))


def _inproj_kernel(x_ref, wt_ref, o32_ref, o16_ref):
    acc = lax.dot_general(x_ref[...], wt_ref[...], (((1,), (1,)), ((), ())), preferred_element_type=F32)
    o32_ref[...] = acc
    o16_ref[...] = acc.astype(BF16)


def _inproj(xn, w_cat_t, tm):
    m, d = xn.shape
    n = w_cat_t.shape[0]
    tn = PROJ_TN
    return pl.pallas_call(
        _inproj_kernel,
        grid=(m // tm, n // tn),
        in_specs=[pl.BlockSpec((tm, d), lambda i, j: (i, 0)), pl.BlockSpec((tn, d), lambda i, j: (j, 0))],
        out_specs=[pl.BlockSpec((tm, tn), lambda i, j: (i, j)), pl.BlockSpec((tm, tn), lambda i, j: (i, j))],
        out_shape=[jax.ShapeDtypeStruct((m, n), F32), jax.ShapeDtypeStruct((m, n), BF16)],
        compiler_params=_cparams(("parallel", "parallel")),
        name="inproj",
    )(xn, w_cat_t)


def _prep_w_in(w_in):
    (q_m, k_m, v_m, o_m, i_m, f_m, q_a, k_a, v_a, q_i, k_i, w_i, g_m, g_a) = jnp.split(w_in.T, IN_SPLITS, axis=0)
    d = w_in.shape[0]
    q_i = jnp.pad(q_i.reshape(IDX_HEADS, IDX_DIM, d), ((0, 0), (0, LANES - IDX_DIM), (0, 0))).reshape(IDX_HEADS * LANES, d)
    rows = [q_m, k_m, v_m, o_m, q_a, k_a, v_a, g_m, g_a, q_i, k_i, w_i, i_m, f_m]
    rows.append(jnp.zeros((D_CAT - sum(a.shape[0] for a in rows), d), w_in.dtype))
    return jnp.concatenate(rows, axis=0).astype(BF16)


def _sigmoid(x):
    return 1.0 / (1.0 + jnp.exp(-x))


def _log_sigmoid(x):
    return jnp.minimum(x, 0.0) - jnp.log1p(jnp.exp(-jnp.abs(x)))


def _mlstm_kernel(q_ref, k_ref, v_ref, o_ref, tail_ref, grow_ref, cb_ref, convw_ref, convb_ref, bl_ref, bs_ref,
                  nw_ref, c0_ref, n0_ref, m0_ref,
                  h_ref, cout_ref, nout_ref, mout_ref,
                  xq_s, xk_s, c_s, n_s, m_s, vp_s, op_s, tp_s, *, L, RIN, valid):
    c = pl.program_id(1)
    nc = pl.num_programs(1)

    @pl.when(c == 0)
    def _init():
        xq_s[0:SUBLANES, :] = cb_ref[0, :, 0:M_WIDTH]
        xk_s[0:SUBLANES, :] = cb_ref[0, :, M_WIDTH:2 * M_WIDTH]
        c_s[...] = c0_ref[0]
        n_s[...] = n0_ref[0]
        m_s[...] = m0_ref[0]

    if RIN < L:
        zpad = jnp.zeros((L - RIN, M_WIDTH), F32)
        xq_s[SUBLANES + RIN:SUBLANES + L, :] = zpad
        xk_s[SUBLANES + RIN:SUBLANES + L, :] = zpad
        vp_s[RIN:L, :] = zpad
        op_s[RIN:L, :] = zpad
        tp_s[RIN:L, :] = jnp.zeros((L - RIN, LANES), F32)
    xq_s[SUBLANES:SUBLANES + RIN, :] = q_ref[0]
    xk_s[SUBLANES:SUBLANES + RIN, :] = k_ref[0]
    vp_s[0:RIN, :] = v_ref[0]
    op_s[0:RIN, :] = o_ref[0]
    tp_s[0:RIN, :] = tail_ref[0]

    def conv(xs, col0):
        w = convw_ref[:, col0:col0 + M_WIDTH]
        y = convb_ref[:, col0:col0 + M_WIDTH]
        for j in range(CONV_W):
            r0 = SUBLANES - (CONV_W - 1) + j
            y = y + xs[r0:r0 + L, :] * w[j:j + 1, :]
        return y * _sigmoid(y)

    q_all = conv(xq_s, 0) * (M_DH ** -0.5)
    k_all = conv(xk_s, M_WIDTH)
    v_all = vp_s[...]
    tail = tp_s[...] + bl_ref[...]
    grow = grow_ref[0] + bs_ref[...]

    xq_s[0:SUBLANES, :] = xq_s[L:L + SUBLANES, :]
    xk_s[0:SUBLANES, :] = xk_s[L:L + SUBLANES, :]

    tt = lax.broadcasted_iota(I32, (L, L), 0)
    ss = lax.broadcasted_iota(I32, (L, L), 1)
    causal = ss <= tt
    row_ok = lax.broadcasted_iota(I32, (L, 1), 0) < valid
    col_ok = lax.broadcasted_iota(I32, (1, L), 1) < valid

    for h in range(M_HEADS):
        hs = slice(h * M_DH, (h + 1) * M_DH)
        qh = q_all[:, hs]
        kh = k_all[:, hs]
        vh = v_all[:, hs]
        ig_c = jnp.where(row_ok, tail[:, T_IM + h:T_IM + h + 1], NEG)
        lf_c = jnp.where(row_ok, _log_sigmoid(tail[:, T_FM + h:T_FM + h + 1]), 0.0)
        ig_r = jnp.where(col_ok, grow[h:h + 1, :], NEG)
        lf_r = jnp.where(col_ok, _log_sigmoid(grow[M_HEADS + h:M_HEADS + h + 1, :]), 0.0)
        b_c = jnp.sum(jnp.where(causal, lf_r, 0.0), axis=1, keepdims=True)
        b_r = jnp.sum(jnp.where(tt <= ss, lf_c, 0.0), axis=0, keepdims=True)
        dmat = jnp.where(causal, b_c - b_r + ig_r, NEG)
        m_prev = m_s[h][:, 0:1]
        m_t = jnp.maximum(b_c + m_prev, jnp.max(dmat, axis=1, keepdims=True))
        e = jnp.exp(dmat - m_t)
        qb = qh.astype(BF16)
        kb = kh.astype(BF16)
        s = lax.dot_general(qb, kb, (((1,), (1,)), ((), ())), preferred_element_type=F32) * e
        inter = jnp.exp(b_c + m_prev - m_t)
        ch = c_s[h]
        num = jnp.dot(s.astype(BF16), vh.astype(BF16), preferred_element_type=F32) + inter * lax.dot_general(
            qb, ch.astype(BF16), (((1,), (1,)), ((), ())), preferred_element_type=F32)
        nh = n_s[h]
        den = jnp.sum(s, axis=1, keepdims=True) + inter * jnp.sum(qh * nh, axis=1, keepdims=True)
        hh = num / jnp.maximum(jnp.abs(den), jnp.exp(-m_t))
        hh = hh * lax.rsqrt(jnp.mean(hh * hh, axis=1, keepdims=True) + RMS_EPS)
        out = _sigmoid(op_s[:, hs]) * (hh * nw_ref[:, hs])
        h_ref[0, :, hs] = out[0:RIN, :].astype(h_ref.dtype)
        m_new = m_t[L - 1:L, :]
        b_last = b_c[L - 1:L, :]
        w_c = jnp.exp(b_last - b_c + ig_c - m_new)
        decay = jnp.exp(b_last + m_prev - m_new)
        upd = lax.dot_general((w_c * vh).astype(BF16), kb, (((0,), (0,)), ((), ())), preferred_element_type=F32)
        c_s[h] = decay * ch + upd
        n_s[h] = decay * nh + jnp.sum(w_c * kh, axis=0, keepdims=True)
        m_s[h] = jnp.broadcast_to(m_new, (1, LANES))

    @pl.when(c == nc - 1)
    def _fin():
        cout_ref[0] = c_s[...]
        nout_ref[0] = n_s[...]
        mout_ref[0] = m_s[...]


def _mlstm(p32, grow, convbuf, conv_w, conv_b, b_gates, m_norm_w, c0, n0, m0, *, T, L, RIN, valid):
    B = p32.shape[0]
    nc = T // RIN
    bl = jnp.zeros((1, LANES), F32).at[0, T_IM:T_IM + 2 * M_HEADS].set(b_gates)
    bs = jnp.broadcast_to(b_gates[:, None], (2 * M_HEADS, L))
    kern = functools.partial(_mlstm_kernel, L=L, RIN=RIN, valid=valid)
    cblk = lambda col: pl.BlockSpec((1, RIN, M_WIDTH), lambda b, c, col=col: (b, c, col // M_WIDTH))
    const2 = lambda shape: pl.BlockSpec(shape, lambda b, c: (0, 0))
    per_b = lambda shape: pl.BlockSpec(shape, lambda b, c: (b,) + (0,) * (len(shape) - 1))
    return pl.pallas_call(
        kern,
        grid=(B, nc),
        in_specs=[cblk(C_QM), cblk(C_KM), cblk(C_VM), cblk(C_OM),
                  pl.BlockSpec((1, RIN, LANES), lambda b, c: (b, c, C_TAIL // LANES)),
                  pl.BlockSpec((1, 2 * M_HEADS, L), lambda b, c: (b, 0, c)),
                  per_b((1, SUBLANES, 2 * M_WIDTH)),
                  const2((CONV_W, 2 * M_WIDTH)), const2((1, 2 * M_WIDTH)), const2((1, LANES)),
                  const2((2 * M_HEADS, L)), const2((1, M_WIDTH)),
                  per_b((1, M_HEADS, M_DH, M_DH)), per_b((1, M_HEADS, 1, M_DH)), per_b((1, M_HEADS, 1, LANES))],
        out_specs=[pl.BlockSpec((1, RIN, M_WIDTH), lambda b, c: (b, c, 0)),
                   per_b((1, M_HEADS, M_DH, M_DH)), per_b((1, M_HEADS, 1, M_DH)), per_b((1, M_HEADS, 1, LANES))],
        out_shape=[jax.ShapeDtypeStruct((B, T, M_WIDTH), BF16),
                   jax.ShapeDtypeStruct((B, M_HEADS, M_DH, M_DH), F32),
                   jax.ShapeDtypeStruct((B, M_HEADS, 1, M_DH), F32),
                   jax.ShapeDtypeStruct((B, M_HEADS, 1, LANES), F32)],
        scratch_shapes=[pltpu.VMEM((SUBLANES + L, M_WIDTH), F32), pltpu.VMEM((SUBLANES + L, M_WIDTH), F32),
                        pltpu.VMEM((M_HEADS, M_DH, M_DH), F32), pltpu.VMEM((M_HEADS, 1, M_DH), F32),
                        pltpu.VMEM((M_HEADS, 1, LANES), F32),
                        pltpu.VMEM((L, M_WIDTH), F32), pltpu.VMEM((L, M_WIDTH), F32), pltpu.VMEM((L, LANES), F32)],
        compiler_params=_cparams(("parallel", "arbitrary")),
        name="mlstm",
    )(p32, p32, p32, p32, p32, grow, convbuf, conv_w, conv_b.reshape(1, -1), bl, bs, m_norm_w.reshape(1, -1),
      c0, n0.reshape(B, M_HEADS, 1, M_DH), jnp.broadcast_to(m0[:, :, None, None], (B, M_HEADS, 1, LANES)))


def _score_key(sc):
    bits = lax.bitcast_convert_type(sc, I32)
    return jnp.where(bits < 0, INT_MIN - bits, bits)


DSA_TQ = 512
DSA_TK = 512
DSA_RG = 64
DSA_RB = 512
LOG2E = 1.4426950408889634


def _dsa_prompt_kernel(qi_tab, kj_tab, qidx_ref, tail_ref, kit_ref, qa_ref, ka_ref, va_ref, o_ref,
                       keys_s, cand_s, cnt_s, thr_s, thrm_s, cut_s, bias_s, tie_s, wrep_s, m_s, l_s, acc_s,
                       *, TQ, TK, topk, pos_bits):
    step = pl.program_id(0)
    qi = qi_tab[step]
    kj = kj_tab[step]
    RG = DSA_RG
    nlc = TK // LANES

    def count_pass(pred):
        def rbody(r, _):
            r0 = pl.multiple_of(r * RG, RG)
            cand = cand_s[pl.ds(r0, RG), :]
            aux = thr_s[pl.ds(r0, RG), :]

            def kb(j, cnt):
                for c in range(nlc):
                    blk = keys_s[j, pl.ds(r0, RG), c * LANES:(c + 1) * LANES]
                    pos = j * TK + c * LANES + lax.broadcasted_iota(I32, (RG, LANES), 1)
                    cnt = cnt + jnp.where(pred(blk, cand, aux, pos), 1.0, 0.0)
                return cnt

            cnt_s[pl.ds(r0, RG), :] = lax.fori_loop(0, qi + 1, kb, jnp.zeros((RG, LANES), F32))
            return 0

        lax.fori_loop(0, TQ // RG, rbody, 0)
        return jnp.sum(cnt_s[...], axis=1, keepdims=True)

    @pl.when(kj == 0)
    def _phase1():
        w = tail_ref[:, T_WI:T_WI + IDX_HEADS] * (IDX_HEADS ** -0.5) * (IDX_DIM ** -0.5)
        for h in range(IDX_HEADS):
            wrep_s[h] = jnp.broadcast_to(w[:, h:h + 1], (TQ, LANES))
        rowpos = qi * TQ + lax.broadcasted_iota(I32, (TQ, TK), 0)

        def kbody(j, _):
            kt = kit_ref[j].astype(BF16)
            sc = jnp.zeros((TQ, TK), F32)
            for h in range(IDX_HEADS):
                qh = qidx_ref[:, h * LANES:(h + 1) * LANES].astype(BF16)
                s = jnp.dot(qh, kt, preferred_element_type=F32)
                sc = sc + jnp.maximum(s, 0.0) * jnp.concatenate([wrep_s[h]] * nlc, axis=1)
            colpos = j * TK + lax.broadcasted_iota(I32, (TQ, TK), 1)
            keys_s[j] = jnp.where(colpos <= rowpos, _score_key(sc), INT_MIN)
            return 0

        lax.fori_loop(0, qi + 1, kbody, 0)

        ge = lambda blk, cand, aux, pos: blk >= cand
        cand_s[...] = jnp.zeros((TQ, LANES), I32)
        cnt = count_pass(ge)
        lo = jnp.where(cnt >= topk, 0, INT_MIN).astype(I32)

        def bit_body(b, carry):
            lo, n_lo = carry
            cand = lo | lax.shift_left(jnp.int32(1), 30 - b)
            cand_s[...] = jnp.broadcast_to(cand, (TQ, LANES))
            cnt = count_pass(ge)
            take = cnt >= topk
            return jnp.where(take, cand, lo), jnp.where(take, cnt, n_lo)

        thr, n_ge = lax.fori_loop(0, 31, bit_body, (lo, cnt))
        thr_s[...] = jnp.broadcast_to(thr, (TQ, LANES))
        short = thr == INT_MIN
        thrm_s[...] = jnp.broadcast_to(jnp.where(short, INT_MIN, thr - 1), (TQ, LANES))
        cut_s[...] = jnp.broadcast_to(jnp.where(short, -1, 2 ** 30).astype(I32), (TQ, LANES))
        surplus = jnp.max(jnp.where((n_ge > topk) & jnp.logical_not(short), 1.0, 0.0))
        tie_s[0] = (surplus > 0.0).astype(I32)

        @pl.when(surplus > 0.0)
        def _ties():
            cand_s[...] = thr_s[...]
            need = topk - count_pass(lambda blk, cand, aux, pos: blk > cand)
            eq_below = lambda blk, cand, aux, pos: (blk == aux) & (pos < cand)

            def tie_body(b, x):
                cand = x + lax.shift_left(jnp.int32(1), pos_bits - 1 - b)
                cand_s[...] = jnp.broadcast_to(cand, (TQ, LANES))
                cnt = count_pass(eq_below)
                return jnp.where(cnt < need, cand, x)

            x = lax.fori_loop(0, pos_bits, tie_body, jnp.zeros((TQ, 1), I32))
            cut_s[...] = jnp.broadcast_to(jnp.where(short, -1, x), (TQ, LANES))

        m_s[...] = jnp.full(m_s.shape, NEG, F32)
        l_s[...] = jnp.zeros(l_s.shape, F32)
        acc_s[...] = jnp.zeros(acc_s.shape, F32)

    tile_l = lambda a: jnp.concatenate([a] * nlc, axis=1)

    @pl.when(tie_s[0] == 0)
    def _bias_plain():
        bias_s[...] = jnp.where(keys_s[kj] > tile_l(thrm_s[...]), 0.0, NEG)

    @pl.when(tie_s[0] != 0)
    def _bias_tied():
        key = keys_s[kj]
        thr = tile_l(thr_s[...])
        colpos = kj * TK + lax.broadcasted_iota(I32, (TQ, TK), 1)
        sel = (key > thr) | ((key == thr) & (colpos <= tile_l(cut_s[...])))
        bias_s[...] = jnp.where(sel, 0.0, NEG)

    RB = DSA_RB
    c1 = (A_DH ** -0.5) * LOG2E

    def rb_body(rb, _):
        rows = pl.ds(pl.multiple_of(rb * RB, RB), RB)
        bias = bias_s[rows, :]
        hsl = [slice(h * A_DH, (h + 1) * A_DH) for h in range(A_HEADS)]
        m_old = [m_s[h, rows, :] for h in range(A_HEADS)]
        l_old = [l_s[h, rows, :] for h in range(A_HEADS)]
        a_old = [acc_s[rows, hs] for hs in hsl]
        ts = [lax.dot_general(qa_ref[rows, hs], ka_ref[:, hs], (((1,), (1,)), ((), ())), preferred_element_type=F32) * c1 + bias
              for hs in hsl]
        m_new = [jnp.maximum(m_old[h], jnp.max(ts[h], axis=1, keepdims=True)) for h in range(A_HEADS)]
        l_new, a_new = [], []
        for h in range(A_HEADS):
            alpha = jnp.exp2(m_old[h] - m_new[h])
            p = jnp.exp2(ts[h] - tile_l(m_new[h]))
            psum = p[:, 0:LANES]
            for c in range(1, nlc):
                psum = psum + p[:, c * LANES:(c + 1) * LANES]
            l_new.append(alpha * l_old[h] + psum)
            a_new.append(alpha * a_old[h] + jnp.dot(p.astype(BF16), va_ref[:, hsl[h]], preferred_element_type=F32))
        for h in range(A_HEADS):
            m_s[h, rows, :] = m_new[h]
            l_s[h, rows, :] = l_new[h]
            acc_s[rows, hsl[h]] = a_new[h]
        return 0

    lax.fori_loop(0, TQ // RB, rb_body, 0)

    @pl.when(kj == qi)
    def _fin():
        for h in range(A_HEADS):
            hs = slice(h * A_DH, (h + 1) * A_DH)
            o_ref[:, hs] = (acc_s[:, hs] / jnp.sum(l_s[h], axis=1, keepdims=True)).astype(o_ref.dtype)


def _dsa_prompt(p32, p16, T, topk):
    TQ, TK = DSA_TQ, DSA_TK
    nq = T // TQ
    assert TQ == TK
    qi_tab = np.concatenate([np.full(i + 1, i) for i in range(nq)]).astype(np.int32)
    kj_tab = np.concatenate([np.arange(i + 1) for i in range(nq)]).astype(np.int32)
    ki = p32[:T, C_TAIL + T_KI:C_TAIL + T_KI + IDX_DIM]
    kit = jnp.pad(ki.T, ((0, LANES - IDX_DIM), (0, 0))).reshape(LANES, T // TK, TK).transpose(1, 0, 2)
    kern = functools.partial(_dsa_prompt_kernel, TQ=TQ, TK=TK, topk=topk, pos_bits=int(T - 1).bit_length())
    gs = pltpu.PrefetchScalarGridSpec(
        num_scalar_prefetch=2,
        grid=(len(qi_tab),),
        in_specs=[pl.BlockSpec((TQ, IDX_HEADS * LANES), lambda s, qt, kt: (qt[s], C_QI // (IDX_HEADS * LANES))),
                  pl.BlockSpec((TQ, LANES), lambda s, qt, kt: (qt[s], C_TAIL // LANES)),
                  pl.BlockSpec((T // TK, LANES, TK), lambda s, qt, kt: (0, 0, 0)),
                  pl.BlockSpec((TQ, A_WIDTH), lambda s, qt, kt: (qt[s], C_QA // A_WIDTH)),
                  pl.BlockSpec((TK, A_WIDTH), lambda s, qt, kt: (kt[s], C_KA // A_WIDTH)),
                  pl.BlockSpec((TK, A_WIDTH), lambda s, qt, kt: (kt[s], C_VA // A_WIDTH))],
        out_specs=pl.BlockSpec((TQ, A_WIDTH), lambda s, qt, kt: (qt[s], 0)),
        scratch_shapes=[pltpu.VMEM((T // TK, TQ, TK), I32),
                        pltpu.VMEM((TQ, LANES), I32), pltpu.VMEM((TQ, LANES), F32), pltpu.VMEM((TQ, LANES), I32),
                        pltpu.VMEM((TQ, LANES), I32), pltpu.VMEM((TQ, LANES), I32),
                        pltpu.VMEM((TQ, TK), F32), pltpu.SMEM((1,), I32), pltpu.VMEM((IDX_HEADS, TQ, LANES), F32),
                        pltpu.VMEM((A_HEADS, TQ, LANES), F32), pltpu.VMEM((A_HEADS, TQ, LANES), F32),
                        pltpu.VMEM((TQ, A_WIDTH), F32)])
    return pl.pallas_call(
        kern, grid_spec=gs,
        out_shape=jax.ShapeDtypeStruct((T, A_WIDTH), BF16),
        compiler_params=_cparams(("arbitrary",)),
        name="dsa_prompt",
    )(jnp.asarray(qi_tab), jnp.asarray(kj_tab), p32, p32, kit, p16, p16, p16)


SMP_PGS = 32
SMP_PG = 8
SMP_ROWS = SUBLANES
SMP_SEL_B = 16


def _idx_scores_t(q, w, kt):
    s = jnp.dot(q.astype(BF16), kt.astype(BF16), preferred_element_type=F32)
    s = jnp.maximum(s * (IDX_DIM ** -0.5), 0.0) * (w * (IDX_HEADS ** -0.5))
    n_tok = q.shape[0] // IDX_HEADS
    rows = [jnp.sum(s[t * IDX_HEADS:(t + 1) * IDX_HEADS], axis=0, keepdims=True) for t in range(n_tok)]
    rows.append(jnp.zeros((SMP_ROWS - n_tok, s.shape[1]), F32))
    return jnp.concatenate(rows, axis=0)


def _smp_scores2_kernel(pt_ref, q_ref, w_ref, *refs):
    o_ref = refs[-1]
    kt = jnp.concatenate([r[0] for r in refs[:-1]], axis=1)
    o_ref[0] = _idx_scores_t(q_ref[0], w_ref[0], kt)


def _smp_scores2(page_table, qs, ws, kidx_t):
    B, n_pages = page_table.shape
    R = qs.shape[1]
    page = lambda i: pl.BlockSpec((1, IDX_DIM, PAGE_SIZE), lambda b, p, pt: (pt[b, p * SMP_PGS + i], 0, 0))
    gs = pltpu.PrefetchScalarGridSpec(
        num_scalar_prefetch=1,
        grid=(B, n_pages // SMP_PGS),
        in_specs=[pl.BlockSpec((1, R, IDX_DIM), lambda b, p, pt: (b, 0, 0)),
                  pl.BlockSpec((1, R, 1), lambda b, p, pt: (b, 0, 0))] + [page(i) for i in range(SMP_PGS)],
        out_specs=pl.BlockSpec((1, SMP_ROWS, SMP_PGS * PAGE_SIZE), lambda b, p, pt: (b, 0, p)))
    return pl.pallas_call(
        _smp_scores2_kernel, grid_spec=gs,
        out_shape=jax.ShapeDtypeStruct((B, SMP_ROWS, n_pages * PAGE_SIZE), F32),
        compiler_params=_cparams(("parallel", "arbitrary")),
        name="smp_scores",
    )(page_table, qs, ws, *([kidx_t] * SMP_PGS))


def _smp_select2_kernel(sc_ref, q_ref, w_ref, kint_ref, mp_ref, mn_ref, keys_s, cand_s, thr_s, cut_s, cnt_s,
                        *, n_tok, topk):
    NB = q_ref.shape[0]
    R = NB * SMP_ROWS
    P = sc_ref.shape[1]
    NCH = P // LANES
    RG = DSA_RG
    lane = lax.broadcasted_iota(I32, (R, LANES), 1)
    trow = lax.broadcasted_iota(I32, (R, LANES), 0) % SMP_ROWS
    for c in range(NCH):
        keys_s[c] = _score_key(sc_ref[:, c * LANES:(c + 1) * LANES])
    s_new = jnp.concatenate([_idx_scores_t(q_ref[b], w_ref[b], kint_ref[b]) for b in range(NB)], axis=0)
    keys_s[NCH] = jnp.where(lane <= trow, _score_key(s_new), INT_MIN)

    def count_pass(pred):
        def rbody(r, _):
            rows = pl.ds(pl.multiple_of(r * RG, RG), RG)
            cand = cand_s[rows, :]
            aux = thr_s[rows, :]

            def cb(c, cnt):
                pos = c * LANES + lax.broadcasted_iota(I32, (RG, LANES), 1)
                return cnt + jnp.where(pred(keys_s[c, rows, :], cand, aux, pos), 1.0, 0.0)

            cnt_s[rows, :] = lax.fori_loop(0, NCH + 1, cb, jnp.zeros((RG, LANES), F32))
            return 0

        lax.fori_loop(0, R // RG, rbody, 0)
        return jnp.sum(cnt_s[...], axis=1, keepdims=True)

    ge = lambda blk, cand, aux, pos: blk >= cand
    cand_s[...] = jnp.zeros((R, LANES), I32)
    cnt0 = count_pass(ge)
    lo = jnp.where(cnt0 >= topk, 0, INT_MIN).astype(I32)

    def bit_body(b, carry):
        lo, n_lo = carry
        cand = lo | lax.shift_left(jnp.int32(1), 30 - b)
        cand_s[...] = jnp.broadcast_to(cand, (R, LANES))
        cnt = count_pass(ge)
        take = cnt >= topk
        return jnp.where(take, cand, lo), jnp.where(take, cnt, n_lo)

    thr, n_ge = lax.fori_loop(0, 31, bit_body, (lo, cnt0))
    thr_s[...] = jnp.broadcast_to(thr, (R, LANES))
    cut_s[...] = jnp.full((R, LANES), 2 ** 30, I32)
    real_row = lax.broadcasted_iota(I32, (R, 1), 0) % SMP_ROWS < n_tok
    surplus = jnp.max(jnp.where((n_ge > topk) & (thr != INT_MIN) & real_row, 1.0, 0.0))

    @pl.when(surplus > 0.0)
    def _ties():
        cand_s[...] = thr_s[...]
        need = topk - count_pass(lambda blk, cand, aux, pos: blk > cand)
        pos_bits = int(P + LANES - 1).bit_length()
        eq_below = lambda blk, cand, aux, pos: (blk == aux) & (pos < cand)

        def tie_body(b, x):
            cand = x + lax.shift_left(jnp.int32(1), pos_bits - 1 - b)
            cand_s[...] = jnp.broadcast_to(cand, (R, LANES))
            return jnp.where(count_pass(eq_below) < need, cand, x)

        cut_s[...] = jnp.broadcast_to(lax.fori_loop(0, pos_bits, tie_body, jnp.zeros((R, 1), I32)), (R, LANES))

    cut = cut_s[...]
    thr_b = thr_s[...]
    row_ok = trow < n_tok
    for c in range(NCH + 1):
        key = keys_s[c]
        sel = ((key > thr_b) | ((key == thr_b) & (c * LANES + lane <= cut))) & (key != INT_MIN) & row_ok
        if c < NCH:
            mp_ref[:, c * LANES:(c + 1) * LANES] = jnp.where(sel, 1.0, 0.0)
        else:
            mn_ref[...] = jnp.where(sel, 1.0, 0.0)


def _smp_select2(sc, qs, ws, kin_t, n_tok, topk):
    R, P = sc.shape
    B = qs.shape[0]
    NB = SMP_SEL_B
    RS = NB * SMP_ROWS
    Rq = qs.shape[1]
    return pl.pallas_call(
        functools.partial(_smp_select2_kernel, n_tok=n_tok, topk=topk),
        grid=(B // NB,),
        in_specs=[pl.BlockSpec((RS, P), lambda i: (i, 0)),
                  pl.BlockSpec((NB, Rq, IDX_DIM), lambda i: (i, 0, 0)), pl.BlockSpec((NB, Rq, 1), lambda i: (i, 0, 0)),
                  pl.BlockSpec((NB, IDX_DIM, LANES), lambda i: (i, 0, 0))],
        out_specs=[pl.BlockSpec((RS, P), lambda i: (i, 0)), pl.BlockSpec((RS, LANES), lambda i: (i, 0))],
        out_shape=[jax.ShapeDtypeStruct((R, P), F32), jax.ShapeDtypeStruct((R, LANES), F32)],
        scratch_shapes=[pltpu.VMEM((P // LANES + 1, RS, LANES), I32), pltpu.VMEM((RS, LANES), I32),
                        pltpu.VMEM((RS, LANES), I32), pltpu.VMEM((RS, LANES), I32), pltpu.VMEM((RS, LANES), F32)],
        compiler_params=_cparams(("parallel",)),
        name="smp_select",
    )(sc, qs, ws, kin_t)


def _smp_attn2_kernel(pt_ref, q_ref, kn_ref, vn_ref, mn_ref, mp_ref, *refs, n_tok):
    k_refs = refs[:SMP_PG]
    v_refs = refs[SMP_PG:2 * SMP_PG]
    o_ref, kpad_s, vpad_s, m_s, l_s, acc_s = refs[2 * SMP_PG:]
    p = pl.program_id(1)
    NL = PAGE_SIZE * A_HEADS
    R = n_tok * A_HEADS
    c1 = (A_DH ** -0.5) * LOG2E
    diag = jnp.where(lax.broadcasted_iota(I32, (A_HEADS, NL), 1) % A_HEADS == lax.broadcasted_iota(I32, (A_HEADS, NL), 0), 1.0, 0.0)
    expand = jnp.where(lax.broadcasted_iota(I32, (PAGE_SIZE, NL), 1) // A_HEADS == lax.broadcasted_iota(I32, (PAGE_SIZE, NL), 0),
                       1.0, 0.0).astype(BF16)
    qb = q_ref[0].astype(BF16)

    def attend(k_list, v_list, masks):
        n = len(k_list)
        x = jnp.dot(jnp.concatenate(masks, axis=0).astype(BF16), expand, preferred_element_type=F32)
        ts = []
        for i in range(n):
            s = lax.dot_general(qb, k_list[i], (((1,), (1,)), ((), ())), preferred_element_type=F32)
            ok = jnp.concatenate([jnp.broadcast_to(x[i * SMP_ROWS + t:i * SMP_ROWS + t + 1, :], (A_HEADS, NL)) * diag
                                  for t in range(n_tok)], axis=0)
            ts.append(s * c1 + jnp.where(ok > 0.5, 0.0, NEG))
        m_old = m_s[...]
        m_new = jnp.maximum(m_old, jnp.max(jnp.concatenate(ts, axis=1), axis=1, keepdims=True))
        alpha = jnp.exp2(m_old - m_new)
        acc = alpha * acc_s[...]
        lsum = alpha * l_s[...]
        m_t = jnp.concatenate([m_new] * (NL // LANES), axis=1)
        for i in range(n):
            pr = jnp.exp2(ts[i] - m_t)
            for c in range(NL // LANES):
                lsum = lsum + pr[:, c * LANES:(c + 1) * LANES]
            acc = acc + jnp.dot(pr.astype(BF16), v_list[i], preferred_element_type=F32)
        m_s[...] = m_new
        l_s[...] = lsum
        acc_s[...] = acc

    @pl.when(p == 0)
    def _first():
        m_s[...] = jnp.full(m_s.shape, NEG, F32)
        l_s[...] = jnp.zeros(l_s.shape, F32)
        acc_s[...] = jnp.zeros(acc_s.shape, F32)
        kpad_s[...] = jnp.zeros(kpad_s.shape, F32)
        vpad_s[...] = jnp.zeros(vpad_s.shape, F32)
        kpad_s[0:R, :] = kn_ref[0]
        vpad_s[0:R, :] = vn_ref[0]
        attend([kpad_s[...].astype(BF16)], [vpad_s[...].astype(BF16)], [mn_ref[0]])

    attend([r[0].reshape(NL, A_DH).astype(BF16) for r in k_refs], [r[0].reshape(NL, A_DH).astype(BF16) for r in v_refs],
           [mp_ref[0][:, i * PAGE_SIZE:(i + 1) * PAGE_SIZE] for i in range(SMP_PG)])

    @pl.when(p == pl.num_programs(1) - 1)
    def _fin():
        o_ref[0] = acc_s[...] / jnp.sum(l_s[...], axis=1, keepdims=True)


def _smp_attn2(page_table, q32, kn32, vn32, mn, mp, ck, cv, n_tok):
    B, n_pages = page_table.shape
    R = n_tok * A_HEADS
    NL = PAGE_SIZE * A_HEADS
    per_b = lambda shape: pl.BlockSpec(shape, lambda b, p, pt: (b, 0, 0))
    page = lambda i: pl.BlockSpec((1, PAGE_SIZE, A_HEADS, A_DH), lambda b, p, pt: (pt[b, p * SMP_PG + i], 0, 0, 0))
    gs = pltpu.PrefetchScalarGridSpec(
        num_scalar_prefetch=1,
        grid=(B, n_pages // SMP_PG),
        in_specs=[per_b((1, R, A_DH)), per_b((1, R, A_DH)), per_b((1, R, A_DH)), per_b((1, SMP_ROWS, LANES)),
                  pl.BlockSpec((1, SMP_ROWS, SMP_PG * PAGE_SIZE), lambda b, p, pt: (b, 0, p))]
                 + [page(i) for i in range(SMP_PG)] + [page(i) for i in range(SMP_PG)],
        out_specs=per_b((1, R, A_DH)),
        scratch_shapes=[pltpu.VMEM((NL, A_DH), F32), pltpu.VMEM((NL, A_DH), F32),
                        pltpu.VMEM((R, LANES), F32), pltpu.VMEM((R, LANES), F32), pltpu.VMEM((R, A_DH), F32)])
    return pl.pallas_call(
        functools.partial(_smp_attn2_kernel, n_tok=n_tok), grid_spec=gs,
        out_shape=jax.ShapeDtypeStruct((B, R, A_DH), F32),
        compiler_params=_cparams(("parallel", "arbitrary")),
        name="smp_attn",
    )(page_table, q32, kn32, vn32, mn, mp, *([ck] * SMP_PG), *([cv] * SMP_PG))


def _merge_kernel(hm_ref, ha_ref, wm_ref, wa_ref, gm_ref, ga_ref, o_ref):
    a = jnp.dot(hm_ref[...], wm_ref[...], preferred_element_type=F32)
    b = jnp.dot(ha_ref[...], wa_ref[...], preferred_element_type=F32)
    o_ref[...] = (_sigmoid(gm_ref[...]) * a + _sigmoid(ga_ref[...]) * b).astype(o_ref.dtype)


def _merge(hm, ha, wm, wa, p32, tm):
    m = hm.shape[0]
    tn = PROJ_TN
    return pl.pallas_call(
        _merge_kernel,
        grid=(D_MODEL // tn, m // tm),
        in_specs=[pl.BlockSpec((tm, M_WIDTH), lambda j, i: (i, 0)), pl.BlockSpec((tm, A_WIDTH), lambda j, i: (i, 0)),
                  pl.BlockSpec((M_WIDTH, tn), lambda j, i: (0, j)), pl.BlockSpec((A_WIDTH, tn), lambda j, i: (0, j)),
                  pl.BlockSpec((tm, tn), lambda j, i: (i, C_GM // tn + j)),
                  pl.BlockSpec((tm, tn), lambda j, i: (i, C_GA // tn + j))],
        out_specs=pl.BlockSpec((tm, tn), lambda j, i: (i, j)),
        out_shape=jax.ShapeDtypeStruct((m, D_MODEL), BF16),
        compiler_params=_cparams(("parallel", "parallel")),
        name="merge",
    )(hm, ha, wm, wa, p32, p32)


def _outproj_kernel(mg_ref, w_ref, x_ref, o_ref):
    o_ref[...] = x_ref[...] + jnp.dot(mg_ref[...], w_ref[...], preferred_element_type=F32)


def _outproj(mg, w, x, tm):
    m = mg.shape[0]
    tn = PROJ_TN
    return pl.pallas_call(
        _outproj_kernel,
        grid=(D_MODEL // tn, m // tm),
        in_specs=[pl.BlockSpec((tm, D_MODEL), lambda j, i: (i, 0)), pl.BlockSpec((D_MODEL, tn), lambda j, i: (0, j)),
                  pl.BlockSpec((tm, tn), lambda j, i: (i, j))],
        out_specs=pl.BlockSpec((tm, tn), lambda j, i: (i, j)),
        out_shape=jax.ShapeDtypeStruct((m, D_MODEL), F32),
        compiler_params=_cparams(("parallel", "parallel")),
        name="outproj",
    )(mg, w, x)


MOE_TM = 256
MOE_NBUF = 3
MOE_RG = 8


def _router_kernel(x_ref, g_ref, wr_ref, br_ref, xn_ref, r_ref):
    x = x_ref[...]
    y = (x * lax.rsqrt(jnp.mean(x * x, axis=-1, keepdims=True) + RMS_EPS)) * g_ref[...]
    xn_ref[...] = y
    lg = jnp.dot(y.astype(BF16), wr_ref[...], preferred_element_type=F32) + br_ref[...]
    lane = lax.broadcasted_iota(I32, lg.shape, 1).astype(F32)
    far = float(LANES)
    gmask = lane < N_GROUPS
    gl = jnp.where(gmask, lg, NEG)
    mg = jnp.max(gl, axis=1, keepdims=True)
    p_g = 1.0 / jnp.sum(jnp.where(gmask, jnp.exp(gl - mg), 0.0), axis=1, keepdims=True)
    g_sel = jnp.min(jnp.where(gmask & (gl == mg), lane, far), axis=1, keepdims=True)
    e_lo = N_GROUPS + g_sel * EXP_PER_GROUP
    emask = (lane >= e_lo) & (lane < e_lo + EXP_PER_GROUP)
    el = jnp.where(emask, lg, NEG)
    me = jnp.max(el, axis=1, keepdims=True)
    pe = jnp.where(emask, jnp.exp(el - me), 0.0)
    probs = pe / jnp.sum(pe, axis=1, keepdims=True)
    p1 = jnp.max(probs, axis=1, keepdims=True)
    i1 = jnp.min(jnp.where(emask & (probs == p1), lane, far), axis=1, keepdims=True)
    probs2 = jnp.where(lane == i1, -1.0, probs)
    p2 = jnp.max(probs2, axis=1, keepdims=True)
    i2 = jnp.min(jnp.where(emask & (probs2 == p2), lane, far), axis=1, keepdims=True)
    tot = p1 + p2
    vals = [i1 - N_GROUPS, i2 - N_GROUPS, p_g * (p1 / tot), p_g * (p2 / tot)]
    out = jnp.zeros(lg.shape, F32)
    for c, v in enumerate(vals):
        out = jnp.where(lane == c, v, out)
    r_ref[...] = out


def _router(x1, g, wr, br, tm):
    m, d = x1.shape
    return pl.pallas_call(
        _router_kernel,
        grid=(m // tm,),
        in_specs=[pl.BlockSpec((tm, d), lambda i: (i, 0)), pl.BlockSpec((1, d), lambda i: (0, 0)),
                  pl.BlockSpec((d, LANES), lambda i: (0, 0)), pl.BlockSpec((1, LANES), lambda i: (0, 0))],
        out_specs=[pl.BlockSpec((tm, d), lambda i: (i, 0)), pl.BlockSpec((tm, LANES), lambda i: (i, 0))],
        out_shape=[jax.ShapeDtypeStruct((m, d), F32), jax.ShapeDtypeStruct((m, LANES), F32)],
        compiler_params=_cparams(("parallel",)),
        name="router",
    )(x1, g.reshape(1, d), wr, br)


def _row_copy(src_hbm, row, dst, r, sem):
    return pltpu.make_async_copy(src_hbm.at[pl.ds(row, 1), :], dst.at[pl.ds(r, 1), :], sem)


def _expert_kernel(be_ref, na_ref, ng_ref, src_ref, x_hbm, wg_ref, wu_ref, wd_ref, o_ref, xbuf, sem):
    blk = pl.program_id(0)
    slot = blk % MOE_NBUF
    ahead = MOE_NBUF - 1

    def gather(b):
        s = b % MOE_NBUF

        def start(g, _):
            for i in range(MOE_RG):
                r = g * MOE_RG + i
                _row_copy(x_hbm, src_ref[b * MOE_TM + r], xbuf.at[s], r, sem.at[s]).start()
            return 0

        lax.fori_loop(0, ng_ref[b], start, 0)

    @pl.when(blk == 0)
    def _prime():
        xbuf[...] = jnp.zeros(xbuf.shape, F32)
        for b in range(ahead):
            @pl.when(b < na_ref[0])
            def _():
                gather(b)

    @pl.when(blk + ahead < na_ref[0])
    def _prefetch():
        gather(blk + ahead)

    @pl.when(blk < na_ref[0])
    def _active():
        def wait(g, _):
            for i in range(MOE_RG):
                _row_copy(x_hbm, 0, xbuf.at[slot], g * MOE_RG + i, sem.at[slot]).wait()
            return 0

        lax.fori_loop(0, ng_ref[blk], wait, 0)
        x = xbuf[slot].astype(BF16)
        hg = jnp.dot(x, wg_ref[0].astype(BF16), preferred_element_type=F32)
        hu = jnp.dot(x, wu_ref[0].astype(BF16), preferred_element_type=F32)
        h = (hg * _sigmoid(hg)) * hu
        o_ref[...] = jnp.dot(h.astype(BF16), wd_ref[0].astype(BF16), preferred_element_type=F32)

    @pl.when(blk >= na_ref[0])
    def _idle():
        o_ref[...] = jnp.zeros(o_ref.shape, F32)


def _experts(blk_exp, n_act, blk_groups, src, xn2, w_gate, w_up, w_down):
    npad = src.shape[0]
    d = xn2.shape[1]
    gs = pltpu.PrefetchScalarGridSpec(
        num_scalar_prefetch=4,
        grid=(npad // MOE_TM,),
        in_specs=[pl.BlockSpec(memory_space=pl.ANY),
                  pl.BlockSpec((1, d, D_EXPERT), lambda b, be, na, ng, sr: (be[b], 0, 0)),
                  pl.BlockSpec((1, d, D_EXPERT), lambda b, be, na, ng, sr: (be[b], 0, 0)),
                  pl.BlockSpec((1, D_EXPERT, d), lambda b, be, na, ng, sr: (be[b], 0, 0))],
        out_specs=pl.BlockSpec((MOE_TM, d), lambda b, be, na, ng, sr: (b, 0)),
        scratch_shapes=[pltpu.VMEM((MOE_NBUF, MOE_TM, d), F32), pltpu.SemaphoreType.DMA((MOE_NBUF,))])
    return pl.pallas_call(
        _expert_kernel, grid_spec=gs,
        out_shape=jax.ShapeDtypeStruct((npad, d), F32),
        compiler_params=_cparams(("arbitrary",)),
        name="experts",
    )(blk_exp, n_act, blk_groups, src, xn2, w_gate, w_up, w_down)


def _combine_kernel(pos_ref, ys_hbm, x1_ref, r_ref, g_ref, o_ref, buf, sem, *, TC, row0):
    base = (row0 + pl.program_id(0) * TC) * TOP_E

    def start(r, _):
        for s in range(TOP_E):
            _row_copy(ys_hbm, pos_ref[base + r * TOP_E + s], buf.at[s], r, sem).start()
        return 0

    def wait(r, _):
        for s in range(TOP_E):
            _row_copy(ys_hbm, 0, buf.at[s], r, sem).wait()
        return 0

    lax.fori_loop(0, TC, start, 0, unroll=8)
    lax.fori_loop(0, TC, wait, 0, unroll=8)
    gates = r_ref[:, TOP_E:2 * TOP_E]
    x = x1_ref[...]
    for s in range(TOP_E):
        x = x + gates[:, s:s + 1] * buf[s]
    y = x * lax.rsqrt(jnp.mean(x * x, axis=-1, keepdims=True) + RMS_EPS)
    o_ref[...] = y * g_ref[...]


def _combine(pos, ys, x1, r, g, row0, n, TC):
    d = x1.shape[1]
    gs = pltpu.PrefetchScalarGridSpec(
        num_scalar_prefetch=1,
        grid=(n // TC,),
        in_specs=[pl.BlockSpec(memory_space=pl.ANY),
                  pl.BlockSpec((TC, d), lambda i, ps: (row0 // TC + i, 0)),
                  pl.BlockSpec((TC, LANES), lambda i, ps: (row0 // TC + i, 0)),
                  pl.BlockSpec((1, d), lambda i, ps: (0, 0))],
        out_specs=pl.BlockSpec((TC, d), lambda i, ps: (i, 0)),
        scratch_shapes=[pltpu.VMEM((TOP_E, TC, d), F32), pltpu.SemaphoreType.DMA(())])
    return pl.pallas_call(
        functools.partial(_combine_kernel, TC=TC, row0=row0), grid_spec=gs,
        out_shape=jax.ShapeDtypeStruct((n, d), F32),
        compiler_params=_cparams(("arbitrary",)),
        name="combine",
    )(pos, ys, x1, r, g.reshape(1, d))


def _route_tables(r, npad):
    nt = r.shape[0]
    ef = r[:, 0:TOP_E].astype(I32).reshape(-1)
    onehot = (ef[:, None] == jnp.arange(N_EXPERTS, dtype=I32)[None, :]).astype(I32)
    csum = jnp.cumsum(onehot, axis=0)
    rank = jnp.sum(onehot * csum, axis=1) - 1
    cnt = csum[-1]
    nblk = (cnt + MOE_TM - 1) // MOE_TM
    blk_end = jnp.cumsum(nblk)
    blk_start = blk_end - nblk
    pos = blk_start[ef] * MOE_TM + rank
    n_act = blk_end[-1:]
    ball = jnp.arange(npad // MOE_TM, dtype=I32)
    b = jnp.minimum(ball, n_act[0] - 1)
    blk_exp = jnp.minimum(jnp.sum((blk_end[None, :] <= b[:, None]).astype(I32), axis=1), N_EXPERTS - 1)
    rows = jnp.clip(cnt[blk_exp] - (ball - blk_start[blk_exp]) * MOE_TM, 0, MOE_TM)
    blk_groups = jnp.where(ball < n_act[0], (rows + MOE_RG - 1) // MOE_RG, 0).astype(I32)
    src = jnp.zeros((npad,), I32).at[pos].set(jnp.arange(nt * TOP_E, dtype=I32) // TOP_E)
    return blk_exp, n_act.astype(I32), blk_groups, src, pos.astype(I32)


def kernel(x_prompt, x_sample, cache_k, cache_v, cache_kidx, state_conv, state_C, state_n, state_m, page_table,
           g_attn, w_in, b_gates_m, conv_w, conv_b, m_norm_w, w_proj_m, w_proj_a, w_out, g_ffn,
           w_rg, b_rg, w_re, b_re, w_gate, w_up, w_down, g_final):
    assert x_prompt.shape[0] == 1 and g_attn.shape[0] == 1
    l = 0
    Tp = x_prompt.shape[1]
    Bs, Ts = x_sample.shape[:2]
    Ns = Bs * SMP_ROWS
    NT = Tp + Ns
    TM_BIG = 1408
    TM_ROW = 768
    assert NT % TM_BIG == 0 and NT % TM_ROW == 0 and Ts >= CONV_W - 1 and Ts <= SMP_ROWS
    P = page_table.shape[1] * PAGE_SIZE
    pad_rows = lambda a: jnp.pad(a, ((0, 0), (0, SMP_ROWS - Ts), (0, 0)))

    x_all = jnp.concatenate([x_prompt[0], pad_rows(x_sample).reshape(Ns, D_MODEL)], axis=0)
    xn = _rmsnorm(x_all, g_attn[l], BF16, TM_ROW)
    p32, p16 = _inproj(xn, _prep_w_in(w_in[l]), TM_BIG)

    gate_cols = slice(C_TAIL + T_IM, C_TAIL + T_IM + 2 * M_HEADS)
    ps8 = p32[Tp:].reshape(Bs, SMP_ROWS, D_CAT)
    ps3 = ps8[:, :Ts]

    zero = lambda *s: jnp.zeros(s, F32)
    hm_p, C_p, n_p, m_p = _mlstm(p32[None], p32[:Tp, gate_cols].T[None], zero(1, SUBLANES, 2 * M_WIDTH),
                                 conv_w[l], conv_b[l], b_gates_m[l], m_norm_w[l],
                                 zero(1, M_HEADS, M_DH, M_DH), zero(1, M_HEADS, M_DH), zero(1, M_HEADS),
                                 T=Tp, L=256, RIN=256, valid=256)
    grow_s = jnp.pad(jnp.swapaxes(ps3[:, :, gate_cols], 1, 2), ((0, 0), (0, 0), (0, LANES - Ts)))
    cb_s = jnp.pad(state_conv[l], ((0, 0), (SUBLANES - (CONV_W - 1), 0), (0, 0)))
    hm_s, C_s, n_s, m_s = _mlstm(ps8, grow_s, cb_s, conv_w[l], conv_b[l], b_gates_m[l], m_norm_w[l],
                                 state_C[l], state_n[l], state_m[l], T=SUBLANES, L=LANES, RIN=SUBLANES, valid=Ts)

    ha_p = _dsa_prompt(p32, p16, Tp, min(TOPK_MAX, Tp // 4))
    qs = ps3[:, :, C_QI:C_QI + IDX_HEADS * LANES].reshape(Bs, Ts, IDX_HEADS, LANES)[..., :IDX_DIM].reshape(Bs, Ts * IDX_HEADS, IDX_DIM)
    ws = ps3[:, :, C_TAIL + T_WI:C_TAIL + T_WI + IDX_HEADS].reshape(Bs, Ts * IDX_HEADS, 1)
    kin_t = jnp.pad(jnp.swapaxes(ps3[:, :, C_TAIL + T_KI:C_TAIL + T_KI + IDX_DIM], 1, 2), ((0, 0), (0, 0), (0, LANES - Ts)))
    sc = _smp_scores2(page_table, qs, ws, jnp.swapaxes(cache_kidx[l], 1, 2))
    mp, mn = _smp_select2(sc.reshape(Bs * SMP_ROWS, P), qs, ws, kin_t, Ts, min(TOPK_MAX, (P + Ts) // 4))
    rows_th = lambda c0: ps3[:, :, c0:c0 + A_WIDTH].reshape(Bs, Ts * A_HEADS, A_DH)
    ha_s = _smp_attn2(page_table, rows_th(C_QA), rows_th(C_KA), rows_th(C_VA), mn.reshape(Bs, SMP_ROWS, LANES),
                      mp.reshape(Bs, SMP_ROWS, P), cache_k[l], cache_v[l], Ts)

    hm_all = jnp.concatenate([hm_p[0], hm_s.reshape(Ns, M_WIDTH)], axis=0)
    ha_all = jnp.concatenate([ha_p, pad_rows(ha_s.reshape(Bs, Ts, A_WIDTH)).reshape(Ns, A_WIDTH).astype(BF16)], axis=0)
    merged = _merge(hm_all, ha_all, w_proj_m[l].astype(BF16), w_proj_a[l].astype(BF16), p32, TM_BIG)
    x1 = _outproj(merged, w_out[l].astype(BF16), x_all, TM_BIG)

    wr = jnp.pad(jnp.concatenate([w_rg[l], w_re[l]], axis=1), ((0, 0), (0, LANES - N_GROUPS - N_EXPERTS))).astype(BF16)
    br = jnp.pad(jnp.concatenate([b_rg[l], b_re[l]]), (0, LANES - N_GROUPS - N_EXPERTS)).reshape(1, LANES)
    xn2, r = _router(x1, g_ffn[l], wr, br, TM_ROW)
    npad = NT * TOP_E + N_EXPERTS * MOE_TM
    blk_exp, n_act, blk_groups, src, pos = _route_tables(r, npad)
    ys = _experts(blk_exp, n_act, blk_groups, src, xn2, w_gate[l], w_up[l], w_down[l])
    y_p = _combine(pos, ys, x1, r, g_final, 0, Tp, 256)
    y_s = _combine(pos, ys, x1, r, g_final, Tp, Ns, 256)

    st = lambda a, shape: a.reshape((1,) + shape)
    pp = p32[:Tp]
    return (y_p[None], y_s.reshape(Bs, SMP_ROWS, D_MODEL)[:, :Ts],
            st(pp[:, C_KA:C_KA + A_WIDTH], (1, Tp, A_HEADS, A_DH)), st(pp[:, C_VA:C_VA + A_WIDTH], (1, Tp, A_HEADS, A_DH)),
            st(pp[:, C_TAIL + T_KI:C_TAIL + T_KI + IDX_DIM], (1, Tp, IDX_DIM)),
            st(pp[Tp - (CONV_W - 1):, 0:2 * M_WIDTH], (1, CONV_W - 1, 2 * M_WIDTH)),
            st(C_p, (1, M_HEADS, M_DH, M_DH)), st(n_p, (1, M_HEADS, M_DH)), st(m_p[:, :, 0, 0], (1, M_HEADS)),
            st(ps3[:, :, C_KA:C_KA + A_WIDTH], (Bs, Ts, A_HEADS, A_DH)), st(ps3[:, :, C_VA:C_VA + A_WIDTH], (Bs, Ts, A_HEADS, A_DH)),
            st(ps3[:, :, C_TAIL + T_KI:C_TAIL + T_KI + IDX_DIM], (Bs, Ts, IDX_DIM)),
            st(ps3[:, Ts - (CONV_W - 1):, 0:2 * M_WIDTH], (Bs, CONV_W - 1, 2 * M_WIDTH)),
            st(C_s, (Bs, M_HEADS, M_DH, M_DH)), st(n_s, (Bs, M_HEADS, M_DH)), st(m_s[:, :, 0, 0], (Bs, M_HEADS)))
```

```python
import functools

import jax
import jax.numpy as jnp
import numpy as np
from jax import lax
from jax.experimental import pallas as pl
from jax.experimental.pallas import tpu as pltpu

F32 = jnp.float32
BF16 = jnp.bfloat16
I32 = jnp.int32

D_MODEL = 2048
M_WIDTH = D_MODEL // 2
M_HEADS = 4
M_DH = M_WIDTH // M_HEADS
CONV_W = 4
A_WIDTH = D_MODEL // 2
A_DH = 128
A_HEADS = A_WIDTH // A_DH
IDX_HEADS = 8
IDX_DIM = 64
TOPK_MAX = 256
PAGE_SIZE = 128
N_GROUPS = 4
EXP_PER_GROUP = 8
N_EXPERTS = N_GROUPS * EXP_PER_GROUP
TOP_E = 2
D_EXPERT = D_MODEL // 4
RMS_EPS = 1e-6
IN_SIZES = (M_WIDTH, M_WIDTH, M_WIDTH, M_WIDTH, M_HEADS, M_HEADS, A_WIDTH, A_WIDTH, A_WIDTH,
            IDX_HEADS * IDX_DIM, IDX_DIM, IDX_HEADS, D_MODEL, D_MODEL)
IN_SPLITS = tuple(int(s) for s in np.cumsum(IN_SIZES)[:-1])

LANES = 128
SUBLANES = 8
VMEM_LIMIT = 56 * 1024 * 1024

C_QM, C_KM, C_VM, C_OM = 0, 1024, 2048, 3072
C_QA, C_KA, C_VA = 4096, 5120, 6144
C_GM, C_GA = 7168, 9216
C_QI = 11264
C_TAIL = 12288
T_KI, T_WI, T_IM, T_FM = 0, 64, 72, 76
D_CAT = 12800
PROJ_TN = 512

NEG = -1e30
INT_MIN = -2 ** 31


def _cparams(sem):
    return pltpu.CompilerParams(dimension_semantics=sem, vmem_limit_bytes=VMEM_LIMIT)


def _rms_kernel(x_ref, g_ref, o_ref):
    x = x_ref[...]
    y = x * lax.rsqrt(jnp.mean(x * x, axis=-1, keepdims=True) + RMS_EPS)
    o_ref[...] = (y * g_ref[...]).astype(o_ref.dtype)


def _rmsnorm(x, g, out_dtype, tm):
    m, d = x.shape
    return pl.pallas_call(
        _rms_kernel,
        grid=(m // tm,),
        in_specs=[pl.BlockSpec((tm, d), lambda i: (i, 0)), pl.BlockSpec((1, d), lambda i: (0, 0))],
        out_specs=pl.BlockSpec((tm, d), lambda i: (i, 0)),
        out_shape=jax.ShapeDtypeStruct((m, d), out_dtype),
        compiler_params=_cparams(("parallel",)),
        name="rmsnorm",
    )(x, g.reshape(1, d))


def _inproj_kernel(x_ref, wt_ref, o32_ref, o16_ref):
    acc = lax.dot_general(x_ref[...], wt_ref[...], (((1,), (1,)), ((), ())), preferred_element_type=F32)
    o32_ref[...] = acc
    o16_ref[...] = acc.astype(BF16)


def _inproj(xn, w_cat_t, tm):
    m, d = xn.shape
    n = w_cat_t.shape[0]
    tn = PROJ_TN
    return pl.pallas_call(
        _inproj_kernel,
        grid=(m // tm, n // tn),
        in_specs=[pl.BlockSpec((tm, d), lambda i, j: (i, 0)), pl.BlockSpec((tn, d), lambda i, j: (j, 0))],
        out_specs=[pl.BlockSpec((tm, tn), lambda i, j: (i, j)), pl.BlockSpec((tm, tn), lambda i, j: (i, j))],
        out_shape=[jax.ShapeDtypeStruct((m, n), F32), jax.ShapeDtypeStruct((m, n), BF16)],
        compiler_params=_cparams(("parallel", "parallel")),
        name="inproj",
    )(xn, w_cat_t)


def _prep_w_in(w_in):
    (q_m, k_m, v_m, o_m, i_m, f_m, q_a, k_a, v_a, q_i, k_i, w_i, g_m, g_a) = jnp.split(w_in.T, IN_SPLITS, axis=0)
    d = w_in.shape[0]
    q_i = jnp.pad(q_i.reshape(IDX_HEADS, IDX_DIM, d), ((0, 0), (0, LANES - IDX_DIM), (0, 0))).reshape(IDX_HEADS * LANES, d)
    rows = [q_m, k_m, v_m, o_m, q_a, k_a, v_a, g_m, g_a, q_i, k_i, w_i, i_m, f_m]
    rows.append(jnp.zeros((D_CAT - sum(a.shape[0] for a in rows), d), w_in.dtype))
    return jnp.concatenate(rows, axis=0).astype(BF16)


def _sigmoid(x):
    return 1.0 / (1.0 + jnp.exp(-x))


def _log_sigmoid(x):
    return jnp.minimum(x, 0.0) - jnp.log1p(jnp.exp(-jnp.abs(x)))


def _mlstm_kernel(q_ref, k_ref, v_ref, o_ref, tail_ref, grow_ref, cb_ref, convw_ref, convb_ref, bl_ref, bs_ref,
                  nw_ref, c0_ref, n0_ref, m0_ref,
                  h_ref, cout_ref, nout_ref, mout_ref,
                  xq_s, xk_s, c_s, n_s, m_s, vp_s, op_s, tp_s, *, L, RIN, valid):
    c = pl.program_id(1)
    nc = pl.num_programs(1)

    @pl.when(c == 0)
    def _init():
        xq_s[0:SUBLANES, :] = cb_ref[0, :, 0:M_WIDTH]
        xk_s[0:SUBLANES, :] = cb_ref[0, :, M_WIDTH:2 * M_WIDTH]
        c_s[...] = c0_ref[0]
        n_s[...] = n0_ref[0]
        m_s[...] = m0_ref[0]

    if RIN < L:
        zpad = jnp.zeros((L - RIN, M_WIDTH), F32)
        xq_s[SUBLANES + RIN:SUBLANES + L, :] = zpad
        xk_s[SUBLANES + RIN:SUBLANES + L, :] = zpad
        vp_s[RIN:L, :] = zpad
        op_s[RIN:L, :] = zpad
        tp_s[RIN:L, :] = jnp.zeros((L - RIN, LANES), F32)
    xq_s[SUBLANES:SUBLANES + RIN, :] = q_ref[0]
    xk_s[SUBLANES:SUBLANES + RIN, :] = k_ref[0]
    vp_s[0:RIN, :] = v_ref[0]
    op_s[0:RIN, :] = o_ref[0]
    tp_s[0:RIN, :] = tail_ref[0]

    def conv(xs, col0):
        w = convw_ref[:, col0:col0 + M_WIDTH]
        y = convb_ref[:, col0:col0 + M_WIDTH]
        for j in range(CONV_W):
            r0 = SUBLANES - (CONV_W - 1) + j
            y = y + xs[r0:r0 + L, :] * w[j:j + 1, :]
        return y * _sigmoid(y)

    q_all = conv(xq_s, 0) * (M_DH ** -0.5)
    k_all = conv(xk_s, M_WIDTH)
    v_all = vp_s[...]
    tail = tp_s[...] + bl_ref[...]
    grow = grow_ref[0] + bs_ref[...]

    xq_s[0:SUBLANES, :] = xq_s[L:L + SUBLANES, :]
    xk_s[0:SUBLANES, :] = xk_s[L:L + SUBLANES, :]

    tt = lax.broadcasted_iota(I32, (L, L), 0)
    ss = lax.broadcasted_iota(I32, (L, L), 1)
    causal = ss <= tt
    row_ok = lax.broadcasted_iota(I32, (L, 1), 0) < valid
    col_ok = lax.broadcasted_iota(I32, (1, L), 1) < valid

    for h in range(M_HEADS):
        hs = slice(h * M_DH, (h + 1) * M_DH)
        qh = q_all[:, hs]
        kh = k_all[:, hs]
        vh = v_all[:, hs]
        ig_c = jnp.where(row_ok, tail[:, T_IM + h:T_IM + h + 1], NEG)
        lf_c = jnp.where(row_ok, _log_sigmoid(tail[:, T_FM + h:T_FM + h + 1]), 0.0)
        ig_r = jnp.where(col_ok, grow[h:h + 1, :], NEG)
        lf_r = jnp.where(col_ok, _log_sigmoid(grow[M_HEADS + h:M_HEADS + h + 1, :]), 0.0)
        b_c = jnp.sum(jnp.where(causal, lf_r, 0.0), axis=1, keepdims=True)
        b_r = jnp.sum(jnp.where(tt <= ss, lf_c, 0.0), axis=0, keepdims=True)
        dmat = jnp.where(causal, b_c - b_r + ig_r, NEG)
        m_prev = m_s[h][:, 0:1]
        m_t = jnp.maximum(b_c + m_prev, jnp.max(dmat, axis=1, keepdims=True))
        e = jnp.exp(dmat - m_t)
        qb = qh.astype(BF16)
        kb = kh.astype(BF16)
        s = lax.dot_general(qb, kb, (((1,), (1,)), ((), ())), preferred_element_type=F32) * e
        inter = jnp.exp(b_c + m_prev - m_t)
        ch = c_s[h]
        num = jnp.dot(s.astype(BF16), vh.astype(BF16), preferred_element_type=F32) + inter * lax.dot_general(
            qb, ch.astype(BF16), (((1,), (1,)), ((), ())), preferred_element_type=F32)
        nh = n_s[h]
        den = jnp.sum(s, axis=1, keepdims=True) + inter * jnp.sum(qh * nh, axis=1, keepdims=True)
        hh = num / jnp.maximum(jnp.abs(den), jnp.exp(-m_t))
        hh = hh * lax.rsqrt(jnp.mean(hh * hh, axis=1, keepdims=True) + RMS_EPS)
        out = _sigmoid(op_s[:, hs]) * (hh * nw_ref[:, hs])
        h_ref[0, :, hs] = out[0:RIN, :].astype(h_ref.dtype)
        m_new = m_t[L - 1:L, :]
        b_last = b_c[L - 1:L, :]
        w_c = jnp.exp(b_last - b_c + ig_c - m_new)
        decay = jnp.exp(b_last + m_prev - m_new)
        upd = lax.dot_general((w_c * vh).astype(BF16), kb, (((0,), (0,)), ((), ())), preferred_element_type=F32)
        c_s[h] = decay * ch + upd
        n_s[h] = decay * nh + jnp.sum(w_c * kh, axis=0, keepdims=True)
        m_s[h] = jnp.broadcast_to(m_new, (1, LANES))

    @pl.when(c == nc - 1)
    def _fin():
        cout_ref[0] = c_s[...]
        nout_ref[0] = n_s[...]
        mout_ref[0] = m_s[...]


def _mlstm(p32, grow, convbuf, conv_w, conv_b, b_gates, m_norm_w, c0, n0, m0, *, T, L, RIN, valid):
    B = p32.shape[0]
    nc = T // RIN
    bl = jnp.zeros((1, LANES), F32).at[0, T_IM:T_IM + 2 * M_HEADS].set(b_gates)
    bs = jnp.broadcast_to(b_gates[:, None], (2 * M_HEADS, L))
    kern = functools.partial(_mlstm_kernel, L=L, RIN=RIN, valid=valid)
    cblk = lambda col: pl.BlockSpec((1, RIN, M_WIDTH), lambda b, c, col=col: (b, c, col // M_WIDTH))
    const2 = lambda shape: pl.BlockSpec(shape, lambda b, c: (0, 0))
    per_b = lambda shape: pl.BlockSpec(shape, lambda b, c: (b,) + (0,) * (len(shape) - 1))
    return pl.pallas_call(
        kern,
        grid=(B, nc),
        in_specs=[cblk(C_QM), cblk(C_KM), cblk(C_VM), cblk(C_OM),
                  pl.BlockSpec((1, RIN, LANES), lambda b, c: (b, c, C_TAIL // LANES)),
                  pl.BlockSpec((1, 2 * M_HEADS, L), lambda b, c: (b, 0, c)),
                  per_b((1, SUBLANES, 2 * M_WIDTH)),
                  const2((CONV_W, 2 * M_WIDTH)), const2((1, 2 * M_WIDTH)), const2((1, LANES)),
                  const2((2 * M_HEADS, L)), const2((1, M_WIDTH)),
                  per_b((1, M_HEADS, M_DH, M_DH)), per_b((1, M_HEADS, 1, M_DH)), per_b((1, M_HEADS, 1, LANES))],
        out_specs=[pl.BlockSpec((1, RIN, M_WIDTH), lambda b, c: (b, c, 0)),
                   per_b((1, M_HEADS, M_DH, M_DH)), per_b((1, M_HEADS, 1, M_DH)), per_b((1, M_HEADS, 1, LANES))],
        out_shape=[jax.ShapeDtypeStruct((B, T, M_WIDTH), BF16),
                   jax.ShapeDtypeStruct((B, M_HEADS, M_DH, M_DH), F32),
                   jax.ShapeDtypeStruct((B, M_HEADS, 1, M_DH), F32),
                   jax.ShapeDtypeStruct((B, M_HEADS, 1, LANES), F32)],
        scratch_shapes=[pltpu.VMEM((SUBLANES + L, M_WIDTH), F32), pltpu.VMEM((SUBLANES + L, M_WIDTH), F32),
                        pltpu.VMEM((M_HEADS, M_DH, M_DH), F32), pltpu.VMEM((M_HEADS, 1, M_DH), F32),
                        pltpu.VMEM((M_HEADS, 1, LANES), F32),
                        pltpu.VMEM((L, M_WIDTH), F32), pltpu.VMEM((L, M_WIDTH), F32), pltpu.VMEM((L, LANES), F32)],
        compiler_params=_cparams(("parallel", "arbitrary")),
        name="mlstm",
    )(p32, p32, p32, p32, p32, grow, convbuf, conv_w, conv_b.reshape(1, -1), bl, bs, m_norm_w.reshape(1, -1),
      c0, n0.reshape(B, M_HEADS, 1, M_DH), jnp.broadcast_to(m0[:, :, None, None], (B, M_HEADS, 1, LANES)))


def _score_key(sc):
    bits = lax.bitcast_convert_type(sc, I32)
    return jnp.where(bits < 0, INT_MIN - bits, bits)


DSA_TQ = 512
DSA_TK = 512
DSA_RG = 64
DSA_RB = 512
LOG2E = 1.4426950408889634


def _dsa_prompt_kernel(qi_tab, kj_tab, qidx_ref, tail_ref, kit_ref, qa_ref, ka_ref, va_ref, o_ref,
                       keys_s, cand_s, cnt_s, thr_s, thrm_s, cut_s, bias_s, tie_s, wrep_s, m_s, l_s, acc_s,
                       *, TQ, TK, topk, pos_bits):
    step = pl.program_id(0)
    qi = qi_tab[step]
    kj = kj_tab[step]
    RG = DSA_RG
    nlc = TK // LANES

    def count_pass(pred):
        def rbody(r, _):
            r0 = pl.multiple_of(r * RG, RG)
            cand = cand_s[pl.ds(r0, RG), :]
            aux = thr_s[pl.ds(r0, RG), :]

            def kb(j, cnt):
                for c in range(nlc):
                    blk = keys_s[j, pl.ds(r0, RG), c * LANES:(c + 1) * LANES]
                    pos = j * TK + c * LANES + lax.broadcasted_iota(I32, (RG, LANES), 1)
                    cnt = jnp.where(pred(blk, cand, aux, pos), cnt + 1.0, cnt)
                return cnt

            cnt_s[pl.ds(r0, RG), :] = lax.fori_loop(0, qi + 1, kb, jnp.zeros((RG, LANES), F32))
            return 0

        lax.fori_loop(0, TQ // RG, rbody, 0)
        return jnp.sum(cnt_s[...], axis=1, keepdims=True)

    @pl.when(kj == 0)
    def _phase1():
        w = tail_ref[:, T_WI:T_WI + IDX_HEADS] * (IDX_HEADS ** -0.5) * (IDX_DIM ** -0.5)
        for h in range(IDX_HEADS):
            wrep_s[h] = jnp.broadcast_to(w[:, h:h + 1], (TQ, LANES))
        rowpos = qi * TQ + lax.broadcasted_iota(I32, (TQ, TK), 0)

        def kbody(j, _):
            kt = kit_ref[j].astype(BF16)
            sc = jnp.zeros((TQ, TK), F32)
            for h in range(IDX_HEADS):
                qh = qidx_ref[:, h * LANES:(h + 1) * LANES].astype(BF16)
                s = jnp.dot(qh, kt, preferred_element_type=F32)
                sc = sc + jnp.maximum(s, 0.0) * jnp.concatenate([wrep_s[h]] * nlc, axis=1)
            colpos = j * TK + lax.broadcasted_iota(I32, (TQ, TK), 1)
            keys_s[j] = jnp.where(colpos <= rowpos, _score_key(sc), INT_MIN)
            return 0

        lax.fori_loop(0, qi + 1, kbody, 0)

        ge = lambda blk, cand, aux, pos: blk >= cand
        cand_s[...] = jnp.zeros((TQ, LANES), I32)
        cnt = count_pass(ge)
        lo = jnp.where(cnt >= topk, 0, INT_MIN).astype(I32)

        def bit_body(b, carry):
            lo, n_lo = carry
            cand = lo | lax.shift_left(jnp.int32(1), 30 - b)
            cand_s[...] = jnp.broadcast_to(cand, (TQ, LANES))
            cnt = count_pass(ge)
            take = cnt >= topk
            return jnp.where(take, cand, lo), jnp.where(take, cnt, n_lo)

        thr, n_ge = lax.fori_loop(0, 31, bit_body, (lo, cnt))
        thr_s[...] = jnp.broadcast_to(thr, (TQ, LANES))
        short = thr == INT_MIN
        thrm_s[...] = jnp.broadcast_to(jnp.where(short, INT_MIN, thr - 1), (TQ, LANES))
        cut_s[...] = jnp.broadcast_to(jnp.where(short, -1, 2 ** 30).astype(I32), (TQ, LANES))
        surplus = jnp.max(jnp.where((n_ge > topk) & jnp.logical_not(short), 1.0, 0.0))
        tie_s[0] = (surplus > 0.0).astype(I32)

        @pl.when(surplus > 0.0)
        def _ties():
            cand_s[...] = thr_s[...]
            need = topk - count_pass(lambda blk, cand, aux, pos: blk > cand)
            eq_below = lambda blk, cand, aux, pos: (blk == aux) & (pos < cand)

            def tie_body(b, x):
                cand = x + lax.shift_left(jnp.int32(1), pos_bits - 1 - b)
                cand_s[...] = jnp.broadcast_to(cand, (TQ, LANES))
                cnt = count_pass(eq_below)
                return jnp.where(cnt < need, cand, x)

            x = lax.fori_loop(0, pos_bits, tie_body, jnp.zeros((TQ, 1), I32))
            cut_s[...] = jnp.broadcast_to(jnp.where(short, -1, x), (TQ, LANES))

        m_s[...] = jnp.full(m_s.shape, NEG, F32)
        l_s[...] = jnp.zeros(l_s.shape, F32)
        acc_s[...] = jnp.zeros(acc_s.shape, F32)

    tile_l = lambda a: jnp.concatenate([a] * nlc, axis=1)

    @pl.when(tie_s[0] == 0)
    def _bias_plain():
        bias_s[...] = jnp.where(keys_s[kj] > tile_l(thrm_s[...]), 0.0, NEG)

    @pl.when(tie_s[0] != 0)
    def _bias_tied():
        key = keys_s[kj]
        thr = tile_l(thr_s[...])
        colpos = kj * TK + lax.broadcasted_iota(I32, (TQ, TK), 1)
        sel = (key > thr) | ((key == thr) & (colpos <= tile_l(cut_s[...])))
        bias_s[...] = jnp.where(sel, 0.0, NEG)

    RB = DSA_RB
    c1 = (A_DH ** -0.5) * LOG2E

    def rb_body(rb, _):
        rows = pl.ds(pl.multiple_of(rb * RB, RB), RB)
        bias = bias_s[rows, :]
        hsl = [slice(h * A_DH, (h + 1) * A_DH) for h in range(A_HEADS)]
        m_old = [m_s[h, rows, :] for h in range(A_HEADS)]
        l_old = [l_s[h, rows, :] for h in range(A_HEADS)]
        a_old = [acc_s[rows, hs] for hs in hsl]
        ts = [lax.dot_general(qa_ref[rows, hs], ka_ref[:, hs], (((1,), (1,)), ((), ())), preferred_element_type=F32) * c1 + bias
              for hs in hsl]
        m_new = [jnp.maximum(m_old[h], jnp.max(ts[h], axis=1, keepdims=True)) for h in range(A_HEADS)]
        l_new, a_new = [], []
        for h in range(A_HEADS):
            alpha = jnp.exp2(m_old[h] - m_new[h])
            p = jnp.exp2(ts[h] - tile_l(m_new[h]))
            psum = p[:, 0:LANES]
            for c in range(1, nlc):
                psum = psum + p[:, c * LANES:(c + 1) * LANES]
            l_new.append(alpha * l_old[h] + psum)
            a_new.append(alpha * a_old[h] + jnp.dot(p.astype(BF16), va_ref[:, hsl[h]], preferred_element_type=F32))
        for h in range(A_HEADS):
            m_s[h, rows, :] = m_new[h]
            l_s[h, rows, :] = l_new[h]
            acc_s[rows, hsl[h]] = a_new[h]
        return 0

    lax.fori_loop(0, TQ // RB, rb_body, 0)

    @pl.when(kj == qi)
    def _fin():
        for h in range(A_HEADS):
            hs = slice(h * A_DH, (h + 1) * A_DH)
            o_ref[:, hs] = (acc_s[:, hs] / jnp.sum(l_s[h], axis=1, keepdims=True)).astype(o_ref.dtype)


def _dsa_prompt(p32, p16, T, topk):
    TQ, TK = DSA_TQ, DSA_TK
    nq = T // TQ
    assert TQ == TK
    qi_tab = np.concatenate([np.full(i + 1, i) for i in range(nq)]).astype(np.int32)
    kj_tab = np.concatenate([np.arange(i + 1) for i in range(nq)]).astype(np.int32)
    ki = p32[:T, C_TAIL + T_KI:C_TAIL + T_KI + IDX_DIM]
    kit = jnp.pad(ki.T, ((0, LANES - IDX_DIM), (0, 0))).reshape(LANES, T // TK, TK).transpose(1, 0, 2)
    kern = functools.partial(_dsa_prompt_kernel, TQ=TQ, TK=TK, topk=topk, pos_bits=int(T - 1).bit_length())
    gs = pltpu.PrefetchScalarGridSpec(
        num_scalar_prefetch=2,
        grid=(len(qi_tab),),
        in_specs=[pl.BlockSpec((TQ, IDX_HEADS * LANES), lambda s, qt, kt: (qt[s], C_QI // (IDX_HEADS * LANES))),
                  pl.BlockSpec((TQ, LANES), lambda s, qt, kt: (qt[s], C_TAIL // LANES)),
                  pl.BlockSpec((T // TK, LANES, TK), lambda s, qt, kt: (0, 0, 0)),
                  pl.BlockSpec((TQ, A_WIDTH), lambda s, qt, kt: (qt[s], C_QA // A_WIDTH)),
                  pl.BlockSpec((TK, A_WIDTH), lambda s, qt, kt: (kt[s], C_KA // A_WIDTH)),
                  pl.BlockSpec((TK, A_WIDTH), lambda s, qt, kt: (kt[s], C_VA // A_WIDTH))],
        out_specs=pl.BlockSpec((TQ, A_WIDTH), lambda s, qt, kt: (qt[s], 0)),
        scratch_shapes=[pltpu.VMEM((T // TK, TQ, TK), I32),
                        pltpu.VMEM((TQ, LANES), I32), pltpu.VMEM((TQ, LANES), F32), pltpu.VMEM((TQ, LANES), I32),
                        pltpu.VMEM((TQ, LANES), I32), pltpu.VMEM((TQ, LANES), I32),
                        pltpu.VMEM((TQ, TK), F32), pltpu.SMEM((1,), I32), pltpu.VMEM((IDX_HEADS, TQ, LANES), F32),
                        pltpu.VMEM((A_HEADS, TQ, LANES), F32), pltpu.VMEM((A_HEADS, TQ, LANES), F32),
                        pltpu.VMEM((TQ, A_WIDTH), F32)])
    return pl.pallas_call(
        kern, grid_spec=gs,
        out_shape=jax.ShapeDtypeStruct((T, A_WIDTH), BF16),
        compiler_params=_cparams(("arbitrary",)),
        name="dsa_prompt",
    )(jnp.asarray(qi_tab), jnp.asarray(kj_tab), p32, p32, kit, p16, p16, p16)


SMP_PGS = 32
SMP_PG = 8
SMP_ROWS = SUBLANES
SMP_SEL_B = 16


def _idx_scores_t(q, w, kt):
    s = jnp.dot(q.astype(BF16), kt.astype(BF16), preferred_element_type=F32)
    s = jnp.maximum(s * (IDX_DIM ** -0.5), 0.0) * (w * (IDX_HEADS ** -0.5))
    n_tok = q.shape[0] // IDX_HEADS
    rows = [jnp.sum(s[t * IDX_HEADS:(t + 1) * IDX_HEADS], axis=0, keepdims=True) for t in range(n_tok)]
    rows.append(jnp.zeros((SMP_ROWS - n_tok, s.shape[1]), F32))
    return jnp.concatenate(rows, axis=0)


def _smp_scores2_kernel(pt_ref, q_ref, w_ref, *refs):
    o_ref = refs[-1]
    kt = jnp.concatenate([r[0] for r in refs[:-1]], axis=1)
    o_ref[0] = _idx_scores_t(q_ref[0], w_ref[0], kt)


def _smp_scores2(page_table, qs, ws, kidx_t):
    B, n_pages = page_table.shape
    R = qs.shape[1]
    page = lambda i: pl.BlockSpec((1, IDX_DIM, PAGE_SIZE), lambda b, p, pt: (pt[b, p * SMP_PGS + i], 0, 0))
    gs = pltpu.PrefetchScalarGridSpec(
        num_scalar_prefetch=1,
        grid=(B, n_pages // SMP_PGS),
        in_specs=[pl.BlockSpec((1, R, IDX_DIM), lambda b, p, pt: (b, 0, 0)),
                  pl.BlockSpec((1, R, 1), lambda b, p, pt: (b, 0, 0))] + [page(i) for i in range(SMP_PGS)],
        out_specs=pl.BlockSpec((1, SMP_ROWS, SMP_PGS * PAGE_SIZE), lambda b, p, pt: (b, 0, p)))
    return pl.pallas_call(
        _smp_scores2_kernel, grid_spec=gs,
        out_shape=jax.ShapeDtypeStruct((B, SMP_ROWS, n_pages * PAGE_SIZE), F32),
        compiler_params=_cparams(("parallel", "arbitrary")),
        name="smp_scores",
    )(page_table, qs, ws, *([kidx_t] * SMP_PGS))


def _smp_select2_kernel(sc_ref, q_ref, w_ref, kint_ref, mp_ref, mn_ref, keys_s, cand_s, thr_s, cut_s, cnt_s,
                        *, n_tok, topk):
    NB = q_ref.shape[0]
    R = NB * SMP_ROWS
    P = sc_ref.shape[1]
    NCH = P // LANES
    RG = DSA_RG
    lane = lax.broadcasted_iota(I32, (R, LANES), 1)
    trow = lax.broadcasted_iota(I32, (R, LANES), 0) % SMP_ROWS
    for c in range(NCH):
        keys_s[c] = _score_key(sc_ref[:, c * LANES:(c + 1) * LANES])
    s_new = jnp.concatenate([_idx_scores_t(q_ref[b], w_ref[b], kint_ref[b]) for b in range(NB)], axis=0)
    keys_s[NCH] = jnp.where(lane <= trow, _score_key(s_new), INT_MIN)

    def count_pass(pred):
        def rbody(r, _):
            rows = pl.ds(pl.multiple_of(r * RG, RG), RG)
            cand = cand_s[rows, :]
            aux = thr_s[rows, :]

            def cb(c, cnt):
                pos = c * LANES + lax.broadcasted_iota(I32, (RG, LANES), 1)
                return cnt + jnp.where(pred(keys_s[c, rows, :], cand, aux, pos), 1.0, 0.0)

            cnt_s[rows, :] = lax.fori_loop(0, NCH + 1, cb, jnp.zeros((RG, LANES), F32))
            return 0

        lax.fori_loop(0, R // RG, rbody, 0)
        return jnp.sum(cnt_s[...], axis=1, keepdims=True)

    ge = lambda blk, cand, aux, pos: blk >= cand
    cand_s[...] = jnp.zeros((R, LANES), I32)
    cnt0 = count_pass(ge)
    lo = jnp.where(cnt0 >= topk, 0, INT_MIN).astype(I32)

    def bit_body(b, carry):
        lo, n_lo = carry
        cand = lo | lax.shift_left(jnp.int32(1), 30 - b)
        cand_s[...] = jnp.broadcast_to(cand, (R, LANES))
        cnt = count_pass(ge)
        take = cnt >= topk
        return jnp.where(take, cand, lo), jnp.where(take, cnt, n_lo)

    thr, n_ge = lax.fori_loop(0, 31, bit_body, (lo, cnt0))
    thr_s[...] = jnp.broadcast_to(thr, (R, LANES))
    cut_s[...] = jnp.full((R, LANES), 2 ** 30, I32)
    real_row = lax.broadcasted_iota(I32, (R, 1), 0) % SMP_ROWS < n_tok
    surplus = jnp.max(jnp.where((n_ge > topk) & (thr != INT_MIN) & real_row, 1.0, 0.0))

    @pl.when(surplus > 0.0)
    def _ties():
        cand_s[...] = thr_s[...]
        need = topk - count_pass(lambda blk, cand, aux, pos: blk > cand)
        pos_bits = int(P + LANES - 1).bit_length()
        eq_below = lambda blk, cand, aux, pos: (blk == aux) & (pos < cand)

        def tie_body(b, x):
            cand = x + lax.shift_left(jnp.int32(1), pos_bits - 1 - b)
            cand_s[...] = jnp.broadcast_to(cand, (R, LANES))
            return jnp.where(count_pass(eq_below) < need, cand, x)

        cut_s[...] = jnp.broadcast_to(lax.fori_loop(0, pos_bits, tie_body, jnp.zeros((R, 1), I32)), (R, LANES))

    cut = cut_s[...]
    thr_b = thr_s[...]
    row_ok = trow < n_tok
    for c in range(NCH + 1):
        key = keys_s[c]
        sel = ((key > thr_b) | ((key == thr_b) & (c * LANES + lane <= cut))) & (key != INT_MIN) & row_ok
        if c < NCH:
            mp_ref[:, c * LANES:(c + 1) * LANES] = jnp.where(sel, 1.0, 0.0)
        else:
            mn_ref[...] = jnp.where(sel, 1.0, 0.0)


def _smp_select2(sc, qs, ws, kin_t, n_tok, topk):
    R, P = sc.shape
    B = qs.shape[0]
    NB = SMP_SEL_B
    RS = NB * SMP_ROWS
    Rq = qs.shape[1]
    return pl.pallas_call(
        functools.partial(_smp_select2_kernel, n_tok=n_tok, topk=topk),
        grid=(B // NB,),
        in_specs=[pl.BlockSpec((RS, P), lambda i: (i, 0)),
                  pl.BlockSpec((NB, Rq, IDX_DIM), lambda i: (i, 0, 0)), pl.BlockSpec((NB, Rq, 1), lambda i: (i, 0, 0)),
                  pl.BlockSpec((NB, IDX_DIM, LANES), lambda i: (i, 0, 0))],
        out_specs=[pl.BlockSpec((RS, P), lambda i: (i, 0)), pl.BlockSpec((RS, LANES), lambda i: (i, 0))],
        out_shape=[jax.ShapeDtypeStruct((R, P), F32), jax.ShapeDtypeStruct((R, LANES), F32)],
        scratch_shapes=[pltpu.VMEM((P // LANES + 1, RS, LANES), I32), pltpu.VMEM((RS, LANES), I32),
                        pltpu.VMEM((RS, LANES), I32), pltpu.VMEM((RS, LANES), I32), pltpu.VMEM((RS, LANES), F32)],
        compiler_params=_cparams(("parallel",)),
        name="smp_select",
    )(sc, qs, ws, kin_t)


def _smp_attn2_kernel(pt_ref, q_ref, kn_ref, vn_ref, mn_ref, mp_ref, *refs, n_tok):
    k_refs = refs[:SMP_PG]
    v_refs = refs[SMP_PG:2 * SMP_PG]
    o_ref, kpad_s, vpad_s, m_s, l_s, acc_s = refs[2 * SMP_PG:]
    p = pl.program_id(1)
    NL = PAGE_SIZE * A_HEADS
    R = n_tok * A_HEADS
    c1 = (A_DH ** -0.5) * LOG2E
    diag = jnp.where(lax.broadcasted_iota(I32, (A_HEADS, NL), 1) % A_HEADS == lax.broadcasted_iota(I32, (A_HEADS, NL), 0), 1.0, 0.0)
    expand = jnp.where(lax.broadcasted_iota(I32, (PAGE_SIZE, NL), 1) // A_HEADS == lax.broadcasted_iota(I32, (PAGE_SIZE, NL), 0),
                       1.0, 0.0).astype(BF16)
    qb = q_ref[0].astype(BF16)

    def attend(k_list, v_list, masks):
        n = len(k_list)
        x = jnp.dot(jnp.concatenate(masks, axis=0).astype(BF16), expand, preferred_element_type=F32)
        ts = []
        for i in range(n):
            s = lax.dot_general(qb, k_list[i], (((1,), (1,)), ((), ())), preferred_element_type=F32)
            ok = jnp.concatenate([jnp.broadcast_to(x[i * SMP_ROWS + t:i * SMP_ROWS + t + 1, :], (A_HEADS, NL)) * diag
                                  for t in range(n_tok)], axis=0)
            ts.append(s * c1 + jnp.where(ok > 0.5, 0.0, NEG))
        m_old = m_s[...]
        m_new = jnp.maximum(m_old, jnp.max(jnp.concatenate(ts, axis=1), axis=1, keepdims=True))
        alpha = jnp.exp2(m_old - m_new)
        acc = alpha * acc_s[...]
        lsum = alpha * l_s[...]
        m_t = jnp.concatenate([m_new] * (NL // LANES), axis=1)
        for i in range(n):
            pr = jnp.exp2(ts[i] - m_t)
            for c in range(NL // LANES):
                lsum = lsum + pr[:, c * LANES:(c + 1) * LANES]
            acc = acc + jnp.dot(pr.astype(BF16), v_list[i], preferred_element_type=F32)
        m_s[...] = m_new
        l_s[...] = lsum
        acc_s[...] = acc

    @pl.when(p == 0)
    def _first():
        m_s[...] = jnp.full(m_s.shape, NEG, F32)
        l_s[...] = jnp.zeros(l_s.shape, F32)
        acc_s[...] = jnp.zeros(acc_s.shape, F32)
        kpad_s[...] = jnp.zeros(kpad_s.shape, F32)
        vpad_s[...] = jnp.zeros(vpad_s.shape, F32)
        kpad_s[0:R, :] = kn_ref[0]
        vpad_s[0:R, :] = vn_ref[0]
        attend([kpad_s[...].astype(BF16)], [vpad_s[...].astype(BF16)], [mn_ref[0]])

    attend([r[0].reshape(NL, A_DH).astype(BF16) for r in k_refs], [r[0].reshape(NL, A_DH).astype(BF16) for r in v_refs],
           [mp_ref[0][:, i * PAGE_SIZE:(i + 1) * PAGE_SIZE] for i in range(SMP_PG)])

    @pl.when(p == pl.num_programs(1) - 1)
    def _fin():
        o_ref[0] = acc_s[...] / jnp.sum(l_s[...], axis=1, keepdims=True)


def _smp_attn2(page_table, q32, kn32, vn32, mn, mp, ck, cv, n_tok):
    B, n_pages = page_table.shape
    R = n_tok * A_HEADS
    NL = PAGE_SIZE * A_HEADS
    per_b = lambda shape: pl.BlockSpec(shape, lambda b, p, pt: (b, 0, 0))
    page = lambda i: pl.BlockSpec((1, PAGE_SIZE, A_HEADS, A_DH), lambda b, p, pt: (pt[b, p * SMP_PG + i], 0, 0, 0))
    gs = pltpu.PrefetchScalarGridSpec(
        num_scalar_prefetch=1,
        grid=(B, n_pages // SMP_PG),
        in_specs=[per_b((1, R, A_DH)), per_b((1, R, A_DH)), per_b((1, R, A_DH)), per_b((1, SMP_ROWS, LANES)),
                  pl.BlockSpec((1, SMP_ROWS, SMP_PG * PAGE_SIZE), lambda b, p, pt: (b, 0, p))]
                 + [page(i) for i in range(SMP_PG)] + [page(i) for i in range(SMP_PG)],
        out_specs=per_b((1, R, A_DH)),
        scratch_shapes=[pltpu.VMEM((NL, A_DH), F32), pltpu.VMEM((NL, A_DH), F32),
                        pltpu.VMEM((R, LANES), F32), pltpu.VMEM((R, LANES), F32), pltpu.VMEM((R, A_DH), F32)])
    return pl.pallas_call(
        functools.partial(_smp_attn2_kernel, n_tok=n_tok), grid_spec=gs,
        out_shape=jax.ShapeDtypeStruct((B, R, A_DH), F32),
        compiler_params=_cparams(("parallel", "arbitrary")),
        name="smp_attn",
    )(page_table, q32, kn32, vn32, mn, mp, *([ck] * SMP_PG), *([cv] * SMP_PG))


def _merge_kernel(hm_ref, ha_ref, wm_ref, wa_ref, gm_ref, ga_ref, o_ref):
    a = jnp.dot(hm_ref[...], wm_ref[...], preferred_element_type=F32)
    b = jnp.dot(ha_ref[...], wa_ref[...], preferred_element_type=F32)
    o_ref[...] = (_sigmoid(gm_ref[...]) * a + _sigmoid(ga_ref[...]) * b).astype(o_ref.dtype)


def _merge(hm, ha, wm, wa, p32, tm):
    m = hm.shape[0]
    tn = PROJ_TN
    return pl.pallas_call(
        _merge_kernel,
        grid=(D_MODEL // tn, m // tm),
        in_specs=[pl.BlockSpec((tm, M_WIDTH), lambda j, i: (i, 0)), pl.BlockSpec((tm, A_WIDTH), lambda j, i: (i, 0)),
                  pl.BlockSpec((M_WIDTH, tn), lambda j, i: (0, j)), pl.BlockSpec((A_WIDTH, tn), lambda j, i: (0, j)),
                  pl.BlockSpec((tm, tn), lambda j, i: (i, C_GM // tn + j)),
                  pl.BlockSpec((tm, tn), lambda j, i: (i, C_GA // tn + j))],
        out_specs=pl.BlockSpec((tm, tn), lambda j, i: (i, j)),
        out_shape=jax.ShapeDtypeStruct((m, D_MODEL), BF16),
        compiler_params=_cparams(("parallel", "parallel")),
        name="merge",
    )(hm, ha, wm, wa, p32, p32)


def _outproj_kernel(mg_ref, w_ref, x_ref, o_ref):
    o_ref[...] = x_ref[...] + jnp.dot(mg_ref[...], w_ref[...], preferred_element_type=F32)


def _outproj(mg, w, x, tm):
    m = mg.shape[0]
    tn = PROJ_TN
    return pl.pallas_call(
        _outproj_kernel,
        grid=(D_MODEL // tn, m // tm),
        in_specs=[pl.BlockSpec((tm, D_MODEL), lambda j, i: (i, 0)), pl.BlockSpec((D_MODEL, tn), lambda j, i: (0, j)),
                  pl.BlockSpec((tm, tn), lambda j, i: (i, j))],
        out_specs=pl.BlockSpec((tm, tn), lambda j, i: (i, j)),
        out_shape=jax.ShapeDtypeStruct((m, D_MODEL), F32),
        compiler_params=_cparams(("parallel", "parallel")),
        name="outproj",
    )(mg, w, x)


MOE_TM = 256
MOE_NBUF = 3
MOE_RG = 8


def _router_kernel(x_ref, g_ref, wr_ref, br_ref, xn_ref, r_ref):
    x = x_ref[...]
    y = (x * lax.rsqrt(jnp.mean(x * x, axis=-1, keepdims=True) + RMS_EPS)) * g_ref[...]
    xn_ref[...] = y
    lg = jnp.dot(y.astype(BF16), wr_ref[...], preferred_element_type=F32) + br_ref[...]
    lane = lax.broadcasted_iota(I32, lg.shape, 1).astype(F32)
    far = float(LANES)
    gmask = lane < N_GROUPS
    gl = jnp.where(gmask, lg, NEG)
    mg = jnp.max(gl, axis=1, keepdims=True)
    p_g = 1.0 / jnp.sum(jnp.where(gmask, jnp.exp(gl - mg), 0.0), axis=1, keepdims=True)
    g_sel = jnp.min(jnp.where(gmask & (gl == mg), lane, far), axis=1, keepdims=True)
    e_lo = N_GROUPS + g_sel * EXP_PER_GROUP
    emask = (lane >= e_lo) & (lane < e_lo + EXP_PER_GROUP)
    el = jnp.where(emask, lg, NEG)
    me = jnp.max(el, axis=1, keepdims=True)
    pe = jnp.where(emask, jnp.exp(el - me), 0.0)
    probs = pe / jnp.sum(pe, axis=1, keepdims=True)
    p1 = jnp.max(probs, axis=1, keepdims=True)
    i1 = jnp.min(jnp.where(emask & (probs == p1), lane, far), axis=1, keepdims=True)
    probs2 = jnp.where(lane == i1, -1.0, probs)
    p2 = jnp.max(probs2, axis=1, keepdims=True)
    i2 = jnp.min(jnp.where(emask & (probs2 == p2), lane, far), axis=1, keepdims=True)
    tot = p1 + p2
    vals = [i1 - N_GROUPS, i2 - N_GROUPS, p_g * (p1 / tot), p_g * (p2 / tot)]
    out = jnp.zeros(lg.shape, F32)
    for c, v in enumerate(vals):
        out = jnp.where(lane == c, v, out)
    r_ref[...] = out


def _router(x1, g, wr, br, tm):
    m, d = x1.shape
    return pl.pallas_call(
        _router_kernel,
        grid=(m // tm,),
        in_specs=[pl.BlockSpec((tm, d), lambda i: (i, 0)), pl.BlockSpec((1, d), lambda i: (0, 0)),
                  pl.BlockSpec((d, LANES), lambda i: (0, 0)), pl.BlockSpec((1, LANES), lambda i: (0, 0))],
        out_specs=[pl.BlockSpec((tm, d), lambda i: (i, 0)), pl.BlockSpec((tm, LANES), lambda i: (i, 0))],
        out_shape=[jax.ShapeDtypeStruct((m, d), F32), jax.ShapeDtypeStruct((m, LANES), F32)],
        compiler_params=_cparams(("parallel",)),
        name="router",
    )(x1, g.reshape(1, d), wr, br)


def _row_copy(src_hbm, row, dst, r, sem):
    return pltpu.make_async_copy(src_hbm.at[pl.ds(row, 1), :], dst.at[pl.ds(r, 1), :], sem)


def _expert_kernel(be_ref, na_ref, ng_ref, src_ref, x_hbm, wg_ref, wu_ref, wd_ref, o_ref, xbuf, sem):
    blk = pl.program_id(0)
    slot = blk % MOE_NBUF
    ahead = MOE_NBUF - 1

    def gather(b):
        s = b % MOE_NBUF

        def start(g, _):
            for i in range(MOE_RG):
                r = g * MOE_RG + i
                _row_copy(x_hbm, src_ref[b * MOE_TM + r], xbuf.at[s], r, sem.at[s]).start()
            return 0

        lax.fori_loop(0, ng_ref[b], start, 0)

    @pl.when(blk == 0)
    def _prime():
        xbuf[...] = jnp.zeros(xbuf.shape, F32)
        for b in range(ahead):
            @pl.when(b < na_ref[0])
            def _():
                gather(b)

    @pl.when(blk + ahead < na_ref[0])
    def _prefetch():
        gather(blk + ahead)

    @pl.when(blk < na_ref[0])
    def _active():
        def wait(g, _):
            for i in range(MOE_RG):
                _row_copy(x_hbm, 0, xbuf.at[slot], g * MOE_RG + i, sem.at[slot]).wait()
            return 0

        lax.fori_loop(0, ng_ref[blk], wait, 0)
        x = xbuf[slot].astype(BF16)
        hg = jnp.dot(x, wg_ref[0].astype(BF16), preferred_element_type=F32)
        hu = jnp.dot(x, wu_ref[0].astype(BF16), preferred_element_type=F32)
        h = (hg * _sigmoid(hg)) * hu
        o_ref[...] = jnp.dot(h.astype(BF16), wd_ref[0].astype(BF16), preferred_element_type=F32)

    @pl.when(blk >= na_ref[0])
    def _idle():
        o_ref[...] = jnp.zeros(o_ref.shape, F32)


def _experts(blk_exp, n_act, blk_groups, src, xn2, w_gate, w_up, w_down):
    npad = src.shape[0]
    d = xn2.shape[1]
    gs = pltpu.PrefetchScalarGridSpec(
        num_scalar_prefetch=4,
        grid=(npad // MOE_TM,),
        in_specs=[pl.BlockSpec(memory_space=pl.ANY),
                  pl.BlockSpec((1, d, D_EXPERT), lambda b, be, na, ng, sr: (be[b], 0, 0)),
                  pl.BlockSpec((1, d, D_EXPERT), lambda b, be, na, ng, sr: (be[b], 0, 0)),
                  pl.BlockSpec((1, D_EXPERT, d), lambda b, be, na, ng, sr: (be[b], 0, 0))],
        out_specs=pl.BlockSpec((MOE_TM, d), lambda b, be, na, ng, sr: (b, 0)),
        scratch_shapes=[pltpu.VMEM((MOE_NBUF, MOE_TM, d), F32), pltpu.SemaphoreType.DMA((MOE_NBUF,))])
    return pl.pallas_call(
        _expert_kernel, grid_spec=gs,
        out_shape=jax.ShapeDtypeStruct((npad, d), F32),
        compiler_params=_cparams(("arbitrary",)),
        name="experts",
    )(blk_exp, n_act, blk_groups, src, xn2, w_gate, w_up, w_down)


def _combine_kernel(pos_ref, ys_hbm, x1_ref, r_ref, g_ref, o_ref, buf, sem, *, TC, row0):
    i = pl.program_id(0)
    slot = i % 2

    def gather(step, s):
        base = (row0 + step * TC) * TOP_E

        def start(r, _):
            for e in range(TOP_E):
                _row_copy(ys_hbm, pos_ref[base + r * TOP_E + e], buf.at[s, e], r, sem.at[s]).start()
            return 0

        lax.fori_loop(0, TC, start, 0, unroll=8)

    @pl.when(i == 0)
    def _prime():
        gather(0, 0)

    @pl.when(i + 1 < pl.num_programs(0))
    def _prefetch():
        gather(i + 1, 1 - slot)

    def wait(r, _):
        for e in range(TOP_E):
            _row_copy(ys_hbm, 0, buf.at[slot, e], r, sem.at[slot]).wait()
        return 0

    lax.fori_loop(0, TC, wait, 0, unroll=8)
    gates = r_ref[:, TOP_E:2 * TOP_E]
    x = x1_ref[...]
    for e in range(TOP_E):
        x = x + gates[:, e:e + 1] * buf[slot, e]
    y = x * lax.rsqrt(jnp.mean(x * x, axis=-1, keepdims=True) + RMS_EPS)
    o_ref[...] = y * g_ref[...]


def _combine(pos, ys, x1, r, g, row0, n, TC):
    d = x1.shape[1]
    gs = pltpu.PrefetchScalarGridSpec(
        num_scalar_prefetch=1,
        grid=(n // TC,),
        in_specs=[pl.BlockSpec(memory_space=pl.ANY),
                  pl.BlockSpec((TC, d), lambda i, ps: (row0 // TC + i, 0)),
                  pl.BlockSpec((TC, LANES), lambda i, ps: (row0 // TC + i, 0)),
                  pl.BlockSpec((1, d), lambda i, ps: (0, 0))],
        out_specs=pl.BlockSpec((TC, d), lambda i, ps: (i, 0)),
        scratch_shapes=[pltpu.VMEM((2, TOP_E, TC, d), F32), pltpu.SemaphoreType.DMA((2,))])
    return pl.pallas_call(
        functools.partial(_combine_kernel, TC=TC, row0=row0), grid_spec=gs,
        out_shape=jax.ShapeDtypeStruct((n, d), F32),
        compiler_params=_cparams(("arbitrary",)),
        name="combine",
    )(pos, ys, x1, r, g.reshape(1, d))


def _route_tables(r, npad):
    nt = r.shape[0]
    ef = r[:, 0:TOP_E].astype(I32).reshape(-1)
    onehot = (ef[:, None] == jnp.arange(N_EXPERTS, dtype=I32)[None, :]).astype(I32)
    csum = jnp.cumsum(onehot, axis=0)
    rank = jnp.sum(onehot * csum, axis=1) - 1
    cnt = csum[-1]
    nblk = (cnt + MOE_TM - 1) // MOE_TM
    blk_end = jnp.cumsum(nblk)
    blk_start = blk_end - nblk
    pos = blk_start[ef] * MOE_TM + rank
    n_act = blk_end[-1:]
    ball = jnp.arange(npad // MOE_TM, dtype=I32)
    b = jnp.minimum(ball, n_act[0] - 1)
    blk_exp = jnp.minimum(jnp.sum((blk_end[None, :] <= b[:, None]).astype(I32), axis=1), N_EXPERTS - 1)
    rows = jnp.clip(cnt[blk_exp] - (ball - blk_start[blk_exp]) * MOE_TM, 0, MOE_TM)
    blk_groups = jnp.where(ball < n_act[0], (rows + MOE_RG - 1) // MOE_RG, 0).astype(I32)
    src = jnp.zeros((npad,), I32).at[pos].set(jnp.arange(nt * TOP_E, dtype=I32) // TOP_E)
    return blk_exp, n_act.astype(I32), blk_groups, src, pos.astype(I32)


def kernel(x_prompt, x_sample, cache_k, cache_v, cache_kidx, state_conv, state_C, state_n, state_m, page_table,
           g_attn, w_in, b_gates_m, conv_w, conv_b, m_norm_w, w_proj_m, w_proj_a, w_out, g_ffn,
           w_rg, b_rg, w_re, b_re, w_gate, w_up, w_down, g_final):
    assert x_prompt.shape[0] == 1 and g_attn.shape[0] == 1
    l = 0
    Tp = x_prompt.shape[1]
    Bs, Ts = x_sample.shape[:2]
    Ns = Bs * SMP_ROWS
    NT = Tp + Ns
    TM_BIG = 1408
    TM_ROW = 768
    assert NT % TM_BIG == 0 and NT % TM_ROW == 0 and Ts >= CONV_W - 1 and Ts <= SMP_ROWS
    P = page_table.shape[1] * PAGE_SIZE
    pad_rows = lambda a: jnp.pad(a, ((0, 0), (0, SMP_ROWS - Ts), (0, 0)))

    x_all = jnp.concatenate([x_prompt[0], pad_rows(x_sample).reshape(Ns, D_MODEL)], axis=0)
    xn = _rmsnorm(x_all, g_attn[l], BF16, TM_ROW)
    p32, p16 = _inproj(xn, _prep_w_in(w_in[l]), TM_BIG)

    gate_cols = slice(C_TAIL + T_IM, C_TAIL + T_IM + 2 * M_HEADS)
    ps8 = p32[Tp:].reshape(Bs, SMP_ROWS, D_CAT)
    ps3 = ps8[:, :Ts]

    zero = lambda *s: jnp.zeros(s, F32)
    hm_p, C_p, n_p, m_p = _mlstm(p32[None], p32[:Tp, gate_cols].T[None], zero(1, SUBLANES, 2 * M_WIDTH),
                                 conv_w[l], conv_b[l], b_gates_m[l], m_norm_w[l],
                                 zero(1, M_HEADS, M_DH, M_DH), zero(1, M_HEADS, M_DH), zero(1, M_HEADS),
                                 T=Tp, L=256, RIN=256, valid=256)
    grow_s = jnp.pad(jnp.swapaxes(ps3[:, :, gate_cols], 1, 2), ((0, 0), (0, 0), (0, LANES - Ts)))
    cb_s = jnp.pad(state_conv[l], ((0, 0), (SUBLANES - (CONV_W - 1), 0), (0, 0)))
    hm_s, C_s, n_s, m_s = _mlstm(ps8, grow_s, cb_s, conv_w[l], conv_b[l], b_gates_m[l], m_norm_w[l],
                                 state_C[l], state_n[l], state_m[l], T=SUBLANES, L=LANES, RIN=SUBLANES, valid=Ts)

    ha_p = _dsa_prompt(p32, p16, Tp, min(TOPK_MAX, Tp // 4))
    qs = ps3[:, :, C_QI:C_QI + IDX_HEADS * LANES].reshape(Bs, Ts, IDX_HEADS, LANES)[..., :IDX_DIM].reshape(Bs, Ts * IDX_HEADS, IDX_DIM)
    ws = ps3[:, :, C_TAIL + T_WI:C_TAIL + T_WI + IDX_HEADS].reshape(Bs, Ts * IDX_HEADS, 1)
    kin_t = jnp.pad(jnp.swapaxes(ps3[:, :, C_TAIL + T_KI:C_TAIL + T_KI + IDX_DIM], 1, 2), ((0, 0), (0, 0), (0, LANES - Ts)))
    sc = _smp_scores2(page_table, qs, ws, jnp.swapaxes(cache_kidx[l], 1, 2))
    mp, mn = _smp_select2(sc.reshape(Bs * SMP_ROWS, P), qs, ws, kin_t, Ts, min(TOPK_MAX, (P + Ts) // 4))
    rows_th = lambda c0: ps3[:, :, c0:c0 + A_WIDTH].reshape(Bs, Ts * A_HEADS, A_DH)
    ha_s = _smp_attn2(page_table, rows_th(C_QA), rows_th(C_KA), rows_th(C_VA), mn.reshape(Bs, SMP_ROWS, LANES),
                      mp.reshape(Bs, SMP_ROWS, P), cache_k[l], cache_v[l], Ts)

    hm_all = jnp.concatenate([hm_p[0], hm_s.reshape(Ns, M_WIDTH)], axis=0)
    ha_all = jnp.concatenate([ha_p, pad_rows(ha_s.reshape(Bs, Ts, A_WIDTH)).reshape(Ns, A_WIDTH).astype(BF16)], axis=0)
    merged = _merge(hm_all, ha_all, w_proj_m[l].astype(BF16), w_proj_a[l].astype(BF16), p32, TM_BIG)
    x1 = _outproj(merged, w_out[l].astype(BF16), x_all, TM_BIG)

    wr = jnp.pad(jnp.concatenate([w_rg[l], w_re[l]], axis=1), ((0, 0), (0, LANES - N_GROUPS - N_EXPERTS))).astype(BF16)
    br = jnp.pad(jnp.concatenate([b_rg[l], b_re[l]]), (0, LANES - N_GROUPS - N_EXPERTS)).reshape(1, LANES)
    xn2, r = _router(x1, g_ffn[l], wr, br, TM_ROW)
    npad = NT * TOP_E + N_EXPERTS * MOE_TM
    blk_exp, n_act, blk_groups, src, pos = _route_tables(r, npad)
    ys = _experts(blk_exp, n_act, blk_groups, src, xn2, w_gate[l], w_up[l], w_down[l])
    y_p = _combine(pos, ys, x1, r, g_final, 0, Tp, 256)
    y_s = _combine(pos, ys, x1, r, g_final, Tp, Ns, 256)

    st = lambda a, shape: a.reshape((1,) + shape)
    pp = p32[:Tp]
    return (y_p[None], y_s.reshape(Bs, SMP_ROWS, D_MODEL)[:, :Ts],
            st(pp[:, C_KA:C_KA + A_WIDTH], (1, Tp, A_HEADS, A_DH)), st(pp[:, C_VA:C_VA + A_WIDTH], (1, Tp, A_HEADS, A_DH)),
            st(pp[:, C_TAIL + T_KI:C_TAIL + T_KI + IDX_DIM], (1, Tp, IDX_DIM)),
            st(pp[Tp - (CONV_W - 1):, 0:2 * M_WIDTH], (1, CONV_W - 1, 2 * M_WIDTH)),
            st(C_p, (1, M_HEADS, M_DH, M_DH)), st(n_p, (1, M_HEADS, M_DH)), st(m_p[:, :, 0, 0], (1, M_HEADS)),
            st(ps3[:, :, C_KA:C_KA + A_WIDTH], (Bs, Ts, A_HEADS, A_DH)), st(ps3[:, :, C_VA:C_VA + A_WIDTH], (Bs, Ts, A_HEADS, A_DH)),
            st(ps3[:, :, C_TAIL + T_KI:C_TAIL + T_KI + IDX_DIM], (Bs, Ts, IDX_DIM)),
            st(ps3[:, Ts - (CONV_W - 1):, 0:2 * M_WIDTH], (Bs, CONV_W - 1, 2 * M_WIDTH)),
            st(C_s, (Bs, M_HEADS, M_DH, M_DH)), st(n_s, (Bs, M_HEADS, M_DH)), st(m_s[:, :, 0, 0], (Bs, M_HEADS)))
```

```python
import functools

import jax
import jax.numpy as jnp
import numpy as np
from jax import lax
from jax.experimental import pallas as pl
from jax.experimental.pallas import tpu as pltpu

F32 = jnp.float32
BF16 = jnp.bfloat16
I32 = jnp.int32

D_MODEL = 2048
M_WIDTH = D_MODEL // 2
M_HEADS = 4
M_DH = M_WIDTH // M_HEADS
CONV_W = 4
A_WIDTH = D_MODEL // 2
A_DH = 128
A_HEADS = A_WIDTH // A_DH
IDX_HEADS = 8
IDX_DIM = 64
TOPK_MAX = 256
PAGE_SIZE = 128
N_GROUPS = 4
EXP_PER_GROUP = 8
N_EXPERTS = N_GROUPS * EXP_PER_GROUP
TOP_E = 2
D_EXPERT = D_MODEL // 4
RMS_EPS = 1e-6
IN_SIZES = (M_WIDTH, M_WIDTH, M_WIDTH, M_WIDTH, M_HEADS, M_HEADS, A_WIDTH, A_WIDTH, A_WIDTH,
            IDX_HEADS * IDX_DIM, IDX_DIM, IDX_HEADS, D_MODEL, D_MODEL)
IN_SPLITS = tuple(int(s) for s in np.cumsum(IN_SIZES)[:-1])

LANES = 128
SUBLANES = 8
VMEM_LIMIT = 56 * 1024 * 1024

C_QM, C_KM, C_VM, C_OM = 0, 1024, 2048, 3072
C_QA, C_KA, C_VA = 4096, 5120, 6144
C_GM, C_GA = 7168, 9216
C_QI = 11264
C_TAIL = 12288
T_KI, T_WI, T_IM, T_FM = 0, 64, 72, 76
D_CAT = 12800
PROJ_TN = 512

NEG = -1e30
INT_MIN = -2 ** 31


def _cparams(sem):
    return pltpu.CompilerParams(dimension_semantics=sem, vmem_limit_bytes=VMEM_LIMIT)


def _rms_kernel(x_ref, g_ref, o_ref):
    x = x_ref[...]
    y = x * lax.rsqrt(jnp.mean(x * x, axis=-1, keepdims=True) + RMS_EPS)
    o_ref[...] = (y * g_ref[...]).astype(o_ref.dtype)


def _rmsnorm(x, g, out_dtype, tm):
    m, d = x.shape
    return pl.pallas_call(
        _rms_kernel,
        grid=(m // tm,),
        in_specs=[pl.BlockSpec((tm, d), lambda i: (i, 0)), pl.BlockSpec((1, d), lambda i: (0, 0))],
        out_specs=pl.BlockSpec((tm, d), lambda i: (i, 0)),
        out_shape=jax.ShapeDtypeStruct((m, d), out_dtype),
        compiler_params=_cparams(("parallel",)),
        name="rmsnorm",
    )(x, g.reshape(1, d))


def _inproj_kernel(x_ref, wt_ref, o32_ref, o16_ref):
    acc = lax.dot_general(x_ref[...], wt_ref[...], (((1,), (1,)), ((), ())), preferred_element_type=F32)
    o32_ref[...] = acc
    o16_ref[...] = acc.astype(BF16)


def _inproj(xn, w_cat_t, tm):
    m, d = xn.shape
    n = w_cat_t.shape[0]
    tn = PROJ_TN
    return pl.pallas_call(
        _inproj_kernel,
        grid=(m // tm, n // tn),
        in_specs=[pl.BlockSpec((tm, d), lambda i, j: (i, 0)), pl.BlockSpec((tn, d), lambda i, j: (j, 0))],
        out_specs=[pl.BlockSpec((tm, tn), lambda i, j: (i, j)), pl.BlockSpec((tm, tn), lambda i, j: (i, j))],
        out_shape=[jax.ShapeDtypeStruct((m, n), F32), jax.ShapeDtypeStruct((m, n), BF16)],
        compiler_params=_cparams(("parallel", "parallel")),
        name="inproj",
    )(xn, w_cat_t)


def _prep_w_in(w_in):
    (q_m, k_m, v_m, o_m, i_m, f_m, q_a, k_a, v_a, q_i, k_i, w_i, g_m, g_a) = jnp.split(w_in.T, IN_SPLITS, axis=0)
    d = w_in.shape[0]
    q_i = jnp.pad(q_i.reshape(IDX_HEADS, IDX_DIM, d), ((0, 0), (0, LANES - IDX_DIM), (0, 0))).reshape(IDX_HEADS * LANES, d)
    rows = [q_m, k_m, v_m, o_m, q_a, k_a, v_a, g_m, g_a, q_i, k_i, w_i, i_m, f_m]
    rows.append(jnp.zeros((D_CAT - sum(a.shape[0] for a in rows), d), w_in.dtype))
    return jnp.concatenate(rows, axis=0).astype(BF16)


def _sigmoid(x):
    return 1.0 / (1.0 + jnp.exp(-x))


def _log_sigmoid(x):
    return jnp.minimum(x, 0.0) - jnp.log1p(jnp.exp(-jnp.abs(x)))


def _mlstm_kernel(q_ref, k_ref, v_ref, o_ref, tail_ref, grow_ref, cb_ref, convw_ref, convb_ref, bl_ref, bs_ref,
                  nw_ref, c0_ref, n0_ref, m0_ref,
                  h_ref, cout_ref, nout_ref, mout_ref,
                  xq_s, xk_s, c_s, n_s, m_s, vp_s, op_s, tp_s, *, L, RIN, valid):
    c = pl.program_id(1)
    nc = pl.num_programs(1)

    @pl.when(c == 0)
    def _init():
        xq_s[0:SUBLANES, :] = cb_ref[0, :, 0:M_WIDTH]
        xk_s[0:SUBLANES, :] = cb_ref[0, :, M_WIDTH:2 * M_WIDTH]
        c_s[...] = c0_ref[0]
        n_s[...] = n0_ref[0]
        m_s[...] = m0_ref[0]

    if RIN < L:
        zpad = jnp.zeros((L - RIN, M_WIDTH), F32)
        xq_s[SUBLANES + RIN:SUBLANES + L, :] = zpad
        xk_s[SUBLANES + RIN:SUBLANES + L, :] = zpad
        vp_s[RIN:L, :] = zpad
        op_s[RIN:L, :] = zpad
        tp_s[RIN:L, :] = jnp.zeros((L - RIN, LANES), F32)
    xq_s[SUBLANES:SUBLANES + RIN, :] = q_ref[0]
    xk_s[SUBLANES:SUBLANES + RIN, :] = k_ref[0]
    vp_s[0:RIN, :] = v_ref[0]
    op_s[0:RIN, :] = o_ref[0]
    tp_s[0:RIN, :] = tail_ref[0]

    def conv(xs, col0):
        w = convw_ref[:, col0:col0 + M_WIDTH]
        y = convb_ref[:, col0:col0 + M_WIDTH]
        for j in range(CONV_W):
            r0 = SUBLANES - (CONV_W - 1) + j
            y = y + xs[r0:r0 + L, :] * w[j:j + 1, :]
        return y * _sigmoid(y)

    q_all = conv(xq_s, 0) * (M_DH ** -0.5)
    k_all = conv(xk_s, M_WIDTH)
    v_all = vp_s[...]
    tail = tp_s[...] + bl_ref[...]
    grow = grow_ref[0] + bs_ref[...]

    xq_s[0:SUBLANES, :] = xq_s[L:L + SUBLANES, :]
    xk_s[0:SUBLANES, :] = xk_s[L:L + SUBLANES, :]

    tt = lax.broadcasted_iota(I32, (L, L), 0)
    ss = lax.broadcasted_iota(I32, (L, L), 1)
    causal = ss <= tt
    row_ok = lax.broadcasted_iota(I32, (L, 1), 0) < valid
    col_ok = lax.broadcasted_iota(I32, (1, L), 1) < valid

    for h in range(M_HEADS):
        hs = slice(h * M_DH, (h + 1) * M_DH)
        qh = q_all[:, hs]
        kh = k_all[:, hs]
        vh = v_all[:, hs]
        ig_c = jnp.where(row_ok, tail[:, T_IM + h:T_IM + h + 1], NEG)
        lf_c = jnp.where(row_ok, _log_sigmoid(tail[:, T_FM + h:T_FM + h + 1]), 0.0)
        ig_r = jnp.where(col_ok, grow[h:h + 1, :], NEG)
        lf_r = jnp.where(col_ok, _log_sigmoid(grow[M_HEADS + h:M_HEADS + h + 1, :]), 0.0)
        b_c = jnp.sum(jnp.where(causal, lf_r, 0.0), axis=1, keepdims=True)
        b_r = jnp.sum(jnp.where(tt <= ss, lf_c, 0.0), axis=0, keepdims=True)
        dmat = jnp.where(causal, b_c - b_r + ig_r, NEG)
        m_prev = m_s[h][:, 0:1]
        m_t = jnp.maximum(b_c + m_prev, jnp.max(dmat, axis=1, keepdims=True))
        e = jnp.exp(dmat - m_t)
        qb = qh.astype(BF16)
        kb = kh.astype(BF16)
        s = lax.dot_general(qb, kb, (((1,), (1,)), ((), ())), preferred_element_type=F32) * e
        inter = jnp.exp(b_c + m_prev - m_t)
        ch = c_s[h]
        num = jnp.dot(s.astype(BF16), vh.astype(BF16), preferred_element_type=F32) + inter * lax.dot_general(
            qb, ch.astype(BF16), (((1,), (1,)), ((), ())), preferred_element_type=F32)
        nh = n_s[h]
        den = jnp.sum(s, axis=1, keepdims=True) + inter * jnp.sum(qh * nh, axis=1, keepdims=True)
        hh = num / jnp.maximum(jnp.abs(den), jnp.exp(-m_t))
        hh = hh * lax.rsqrt(jnp.mean(hh * hh, axis=1, keepdims=True) + RMS_EPS)
        out = _sigmoid(op_s[:, hs]) * (hh * nw_ref[:, hs])
        h_ref[0, :, hs] = out[0:RIN, :].astype(h_ref.dtype)
        m_new = m_t[L - 1:L, :]
        b_last = b_c[L - 1:L, :]
        w_c = jnp.exp(b_last - b_c + ig_c - m_new)
        decay = jnp.exp(b_last + m_prev - m_new)
        upd = lax.dot_general((w_c * vh).astype(BF16), kb, (((0,), (0,)), ((), ())), preferred_element_type=F32)
        c_s[h] = decay * ch + upd
        n_s[h] = decay * nh + jnp.sum(w_c * kh, axis=0, keepdims=True)
        m_s[h] = jnp.broadcast_to(m_new, (1, LANES))

    @pl.when(c == nc - 1)
    def _fin():
        cout_ref[0] = c_s[...]
        nout_ref[0] = n_s[...]
        mout_ref[0] = m_s[...]


def _mlstm(p32, grow, convbuf, conv_w, conv_b, b_gates, m_norm_w, c0, n0, m0, *, T, L, RIN, valid):
    B = p32.shape[0]
    nc = T // RIN
    bl = jnp.zeros((1, LANES), F32).at[0, T_IM:T_IM + 2 * M_HEADS].set(b_gates)
    bs = jnp.broadcast_to(b_gates[:, None], (2 * M_HEADS, L))
    kern = functools.partial(_mlstm_kernel, L=L, RIN=RIN, valid=valid)
    cblk = lambda col: pl.BlockSpec((1, RIN, M_WIDTH), lambda b, c, col=col: (b, c, col // M_WIDTH))
    const2 = lambda shape: pl.BlockSpec(shape, lambda b, c: (0, 0))
    per_b = lambda shape: pl.BlockSpec(shape, lambda b, c: (b,) + (0,) * (len(shape) - 1))
    return pl.pallas_call(
        kern,
        grid=(B, nc),
        in_specs=[cblk(C_QM), cblk(C_KM), cblk(C_VM), cblk(C_OM),
                  pl.BlockSpec((1, RIN, LANES), lambda b, c: (b, c, C_TAIL // LANES)),
                  pl.BlockSpec((1, 2 * M_HEADS, L), lambda b, c: (b, 0, c)),
                  per_b((1, SUBLANES, 2 * M_WIDTH)),
                  const2((CONV_W, 2 * M_WIDTH)), const2((1, 2 * M_WIDTH)), const2((1, LANES)),
                  const2((2 * M_HEADS, L)), const2((1, M_WIDTH)),
                  per_b((1, M_HEADS, M_DH, M_DH)), per_b((1, M_HEADS, 1, M_DH)), per_b((1, M_HEADS, 1, LANES))],
        out_specs=[pl.BlockSpec((1, RIN, M_WIDTH), lambda b, c: (b, c, 0)),
                   per_b((1, M_HEADS, M_DH, M_DH)), per_b((1, M_HEADS, 1, M_DH)), per_b((1, M_HEADS, 1, LANES))],
        out_shape=[jax.ShapeDtypeStruct((B, T, M_WIDTH), BF16),
                   jax.ShapeDtypeStruct((B, M_HEADS, M_DH, M_DH), F32),
                   jax.ShapeDtypeStruct((B, M_HEADS, 1, M_DH), F32),
                   jax.ShapeDtypeStruct((B, M_HEADS, 1, LANES), F32)],
        scratch_shapes=[pltpu.VMEM((SUBLANES + L, M_WIDTH), F32), pltpu.VMEM((SUBLANES + L, M_WIDTH), F32),
                        pltpu.VMEM((M_HEADS, M_DH, M_DH), F32), pltpu.VMEM((M_HEADS, 1, M_DH), F32),
                        pltpu.VMEM((M_HEADS, 1, LANES), F32),
                        pltpu.VMEM((L, M_WIDTH), F32), pltpu.VMEM((L, M_WIDTH), F32), pltpu.VMEM((L, LANES), F32)],
        compiler_params=_cparams(("parallel", "arbitrary")),
        name="mlstm",
    )(p32, p32, p32, p32, p32, grow, convbuf, conv_w, conv_b.reshape(1, -1), bl, bs, m_norm_w.reshape(1, -1),
      c0, n0.reshape(B, M_HEADS, 1, M_DH), jnp.broadcast_to(m0[:, :, None, None], (B, M_HEADS, 1, LANES)))


def _score_key(sc):
    bits = lax.bitcast_convert_type(sc, I32)
    return jnp.where(bits < 0, INT_MIN - bits, bits)


DSA_TQ = 512
DSA_TK = 512
DSA_RG = 64
DSA_RB = 512
LOG2E = 1.4426950408889634


def _dsa_prompt_kernel(qi_tab, kj_tab, qidx_ref, tail_ref, kit_ref, qa_ref, ka_ref, va_ref, o_ref,
                       keys_s, cand_s, cnt_s, thr_s, thrm_s, cut_s, bias_s, tie_s, wrep_s, m_s, l_s, acc_s,
                       *, TQ, TK, topk, pos_bits):
    step = pl.program_id(0)
    qi = qi_tab[step]
    kj = kj_tab[step]
    RG = DSA_RG
    nlc = TK // LANES

    def count_pass(pred):
        def rbody(r, _):
            r0 = pl.multiple_of(r * RG, RG)
            cand = cand_s[pl.ds(r0, RG), :]
            aux = thr_s[pl.ds(r0, RG), :]

            def kb(j, cnt):
                for c in range(nlc):
                    blk = keys_s[j, pl.ds(r0, RG), c * LANES:(c + 1) * LANES]
                    pos = j * TK + c * LANES + lax.broadcasted_iota(I32, (RG, LANES), 1)
                    cnt = jnp.where(pred(blk, cand, aux, pos), cnt + 1.0, cnt)
                return cnt

            cnt_s[pl.ds(r0, RG), :] = lax.fori_loop(0, qi + 1, kb, jnp.zeros((RG, LANES), F32))
            return 0

        lax.fori_loop(0, TQ // RG, rbody, 0)
        return jnp.sum(cnt_s[...], axis=1, keepdims=True)

    @pl.when(kj == 0)
    def _phase1():
        w = tail_ref[:, T_WI:T_WI + IDX_HEADS] * (IDX_HEADS ** -0.5) * (IDX_DIM ** -0.5)
        for h in range(IDX_HEADS):
            wrep_s[h] = jnp.broadcast_to(w[:, h:h + 1], (TQ, LANES))
        rowpos = qi * TQ + lax.broadcasted_iota(I32, (TQ, TK), 0)

        def kbody(j, _):
            kt = kit_ref[j].astype(BF16)
            sc = jnp.zeros((TQ, TK), F32)
            for h in range(IDX_HEADS):
                qh = qidx_ref[:, h * LANES:(h + 1) * LANES].astype(BF16)
                s = jnp.dot(qh, kt, preferred_element_type=F32)
                sc = sc + jnp.maximum(s, 0.0) * jnp.concatenate([wrep_s[h]] * nlc, axis=1)
            colpos = j * TK + lax.broadcasted_iota(I32, (TQ, TK), 1)
            keys_s[j] = jnp.where(colpos <= rowpos, _score_key(sc), INT_MIN)
            return 0

        lax.fori_loop(0, qi + 1, kbody, 0)

        ge = lambda blk, cand, aux, pos: blk >= cand
        cand_s[...] = jnp.zeros((TQ, LANES), I32)
        cnt = count_pass(ge)
        lo = jnp.where(cnt >= topk, 0, INT_MIN).astype(I32)

        def bit_body(b, carry):
            lo, n_lo = carry
            cand = lo | lax.shift_left(jnp.int32(1), 30 - b)
            cand_s[...] = jnp.broadcast_to(cand, (TQ, LANES))
            cnt = count_pass(ge)
            take = cnt >= topk
            return jnp.where(take, cand, lo), jnp.where(take, cnt, n_lo)

        thr, n_ge = lax.fori_loop(0, 31, bit_body, (lo, cnt))
        thr_s[...] = jnp.broadcast_to(thr, (TQ, LANES))
        short = thr == INT_MIN
        thrm_s[...] = jnp.broadcast_to(jnp.where(short, INT_MIN, thr - 1), (TQ, LANES))
        cut_s[...] = jnp.broadcast_to(jnp.where(short, -1, 2 ** 30).astype(I32), (TQ, LANES))
        surplus = jnp.max(jnp.where((n_ge > topk) & jnp.logical_not(short), 1.0, 0.0))
        tie_s[0] = (surplus > 0.0).astype(I32)

        @pl.when(surplus > 0.0)
        def _ties():
            cand_s[...] = thr_s[...]
            need = topk - count_pass(lambda blk, cand, aux, pos: blk > cand)
            eq_below = lambda blk, cand, aux, pos: (blk == aux) & (pos < cand)

            def tie_body(b, x):
                cand = x + lax.shift_left(jnp.int32(1), pos_bits - 1 - b)
                cand_s[...] = jnp.broadcast_to(cand, (TQ, LANES))
                cnt = count_pass(eq_below)
                return jnp.where(cnt < need, cand, x)

            x = lax.fori_loop(0, pos_bits, tie_body, jnp.zeros((TQ, 1), I32))
            cut_s[...] = jnp.broadcast_to(jnp.where(short, -1, x), (TQ, LANES))

        m_s[...] = jnp.full(m_s.shape, NEG, F32)
        l_s[...] = jnp.zeros(l_s.shape, F32)
        acc_s[...] = jnp.zeros(acc_s.shape, F32)

    tile_l = lambda a: jnp.concatenate([a] * nlc, axis=1)

    @pl.when(tie_s[0] == 0)
    def _bias_plain():
        bias_s[...] = jnp.where(keys_s[kj] > tile_l(thrm_s[...]), 0.0, NEG)

    @pl.when(tie_s[0] != 0)
    def _bias_tied():
        key = keys_s[kj]
        thr = tile_l(thr_s[...])
        colpos = kj * TK + lax.broadcasted_iota(I32, (TQ, TK), 1)
        sel = (key > thr) | ((key == thr) & (colpos <= tile_l(cut_s[...])))
        bias_s[...] = jnp.where(sel, 0.0, NEG)

    RB = DSA_RB
    c1 = (A_DH ** -0.5) * LOG2E

    def rb_body(rb, _):
        rows = pl.ds(pl.multiple_of(rb * RB, RB), RB)
        bias = bias_s[rows, :]
        hsl = [slice(h * A_DH, (h + 1) * A_DH) for h in range(A_HEADS)]
        m_old = [m_s[h, rows, :] for h in range(A_HEADS)]
        l_old = [l_s[h, rows, :] for h in range(A_HEADS)]
        a_old = [acc_s[rows, hs] for hs in hsl]
        ts = [lax.dot_general(qa_ref[rows, hs], ka_ref[:, hs], (((1,), (1,)), ((), ())), preferred_element_type=F32) * c1 + bias
              for hs in hsl]
        m_new = [jnp.maximum(m_old[h], jnp.max(ts[h], axis=1, keepdims=True)) for h in range(A_HEADS)]
        l_new, a_new = [], []
        for h in range(A_HEADS):
            alpha = jnp.exp2(m_old[h] - m_new[h])
            p = jnp.exp2(ts[h] - tile_l(m_new[h]))
            psum = p[:, 0:LANES]
            for c in range(1, nlc):
                psum = psum + p[:, c * LANES:(c + 1) * LANES]
            l_new.append(alpha * l_old[h] + psum)
            a_new.append(alpha * a_old[h] + jnp.dot(p.astype(BF16), va_ref[:, hsl[h]], preferred_element_type=F32))
        for h in range(A_HEADS):
            m_s[h, rows, :] = m_new[h]
            l_s[h, rows, :] = l_new[h]
            acc_s[rows, hsl[h]] = a_new[h]
        return 0

    lax.fori_loop(0, TQ // RB, rb_body, 0)

    @pl.when(kj == qi)
    def _fin():
        for h in range(A_HEADS):
            hs = slice(h * A_DH, (h + 1) * A_DH)
            o_ref[:, hs] = (acc_s[:, hs] / jnp.sum(l_s[h], axis=1, keepdims=True)).astype(o_ref.dtype)


def _dsa_prompt(p32, p16, T, topk):
    TQ, TK = DSA_TQ, DSA_TK
    nq = T // TQ
    assert TQ == TK
    qi_tab = np.concatenate([np.full(i + 1, i) for i in range(nq)]).astype(np.int32)
    kj_tab = np.concatenate([np.arange(i + 1) for i in range(nq)]).astype(np.int32)
    ki = p32[:T, C_TAIL + T_KI:C_TAIL + T_KI + IDX_DIM]
    kit = jnp.pad(ki.T, ((0, LANES - IDX_DIM), (0, 0))).reshape(LANES, T // TK, TK).transpose(1, 0, 2)
    kern = functools.partial(_dsa_prompt_kernel, TQ=TQ, TK=TK, topk=topk, pos_bits=int(T - 1).bit_length())
    gs = pltpu.PrefetchScalarGridSpec(
        num_scalar_prefetch=2,
        grid=(len(qi_tab),),
        in_specs=[pl.BlockSpec((TQ, IDX_HEADS * LANES), lambda s, qt, kt: (qt[s], C_QI // (IDX_HEADS * LANES))),
                  pl.BlockSpec((TQ, LANES), lambda s, qt, kt: (qt[s], C_TAIL // LANES)),
                  pl.BlockSpec((T // TK, LANES, TK), lambda s, qt, kt: (0, 0, 0)),
                  pl.BlockSpec((TQ, A_WIDTH), lambda s, qt, kt: (qt[s], C_QA // A_WIDTH)),
                  pl.BlockSpec((TK, A_WIDTH), lambda s, qt, kt: (kt[s], C_KA // A_WIDTH)),
                  pl.BlockSpec((TK, A_WIDTH), lambda s, qt, kt: (kt[s], C_VA // A_WIDTH))],
        out_specs=pl.BlockSpec((TQ, A_WIDTH), lambda s, qt, kt: (qt[s], 0)),
        scratch_shapes=[pltpu.VMEM((T // TK, TQ, TK), I32),
                        pltpu.VMEM((TQ, LANES), I32), pltpu.VMEM((TQ, LANES), F32), pltpu.VMEM((TQ, LANES), I32),
                        pltpu.VMEM((TQ, LANES), I32), pltpu.VMEM((TQ, LANES), I32),
                        pltpu.VMEM((TQ, TK), F32), pltpu.SMEM((1,), I32), pltpu.VMEM((IDX_HEADS, TQ, LANES), F32),
                        pltpu.VMEM((A_HEADS, TQ, LANES), F32), pltpu.VMEM((A_HEADS, TQ, LANES), F32),
                        pltpu.VMEM((TQ, A_WIDTH), F32)])
    return pl.pallas_call(
        kern, grid_spec=gs,
        out_shape=jax.ShapeDtypeStruct((T, A_WIDTH), BF16),
        compiler_params=_cparams(("arbitrary",)),
        name="dsa_prompt",
    )(jnp.asarray(qi_tab), jnp.asarray(kj_tab), p32, p32, kit, p16, p16, p16)


SMP_PGS = 32
SMP_PG = 16
SMP_ROWS = SUBLANES
SMP_SEL_B = 16


def _idx_scores_t(q, w, kt):
    s = jnp.dot(q.astype(BF16), kt.astype(BF16), preferred_element_type=F32)
    s = jnp.maximum(s * (IDX_DIM ** -0.5), 0.0) * (w * (IDX_HEADS ** -0.5))
    n_tok = q.shape[0] // IDX_HEADS
    rows = [jnp.sum(s[t * IDX_HEADS:(t + 1) * IDX_HEADS], axis=0, keepdims=True) for t in range(n_tok)]
    rows.append(jnp.zeros((SMP_ROWS - n_tok, s.shape[1]), F32))
    return jnp.concatenate(rows, axis=0)


def _smp_scores2_kernel(pt_ref, q_ref, w_ref, *refs):
    o_ref = refs[-1]
    kt = jnp.concatenate([r[0] for r in refs[:-1]], axis=1)
    o_ref[0] = _idx_scores_t(q_ref[0], w_ref[0], kt)


def _smp_scores2(page_table, qs, ws, kidx_t):
    B, n_pages = page_table.shape
    R = qs.shape[1]
    page = lambda i: pl.BlockSpec((1, IDX_DIM, PAGE_SIZE), lambda b, p, pt: (pt[b, p * SMP_PGS + i], 0, 0))
    gs = pltpu.PrefetchScalarGridSpec(
        num_scalar_prefetch=1,
        grid=(B, n_pages // SMP_PGS),
        in_specs=[pl.BlockSpec((1, R, IDX_DIM), lambda b, p, pt: (b, 0, 0)),
                  pl.BlockSpec((1, R, 1), lambda b, p, pt: (b, 0, 0))] + [page(i) for i in range(SMP_PGS)],
        out_specs=pl.BlockSpec((1, SMP_ROWS, SMP_PGS * PAGE_SIZE), lambda b, p, pt: (b, 0, p)))
    return pl.pallas_call(
        _smp_scores2_kernel, grid_spec=gs,
        out_shape=jax.ShapeDtypeStruct((B, SMP_ROWS, n_pages * PAGE_SIZE), F32),
        compiler_params=_cparams(("parallel", "arbitrary")),
        name="smp_scores",
    )(page_table, qs, ws, *([kidx_t] * SMP_PGS))


def _smp_select2_kernel(sc_ref, q_ref, w_ref, kint_ref, mp_ref, mn_ref, keys_s, cand_s, thr_s, cut_s, cnt_s,
                        *, n_tok, topk):
    NB = q_ref.shape[0]
    R = NB * SMP_ROWS
    P = sc_ref.shape[1]
    NCH = P // LANES
    RG = DSA_RG
    lane = lax.broadcasted_iota(I32, (R, LANES), 1)
    trow = lax.broadcasted_iota(I32, (R, LANES), 0) % SMP_ROWS
    for c in range(NCH):
        keys_s[c] = _score_key(sc_ref[:, c * LANES:(c + 1) * LANES])
    s_new = jnp.concatenate([_idx_scores_t(q_ref[b], w_ref[b], kint_ref[b]) for b in range(NB)], axis=0)
    keys_s[NCH] = jnp.where(lane <= trow, _score_key(s_new), INT_MIN)

    def count_pass(pred):
        def rbody(r, _):
            rows = pl.ds(pl.multiple_of(r * RG, RG), RG)
            cand = cand_s[rows, :]
            aux = thr_s[rows, :]

            def cb(c, cnt):
                pos = c * LANES + lax.broadcasted_iota(I32, (RG, LANES), 1)
                return cnt + jnp.where(pred(keys_s[c, rows, :], cand, aux, pos), 1.0, 0.0)

            cnt_s[rows, :] = lax.fori_loop(0, NCH + 1, cb, jnp.zeros((RG, LANES), F32))
            return 0

        lax.fori_loop(0, R // RG, rbody, 0)
        return jnp.sum(cnt_s[...], axis=1, keepdims=True)

    ge = lambda blk, cand, aux, pos: blk >= cand
    cand_s[...] = jnp.zeros((R, LANES), I32)
    cnt0 = count_pass(ge)
    lo = jnp.where(cnt0 >= topk, 0, INT_MIN).astype(I32)

    def bit_body(b, carry):
        lo, n_lo = carry
        cand = lo | lax.shift_left(jnp.int32(1), 30 - b)
        cand_s[...] = jnp.broadcast_to(cand, (R, LANES))
        cnt = count_pass(ge)
        take = cnt >= topk
        return jnp.where(take, cand, lo), jnp.where(take, cnt, n_lo)

    thr, n_ge = lax.fori_loop(0, 31, bit_body, (lo, cnt0))
    thr_s[...] = jnp.broadcast_to(thr, (R, LANES))
    cut_s[...] = jnp.full((R, LANES), 2 ** 30, I32)
    real_row = lax.broadcasted_iota(I32, (R, 1), 0) % SMP_ROWS < n_tok
    surplus = jnp.max(jnp.where((n_ge > topk) & (thr != INT_MIN) & real_row, 1.0, 0.0))

    @pl.when(surplus > 0.0)
    def _ties():
        cand_s[...] = thr_s[...]
        need = topk - count_pass(lambda blk, cand, aux, pos: blk > cand)
        pos_bits = int(P + LANES - 1).bit_length()
        eq_below = lambda blk, cand, aux, pos: (blk == aux) & (pos < cand)

        def tie_body(b, x):
            cand = x + lax.shift_left(jnp.int32(1), pos_bits - 1 - b)
            cand_s[...] = jnp.broadcast_to(cand, (R, LANES))
            return jnp.where(count_pass(eq_below) < need, cand, x)

        cut_s[...] = jnp.broadcast_to(lax.fori_loop(0, pos_bits, tie_body, jnp.zeros((R, 1), I32)), (R, LANES))

    cut = cut_s[...]
    thr_b = thr_s[...]
    row_ok = trow < n_tok
    for c in range(NCH + 1):
        key = keys_s[c]
        sel = ((key > thr_b) | ((key == thr_b) & (c * LANES + lane <= cut))) & (key != INT_MIN) & row_ok
        if c < NCH:
            mp_ref[:, c * LANES:(c + 1) * LANES] = jnp.where(sel, 1.0, 0.0)
        else:
            mn_ref[...] = jnp.where(sel, 1.0, 0.0)


def _smp_select2(sc, qs, ws, kin_t, n_tok, topk):
    R, P = sc.shape
    B = qs.shape[0]
    NB = SMP_SEL_B
    RS = NB * SMP_ROWS
    Rq = qs.shape[1]
    return pl.pallas_call(
        functools.partial(_smp_select2_kernel, n_tok=n_tok, topk=topk),
        grid=(B // NB,),
        in_specs=[pl.BlockSpec((RS, P), lambda i: (i, 0)),
                  pl.BlockSpec((NB, Rq, IDX_DIM), lambda i: (i, 0, 0)), pl.BlockSpec((NB, Rq, 1), lambda i: (i, 0, 0)),
                  pl.BlockSpec((NB, IDX_DIM, LANES), lambda i: (i, 0, 0))],
        out_specs=[pl.BlockSpec((RS, P), lambda i: (i, 0)), pl.BlockSpec((RS, LANES), lambda i: (i, 0))],
        out_shape=[jax.ShapeDtypeStruct((R, P), F32), jax.ShapeDtypeStruct((R, LANES), F32)],
        scratch_shapes=[pltpu.VMEM((P // LANES + 1, RS, LANES), I32), pltpu.VMEM((RS, LANES), I32),
                        pltpu.VMEM((RS, LANES), I32), pltpu.VMEM((RS, LANES), I32), pltpu.VMEM((RS, LANES), F32)],
        compiler_params=_cparams(("parallel",)),
        name="smp_select",
    )(sc, qs, ws, kin_t)


def _smp_attn2_kernel(pt_ref, q_ref, kn_ref, vn_ref, mn_ref, mp_ref, *refs, n_tok):
    k_refs = refs[:SMP_PG]
    v_refs = refs[SMP_PG:2 * SMP_PG]
    o_ref, kpad_s, vpad_s, m_s, l_s, acc_s = refs[2 * SMP_PG:]
    p = pl.program_id(1)
    NL = PAGE_SIZE * A_HEADS
    R = n_tok * A_HEADS
    c1 = (A_DH ** -0.5) * LOG2E
    diag = jnp.where(lax.broadcasted_iota(I32, (A_HEADS, NL), 1) % A_HEADS == lax.broadcasted_iota(I32, (A_HEADS, NL), 0), 1.0, 0.0)
    expand = jnp.where(lax.broadcasted_iota(I32, (PAGE_SIZE, NL), 1) // A_HEADS == lax.broadcasted_iota(I32, (PAGE_SIZE, NL), 0),
                       1.0, 0.0).astype(BF16)
    qb = q_ref[0].astype(BF16)

    def attend(k_list, v_list, masks):
        n = len(k_list)
        x = jnp.dot(jnp.concatenate(masks, axis=0).astype(BF16), expand, preferred_element_type=F32)
        ts = []
        for i in range(n):
            s = lax.dot_general(qb, k_list[i], (((1,), (1,)), ((), ())), preferred_element_type=F32)
            ok = jnp.concatenate([jnp.broadcast_to(x[i * SMP_ROWS + t:i * SMP_ROWS + t + 1, :], (A_HEADS, NL)) * diag
                                  for t in range(n_tok)], axis=0)
            ts.append(s * c1 + jnp.where(ok > 0.5, 0.0, NEG))
        m_old = m_s[...]
        m_new = jnp.maximum(m_old, jnp.max(jnp.concatenate(ts, axis=1), axis=1, keepdims=True))
        alpha = jnp.exp2(m_old - m_new)
        acc = alpha * acc_s[...]
        lsum = alpha * l_s[...]
        m_t = jnp.concatenate([m_new] * (NL // LANES), axis=1)
        for i in range(n):
            pr = jnp.exp2(ts[i] - m_t)
            for c in range(NL // LANES):
                lsum = lsum + pr[:, c * LANES:(c + 1) * LANES]
            acc = acc + jnp.dot(pr.astype(BF16), v_list[i], preferred_element_type=F32)
        m_s[...] = m_new
        l_s[...] = lsum
        acc_s[...] = acc

    @pl.when(p == 0)
    def _first():
        m_s[...] = jnp.full(m_s.shape, NEG, F32)
        l_s[...] = jnp.zeros(l_s.shape, F32)
        acc_s[...] = jnp.zeros(acc_s.shape, F32)
        kpad_s[...] = jnp.zeros(kpad_s.shape, F32)
        vpad_s[...] = jnp.zeros(vpad_s.shape, F32)
        kpad_s[0:R, :] = kn_ref[0]
        vpad_s[0:R, :] = vn_ref[0]
        attend([kpad_s[...].astype(BF16)], [vpad_s[...].astype(BF16)], [mn_ref[0]])

    attend([r[0].reshape(NL, A_DH).astype(BF16) for r in k_refs], [r[0].reshape(NL, A_DH).astype(BF16) for r in v_refs],
           [mp_ref[0][:, i * PAGE_SIZE:(i + 1) * PAGE_SIZE] for i in range(SMP_PG)])

    @pl.when(p == pl.num_programs(1) - 1)
    def _fin():
        o_ref[0] = acc_s[...] / jnp.sum(l_s[...], axis=1, keepdims=True)


def _smp_attn2(page_table, q32, kn32, vn32, mn, mp, ck, cv, n_tok):
    B, n_pages = page_table.shape
    R = n_tok * A_HEADS
    NL = PAGE_SIZE * A_HEADS
    per_b = lambda shape: pl.BlockSpec(shape, lambda b, p, pt: (b, 0, 0))
    page = lambda i: pl.BlockSpec((1, PAGE_SIZE, A_HEADS, A_DH), lambda b, p, pt: (pt[b, p * SMP_PG + i], 0, 0, 0))
    gs = pltpu.PrefetchScalarGridSpec(
        num_scalar_prefetch=1,
        grid=(B, n_pages // SMP_PG),
        in_specs=[per_b((1, R, A_DH)), per_b((1, R, A_DH)), per_b((1, R, A_DH)), per_b((1, SMP_ROWS, LANES)),
                  pl.BlockSpec((1, SMP_ROWS, SMP_PG * PAGE_SIZE), lambda b, p, pt: (b, 0, p))]
                 + [page(i) for i in range(SMP_PG)] + [page(i) for i in range(SMP_PG)],
        out_specs=per_b((1, R, A_DH)),
        scratch_shapes=[pltpu.VMEM((NL, A_DH), F32), pltpu.VMEM((NL, A_DH), F32),
                        pltpu.VMEM((R, LANES), F32), pltpu.VMEM((R, LANES), F32), pltpu.VMEM((R, A_DH), F32)])
    return pl.pallas_call(
        functools.partial(_smp_attn2_kernel, n_tok=n_tok), grid_spec=gs,
        out_shape=jax.ShapeDtypeStruct((B, R, A_DH), F32),
        compiler_params=_cparams(("parallel", "arbitrary")),
        name="smp_attn",
    )(page_table, q32, kn32, vn32, mn, mp, *([ck] * SMP_PG), *([cv] * SMP_PG))


def _merge_kernel(hm_ref, ha_ref, wm_ref, wa_ref, gm_ref, ga_ref, o_ref):
    a = jnp.dot(hm_ref[...], wm_ref[...], preferred_element_type=F32)
    b = jnp.dot(ha_ref[...], wa_ref[...], preferred_element_type=F32)
    o_ref[...] = (_sigmoid(gm_ref[...]) * a + _sigmoid(ga_ref[...]) * b).astype(o_ref.dtype)


def _merge(hm, ha, wm, wa, p32, tm):
    m = hm.shape[0]
    tn = PROJ_TN
    return pl.pallas_call(
        _merge_kernel,
        grid=(D_MODEL // tn, m // tm),
        in_specs=[pl.BlockSpec((tm, M_WIDTH), lambda j, i: (i, 0)), pl.BlockSpec((tm, A_WIDTH), lambda j, i: (i, 0)),
                  pl.BlockSpec((M_WIDTH, tn), lambda j, i: (0, j)), pl.BlockSpec((A_WIDTH, tn), lambda j, i: (0, j)),
                  pl.BlockSpec((tm, tn), lambda j, i: (i, C_GM // tn + j)),
                  pl.BlockSpec((tm, tn), lambda j, i: (i, C_GA // tn + j))],
        out_specs=pl.BlockSpec((tm, tn), lambda j, i: (i, j)),
        out_shape=jax.ShapeDtypeStruct((m, D_MODEL), BF16),
        compiler_params=_cparams(("parallel", "parallel")),
        name="merge",
    )(hm, ha, wm, wa, p32, p32)


def _outproj_kernel(mg_ref, w_ref, x_ref, o_ref):
    o_ref[...] = x_ref[...] + jnp.dot(mg_ref[...], w_ref[...], preferred_element_type=F32)


def _outproj(mg, w, x, tm):
    m = mg.shape[0]
    tn = PROJ_TN
    return pl.pallas_call(
        _outproj_kernel,
        grid=(D_MODEL // tn, m // tm),
        in_specs=[pl.BlockSpec((tm, D_MODEL), lambda j, i: (i, 0)), pl.BlockSpec((D_MODEL, tn), lambda j, i: (0, j)),
                  pl.BlockSpec((tm, tn), lambda j, i: (i, j))],
        out_specs=pl.BlockSpec((tm, tn), lambda j, i: (i, j)),
        out_shape=jax.ShapeDtypeStruct((m, D_MODEL), F32),
        compiler_params=_cparams(("parallel", "parallel")),
        name="outproj",
    )(mg, w, x)


MOE_TM = 256
MOE_NBUF = 3
MOE_RG = 8


def _router_kernel(x_ref, g_ref, wr_ref, br_ref, xn_ref, r_ref):
    x = x_ref[...]
    y = (x * lax.rsqrt(jnp.mean(x * x, axis=-1, keepdims=True) + RMS_EPS)) * g_ref[...]
    xn_ref[...] = y
    lg = jnp.dot(y.astype(BF16), wr_ref[...], preferred_element_type=F32) + br_ref[...]
    lane = lax.broadcasted_iota(I32, lg.shape, 1).astype(F32)
    far = float(LANES)
    gmask = lane < N_GROUPS
    gl = jnp.where(gmask, lg, NEG)
    mg = jnp.max(gl, axis=1, keepdims=True)
    p_g = 1.0 / jnp.sum(jnp.where(gmask, jnp.exp(gl - mg), 0.0), axis=1, keepdims=True)
    g_sel = jnp.min(jnp.where(gmask & (gl == mg), lane, far), axis=1, keepdims=True)
    e_lo = N_GROUPS + g_sel * EXP_PER_GROUP
    emask = (lane >= e_lo) & (lane < e_lo + EXP_PER_GROUP)
    el = jnp.where(emask, lg, NEG)
    me = jnp.max(el, axis=1, keepdims=True)
    pe = jnp.where(emask, jnp.exp(el - me), 0.0)
    probs = pe / jnp.sum(pe, axis=1, keepdims=True)
    p1 = jnp.max(probs, axis=1, keepdims=True)
    i1 = jnp.min(jnp.where(emask & (probs == p1), lane, far), axis=1, keepdims=True)
    probs2 = jnp.where(lane == i1, -1.0, probs)
    p2 = jnp.max(probs2, axis=1, keepdims=True)
    i2 = jnp.min(jnp.where(emask & (probs2 == p2), lane, far), axis=1, keepdims=True)
    tot = p1 + p2
    vals = [i1 - N_GROUPS, i2 - N_GROUPS, p_g * (p1 / tot), p_g * (p2 / tot)]
    out = jnp.zeros(lg.shape, F32)
    for c, v in enumerate(vals):
        out = jnp.where(lane == c, v, out)
    r_ref[...] = out


def _router(x1, g, wr, br, tm):
    m, d = x1.shape
    return pl.pallas_call(
        _router_kernel,
        grid=(m // tm,),
        in_specs=[pl.BlockSpec((tm, d), lambda i: (i, 0)), pl.BlockSpec((1, d), lambda i: (0, 0)),
                  pl.BlockSpec((d, LANES), lambda i: (0, 0)), pl.BlockSpec((1, LANES), lambda i: (0, 0))],
        out_specs=[pl.BlockSpec((tm, d), lambda i: (i, 0)), pl.BlockSpec((tm, LANES), lambda i: (i, 0))],
        out_shape=[jax.ShapeDtypeStruct((m, d), F32), jax.ShapeDtypeStruct((m, LANES), F32)],
        compiler_params=_cparams(("parallel",)),
        name="router",
    )(x1, g.reshape(1, d), wr, br)


def _row_copy(src_hbm, row, dst, r, sem):
    return pltpu.make_async_copy(src_hbm.at[pl.ds(row, 1), :], dst.at[pl.ds(r, 1), :], sem)


def _expert_kernel(be_ref, na_ref, ng_ref, src_ref, x_hbm, wg_ref, wu_ref, wd_ref, o_ref, xbuf, sem):
    blk = pl.program_id(0)
    slot = blk % MOE_NBUF
    ahead = MOE_NBUF - 1

    def gather(b):
        s = b % MOE_NBUF

        def start(g, _):
            for i in range(MOE_RG):
                r = g * MOE_RG + i
                _row_copy(x_hbm, src_ref[b * MOE_TM + r], xbuf.at[s], r, sem.at[s]).start()
            return 0

        lax.fori_loop(0, ng_ref[b], start, 0)

    @pl.when(blk == 0)
    def _prime():
        xbuf[...] = jnp.zeros(xbuf.shape, F32)
        for b in range(ahead):
            @pl.when(b < na_ref[0])
            def _():
                gather(b)

    @pl.when(blk + ahead < na_ref[0])
    def _prefetch():
        gather(blk + ahead)

    @pl.when(blk < na_ref[0])
    def _active():
        def wait(g, _):
            for i in range(MOE_RG):
                _row_copy(x_hbm, 0, xbuf.at[slot], g * MOE_RG + i, sem.at[slot]).wait()
            return 0

        lax.fori_loop(0, ng_ref[blk], wait, 0)
        x = xbuf[slot].astype(BF16)
        hg = jnp.dot(x, wg_ref[0].astype(BF16), preferred_element_type=F32)
        hu = jnp.dot(x, wu_ref[0].astype(BF16), preferred_element_type=F32)
        h = (hg * _sigmoid(hg)) * hu
        o_ref[...] = jnp.dot(h.astype(BF16), wd_ref[0].astype(BF16), preferred_element_type=F32)

    @pl.when(blk >= na_ref[0])
    def _idle():
        o_ref[...] = jnp.zeros(o_ref.shape, F32)


def _experts(blk_exp, n_act, blk_groups, src, xn2, w_gate, w_up, w_down):
    npad = src.shape[0]
    d = xn2.shape[1]
    gs = pltpu.PrefetchScalarGridSpec(
        num_scalar_prefetch=4,
        grid=(npad // MOE_TM,),
        in_specs=[pl.BlockSpec(memory_space=pl.ANY),
                  pl.BlockSpec((1, d, D_EXPERT), lambda b, be, na, ng, sr: (be[b], 0, 0)),
                  pl.BlockSpec((1, d, D_EXPERT), lambda b, be, na, ng, sr: (be[b], 0, 0)),
                  pl.BlockSpec((1, D_EXPERT, d), lambda b, be, na, ng, sr: (be[b], 0, 0))],
        out_specs=pl.BlockSpec((MOE_TM, d), lambda b, be, na, ng, sr: (b, 0)),
        scratch_shapes=[pltpu.VMEM((MOE_NBUF, MOE_TM, d), F32), pltpu.SemaphoreType.DMA((MOE_NBUF,))])
    return pl.pallas_call(
        _expert_kernel, grid_spec=gs,
        out_shape=jax.ShapeDtypeStruct((npad, d), F32),
        compiler_params=_cparams(("arbitrary",)),
        name="experts",
    )(blk_exp, n_act, blk_groups, src, xn2, w_gate, w_up, w_down)


def _combine_kernel(pos_ref, ys_hbm, x1_ref, r_ref, g_ref, o_ref, buf, sem, *, TC, row0):
    i = pl.program_id(0)
    slot = i % 2

    def gather(step, s):
        base = (row0 + step * TC) * TOP_E

        def start(r, _):
            for e in range(TOP_E):
                _row_copy(ys_hbm, pos_ref[base + r * TOP_E + e], buf.at[s, e], r, sem.at[s]).start()
            return 0

        lax.fori_loop(0, TC, start, 0, unroll=8)

    @pl.when(i == 0)
    def _prime():
        gather(0, 0)

    @pl.when(i + 1 < pl.num_programs(0))
    def _prefetch():
        gather(i + 1, 1 - slot)

    def wait(r, _):
        for e in range(TOP_E):
            _row_copy(ys_hbm, 0, buf.at[slot, e], r, sem.at[slot]).wait()
        return 0

    lax.fori_loop(0, TC, wait, 0, unroll=8)
    gates = r_ref[:, TOP_E:2 * TOP_E]
    x = x1_ref[...]
    for e in range(TOP_E):
        x = x + gates[:, e:e + 1] * buf[slot, e]
    y = x * lax.rsqrt(jnp.mean(x * x, axis=-1, keepdims=True) + RMS_EPS)
    o_ref[...] = y * g_ref[...]


def _combine(pos, ys, x1, r, g, row0, n, TC):
    d = x1.shape[1]
    gs = pltpu.PrefetchScalarGridSpec(
        num_scalar_prefetch=1,
        grid=(n // TC,),
        in_specs=[pl.BlockSpec(memory_space=pl.ANY),
                  pl.BlockSpec((TC, d), lambda i, ps: (row0 // TC + i, 0)),
                  pl.BlockSpec((TC, LANES), lambda i, ps: (row0 // TC + i, 0)),
                  pl.BlockSpec((1, d), lambda i, ps: (0, 0))],
        out_specs=pl.BlockSpec((TC, d), lambda i, ps: (i, 0)),
        scratch_shapes=[pltpu.VMEM((2, TOP_E, TC, d), F32), pltpu.SemaphoreType.DMA((2,))])
    return pl.pallas_call(
        functools.partial(_combine_kernel, TC=TC, row0=row0), grid_spec=gs,
        out_shape=jax.ShapeDtypeStruct((n, d), F32),
        compiler_params=_cparams(("arbitrary",)),
        name="combine",
    )(pos, ys, x1, r, g.reshape(1, d))


def _route_tables(r, npad):
    nt = r.shape[0]
    ef = r[:, 0:TOP_E].astype(I32).reshape(-1)
    onehot = (ef[:, None] == jnp.arange(N_EXPERTS, dtype=I32)[None, :]).astype(I32)
    csum = jnp.cumsum(onehot, axis=0)
    rank = jnp.sum(onehot * csum, axis=1) - 1
    cnt = csum[-1]
    nblk = (cnt + MOE_TM - 1) // MOE_TM
    blk_end = jnp.cumsum(nblk)
    blk_start = blk_end - nblk
    pos = blk_start[ef] * MOE_TM + rank
    n_act = blk_end[-1:]
    ball = jnp.arange(npad // MOE_TM, dtype=I32)
    b = jnp.minimum(ball, n_act[0] - 1)
    blk_exp = jnp.minimum(jnp.sum((blk_end[None, :] <= b[:, None]).astype(I32), axis=1), N_EXPERTS - 1)
    rows = jnp.clip(cnt[blk_exp] - (ball - blk_start[blk_exp]) * MOE_TM, 0, MOE_TM)
    blk_groups = jnp.where(ball < n_act[0], (rows + MOE_RG - 1) // MOE_RG, 0).astype(I32)
    src = jnp.zeros((npad,), I32).at[pos].set(jnp.arange(nt * TOP_E, dtype=I32) // TOP_E)
    return blk_exp, n_act.astype(I32), blk_groups, src, pos.astype(I32)


def kernel(x_prompt, x_sample, cache_k, cache_v, cache_kidx, state_conv, state_C, state_n, state_m, page_table,
           g_attn, w_in, b_gates_m, conv_w, conv_b, m_norm_w, w_proj_m, w_proj_a, w_out, g_ffn,
           w_rg, b_rg, w_re, b_re, w_gate, w_up, w_down, g_final):
    assert x_prompt.shape[0] == 1 and g_attn.shape[0] == 1
    l = 0
    Tp = x_prompt.shape[1]
    Bs, Ts = x_sample.shape[:2]
    Ns = Bs * SMP_ROWS
    NT = Tp + Ns
    TM_BIG = 1408
    TM_ROW = 768
    assert NT % TM_BIG == 0 and NT % TM_ROW == 0 and Ts >= CONV_W - 1 and Ts <= SMP_ROWS
    P = page_table.shape[1] * PAGE_SIZE
    pad_rows = lambda a: jnp.pad(a, ((0, 0), (0, SMP_ROWS - Ts), (0, 0)))

    x_all = jnp.concatenate([x_prompt[0], pad_rows(x_sample).reshape(Ns, D_MODEL)], axis=0)
    xn = _rmsnorm(x_all, g_attn[l], BF16, TM_ROW)
    p32, p16 = _inproj(xn, _prep_w_in(w_in[l]), TM_BIG)

    gate_cols = slice(C_TAIL + T_IM, C_TAIL + T_IM + 2 * M_HEADS)
    ps8 = p32[Tp:].reshape(Bs, SMP_ROWS, D_CAT)
    ps3 = ps8[:, :Ts]

    zero = lambda *s: jnp.zeros(s, F32)
    hm_p, C_p, n_p, m_p = _mlstm(p32[None], p32[:Tp, gate_cols].T[None], zero(1, SUBLANES, 2 * M_WIDTH),
                                 conv_w[l], conv_b[l], b_gates_m[l], m_norm_w[l],
                                 zero(1, M_HEADS, M_DH, M_DH), zero(1, M_HEADS, M_DH), zero(1, M_HEADS),
                                 T=Tp, L=256, RIN=256, valid=256)
    grow_s = jnp.pad(jnp.swapaxes(ps3[:, :, gate_cols], 1, 2), ((0, 0), (0, 0), (0, LANES - Ts)))
    cb_s = jnp.pad(state_conv[l], ((0, 0), (SUBLANES - (CONV_W - 1), 0), (0, 0)))
    hm_s, C_s, n_s, m_s = _mlstm(ps8, grow_s, cb_s, conv_w[l], conv_b[l], b_gates_m[l], m_norm_w[l],
                                 state_C[l], state_n[l], state_m[l], T=SUBLANES, L=LANES, RIN=SUBLANES, valid=Ts)

    ha_p = _dsa_prompt(p32, p16, Tp, min(TOPK_MAX, Tp // 4))
    qs = ps3[:, :, C_QI:C_QI + IDX_HEADS * LANES].reshape(Bs, Ts, IDX_HEADS, LANES)[..., :IDX_DIM].reshape(Bs, Ts * IDX_HEADS, IDX_DIM)
    ws = ps3[:, :, C_TAIL + T_WI:C_TAIL + T_WI + IDX_HEADS].reshape(Bs, Ts * IDX_HEADS, 1)
    kin_t = jnp.pad(jnp.swapaxes(ps3[:, :, C_TAIL + T_KI:C_TAIL + T_KI + IDX_DIM], 1, 2), ((0, 0), (0, 0), (0, LANES - Ts)))
    sc = _smp_scores2(page_table, qs, ws, jnp.swapaxes(cache_kidx[l], 1, 2))
    mp, mn = _smp_select2(sc.reshape(Bs * SMP_ROWS, P), qs, ws, kin_t, Ts, min(TOPK_MAX, (P + Ts) // 4))
    rows_th = lambda c0: ps3[:, :, c0:c0 + A_WIDTH].reshape(Bs, Ts * A_HEADS, A_DH)
    ha_s = _smp_attn2(page_table, rows_th(C_QA), rows_th(C_KA), rows_th(C_VA), mn.reshape(Bs, SMP_ROWS, LANES),
                      mp.reshape(Bs, SMP_ROWS, P), cache_k[l], cache_v[l], Ts)

    hm_all = jnp.concatenate([hm_p[0], hm_s.reshape(Ns, M_WIDTH)], axis=0)
    ha_all = jnp.concatenate([ha_p, pad_rows(ha_s.reshape(Bs, Ts, A_WIDTH)).reshape(Ns, A_WIDTH).astype(BF16)], axis=0)
    merged = _merge(hm_all, ha_all, w_proj_m[l].astype(BF16), w_proj_a[l].astype(BF16), p32, TM_BIG)
    x1 = _outproj(merged, w_out[l].astype(BF16), x_all, TM_BIG)

    wr = jnp.pad(jnp.concatenate([w_rg[l], w_re[l]], axis=1), ((0, 0), (0, LANES - N_GROUPS - N_EXPERTS))).astype(BF16)
    br = jnp.pad(jnp.concatenate([b_rg[l], b_re[l]]), (0, LANES - N_GROUPS - N_EXPERTS)).reshape(1, LANES)
    xn2, r = _router(x1, g_ffn[l], wr, br, TM_ROW)
    npad = NT * TOP_E + N_EXPERTS * MOE_TM
    blk_exp, n_act, blk_groups, src, pos = _route_tables(r, npad)
    ys = _experts(blk_exp, n_act, blk_groups, src, xn2, w_gate[l], w_up[l], w_down[l])
    y_p = _combine(pos, ys, x1, r, g_final, 0, Tp, 256)
    y_s = _combine(pos, ys, x1, r, g_final, Tp, Ns, 256)

    st = lambda a, shape: a.reshape((1,) + shape)
    pp = p32[:Tp]
    return (y_p[None], y_s.reshape(Bs, SMP_ROWS, D_MODEL)[:, :Ts],
            st(pp[:, C_KA:C_KA + A_WIDTH], (1, Tp, A_HEADS, A_DH)), st(pp[:, C_VA:C_VA + A_WIDTH], (1, Tp, A_HEADS, A_DH)),
            st(pp[:, C_TAIL + T_KI:C_TAIL + T_KI + IDX_DIM], (1, Tp, IDX_DIM)),
            st(pp[Tp - (CONV_W - 1):, 0:2 * M_WIDTH], (1, CONV_W - 1, 2 * M_WIDTH)),
            st(C_p, (1, M_HEADS, M_DH, M_DH)), st(n_p, (1, M_HEADS, M_DH)), st(m_p[:, :, 0, 0], (1, M_HEADS)),
            st(ps3[:, :, C_KA:C_KA + A_WIDTH], (Bs, Ts, A_HEADS, A_DH)), st(ps3[:, :, C_VA:C_VA + A_WIDTH], (Bs, Ts, A_HEADS, A_DH)),
            st(ps3[:, :, C_TAIL + T_KI:C_TAIL + T_KI + IDX_DIM], (Bs, Ts, IDX_DIM)),
            st(ps3[:, Ts - (CONV_W - 1):, 0:2 * M_WIDTH], (Bs, CONV_W - 1, 2 * M_WIDTH)),
            st(C_s, (Bs, M_HEADS, M_DH, M_DH)), st(n_s, (Bs, M_HEADS, M_DH)), st(m_s[:, :, 0, 0], (Bs, M_HEADS)))
```

```python
import functools

import jax
import jax.numpy as jnp
import numpy as np
from jax import lax
from jax.experimental import pallas as pl
from jax.experimental.pallas import tpu as pltpu

F32 = jnp.float32
BF16 = jnp.bfloat16
I32 = jnp.int32

D_MODEL = 2048
M_WIDTH = D_MODEL // 2
M_HEADS = 4
M_DH = M_WIDTH // M_HEADS
CONV_W = 4
A_WIDTH = D_MODEL // 2
A_DH = 128
A_HEADS = A_WIDTH // A_DH
IDX_HEADS = 8
IDX_DIM = 64
TOPK_MAX = 256
PAGE_SIZE = 128
N_GROUPS = 4
EXP_PER_GROUP = 8
N_EXPERTS = N_GROUPS * EXP_PER_GROUP
TOP_E = 2
D_EXPERT = D_MODEL // 4
RMS_EPS = 1e-6
IN_SIZES = (M_WIDTH, M_WIDTH, M_WIDTH, M_WIDTH, M_HEADS, M_HEADS, A_WIDTH, A_WIDTH, A_WIDTH,
            IDX_HEADS * IDX_DIM, IDX_DIM, IDX_HEADS, D_MODEL, D_MODEL)
IN_SPLITS = tuple(int(s) for s in np.cumsum(IN_SIZES)[:-1])

LANES = 128
SUBLANES = 8
VMEM_LIMIT = 56 * 1024 * 1024

C_QM, C_KM, C_VM, C_OM = 0, 1024, 2048, 3072
C_QA, C_KA, C_VA = 4096, 5120, 6144
C_GM, C_GA = 7168, 9216
C_QI = 11264
C_TAIL = 12288
T_KI, T_WI, T_IM, T_FM = 0, 64, 72, 76
D_CAT = 12800
PROJ_TN = 512

NEG = -1e30
INT_MIN = -2 ** 31


def _cparams(sem):
    return pltpu.CompilerParams(dimension_semantics=sem, vmem_limit_bytes=VMEM_LIMIT)


def _rms_kernel(x_ref, g_ref, o_ref):
    x = x_ref[...]
    y = x * lax.rsqrt(jnp.mean(x * x, axis=-1, keepdims=True) + RMS_EPS)
    o_ref[...] = (y * g_ref[...]).astype(o_ref.dtype)


def _rmsnorm(x, g, out_dtype, tm):
    m, d = x.shape
    return pl.pallas_call(
        _rms_kernel,
        grid=(m // tm,),
        in_specs=[pl.BlockSpec((tm, d), lambda i: (i, 0)), pl.BlockSpec((1, d), lambda i: (0, 0))],
        out_specs=pl.BlockSpec((tm, d), lambda i: (i, 0)),
        out_shape=jax.ShapeDtypeStruct((m, d), out_dtype),
        compiler_params=_cparams(("parallel",)),
        name="rmsnorm",
    )(x, g.reshape(1, d))


def _inproj_kernel(x_ref, wt_ref, o32_ref, o16_ref):
    acc = lax.dot_general(x_ref[...], wt_ref[...], (((1,), (1,)), ((), ())), preferred_element_type=F32)
    o32_ref[...] = acc
    o16_ref[...] = acc.astype(BF16)


def _inproj(xn, w_cat_t, tm):
    m, d = xn.shape
    n = w_cat_t.shape[0]
    tn = PROJ_TN
    return pl.pallas_call(
        _inproj_kernel,
        grid=(m // tm, n // tn),
        in_specs=[pl.BlockSpec((tm, d), lambda i, j: (i, 0)), pl.BlockSpec((tn, d), lambda i, j: (j, 0))],
        out_specs=[pl.BlockSpec((tm, tn), lambda i, j: (i, j)), pl.BlockSpec((tm, tn), lambda i, j: (i, j))],
        out_shape=[jax.ShapeDtypeStruct((m, n), F32), jax.ShapeDtypeStruct((m, n), BF16)],
        compiler_params=_cparams(("parallel", "parallel")),
        name="inproj",
    )(xn, w_cat_t)


def _prep_w_in(w_in):
    (q_m, k_m, v_m, o_m, i_m, f_m, q_a, k_a, v_a, q_i, k_i, w_i, g_m, g_a) = jnp.split(w_in.T, IN_SPLITS, axis=0)
    d = w_in.shape[0]
    q_i = jnp.pad(q_i.reshape(IDX_HEADS, IDX_DIM, d), ((0, 0), (0, LANES - IDX_DIM), (0, 0))).reshape(IDX_HEADS * LANES, d)
    rows = [q_m, k_m, v_m, o_m, q_a, k_a, v_a, g_m, g_a, q_i, k_i, w_i, i_m, f_m]
    rows.append(jnp.zeros((D_CAT - sum(a.shape[0] for a in rows), d), w_in.dtype))
    return jnp.concatenate(rows, axis=0).astype(BF16)


def _sigmoid(x):
    return 1.0 / (1.0 + jnp.exp(-x))


def _log_sigmoid(x):
    return jnp.minimum(x, 0.0) - jnp.log1p(jnp.exp(-jnp.abs(x)))


def _mlstm_kernel(q_ref, k_ref, v_ref, o_ref, tail_ref, grow_ref, cb_ref, convw_ref, convb_ref, bl_ref, bs_ref,
                  nw_ref, c0_ref, n0_ref, m0_ref,
                  h_ref, cout_ref, nout_ref, mout_ref,
                  xq_s, xk_s, c_s, n_s, m_s, vp_s, op_s, tp_s, *, L, RIN, valid):
    c = pl.program_id(1)
    nc = pl.num_programs(1)

    @pl.when(c == 0)
    def _init():
        xq_s[0:SUBLANES, :] = cb_ref[0, :, 0:M_WIDTH]
        xk_s[0:SUBLANES, :] = cb_ref[0, :, M_WIDTH:2 * M_WIDTH]
        c_s[...] = c0_ref[0]
        n_s[...] = n0_ref[0]
        m_s[...] = m0_ref[0]

    if RIN < L:
        zpad = jnp.zeros((L - RIN, M_WIDTH), F32)
        xq_s[SUBLANES + RIN:SUBLANES + L, :] = zpad
        xk_s[SUBLANES + RIN:SUBLANES + L, :] = zpad
        vp_s[RIN:L, :] = zpad
        op_s[RIN:L, :] = zpad
        tp_s[RIN:L, :] = jnp.zeros((L - RIN, LANES), F32)
    xq_s[SUBLANES:SUBLANES + RIN, :] = q_ref[0]
    xk_s[SUBLANES:SUBLANES + RIN, :] = k_ref[0]
    vp_s[0:RIN, :] = v_ref[0]
    op_s[0:RIN, :] = o_ref[0]
    tp_s[0:RIN, :] = tail_ref[0]

    def conv(xs, col0):
        w = convw_ref[:, col0:col0 + M_WIDTH]
        y = convb_ref[:, col0:col0 + M_WIDTH]
        for j in range(CONV_W):
            r0 = SUBLANES - (CONV_W - 1) + j
            y = y + xs[r0:r0 + L, :] * w[j:j + 1, :]
        return y * _sigmoid(y)

    q_all = conv(xq_s, 0) * (M_DH ** -0.5)
    k_all = conv(xk_s, M_WIDTH)
    v_all = vp_s[...]
    tail = tp_s[...] + bl_ref[...]
    grow = grow_ref[0] + bs_ref[...]

    xq_s[0:SUBLANES, :] = xq_s[L:L + SUBLANES, :]
    xk_s[0:SUBLANES, :] = xk_s[L:L + SUBLANES, :]

    tt = lax.broadcasted_iota(I32, (L, L), 0)
    ss = lax.broadcasted_iota(I32, (L, L), 1)
    causal = ss <= tt
    row_ok = lax.broadcasted_iota(I32, (L, 1), 0) < valid
    col_ok = lax.broadcasted_iota(I32, (1, L), 1) < valid

    for h in range(M_HEADS):
        hs = slice(h * M_DH, (h + 1) * M_DH)
        qh = q_all[:, hs]
        kh = k_all[:, hs]
        vh = v_all[:, hs]
        ig_c = jnp.where(row_ok, tail[:, T_IM + h:T_IM + h + 1], NEG)
        lf_c = jnp.where(row_ok, _log_sigmoid(tail[:, T_FM + h:T_FM + h + 1]), 0.0)
        ig_r = jnp.where(col_ok, grow[h:h + 1, :], NEG)
        lf_r = jnp.where(col_ok, _log_sigmoid(grow[M_HEADS + h:M_HEADS + h + 1, :]), 0.0)
        b_c = jnp.sum(jnp.where(causal, lf_r, 0.0), axis=1, keepdims=True)
        b_r = jnp.sum(jnp.where(tt <= ss, lf_c, 0.0), axis=0, keepdims=True)
        dmat = jnp.where(causal, b_c - b_r + ig_r, NEG)
        m_prev = m_s[h][:, 0:1]
        m_t = jnp.maximum(b_c + m_prev, jnp.max(dmat, axis=1, keepdims=True))
        e = jnp.exp(dmat - m_t)
        qb = qh.astype(BF16)
        kb = kh.astype(BF16)
        s = lax.dot_general(qb, kb, (((1,), (1,)), ((), ())), preferred_element_type=F32) * e
        inter = jnp.exp(b_c + m_prev - m_t)
        ch = c_s[h]
        num = jnp.dot(s.astype(BF16), vh.astype(BF16), preferred_element_type=F32) + inter * lax.dot_general(
            qb, ch.astype(BF16), (((1,), (1,)), ((), ())), preferred_element_type=F32)
        nh = n_s[h]
        den = jnp.sum(s, axis=1, keepdims=True) + inter * jnp.sum(qh * nh, axis=1, keepdims=True)
        hh = num / jnp.maximum(jnp.abs(den), jnp.exp(-m_t))
        hh = hh * lax.rsqrt(jnp.mean(hh * hh, axis=1, keepdims=True) + RMS_EPS)
        out = _sigmoid(op_s[:, hs]) * (hh * nw_ref[:, hs])
        h_ref[0, :, hs] = out[0:RIN, :].astype(h_ref.dtype)
        m_new = m_t[L - 1:L, :]
        b_last = b_c[L - 1:L, :]
        w_c = jnp.exp(b_last - b_c + ig_c - m_new)
        decay = jnp.exp(b_last + m_prev - m_new)
        upd = lax.dot_general((w_c * vh).astype(BF16), kb, (((0,), (0,)), ((), ())), preferred_element_type=F32)
        c_s[h] = decay * ch + upd
        n_s[h] = decay * nh + jnp.sum(w_c * kh, axis=0, keepdims=True)
        m_s[h] = jnp.broadcast_to(m_new, (1, LANES))

    @pl.when(c == nc - 1)
    def _fin():
        cout_ref[0] = c_s[...]
        nout_ref[0] = n_s[...]
        mout_ref[0] = m_s[...]


def _mlstm(p32, grow, convbuf, conv_w, conv_b, b_gates, m_norm_w, c0, n0, m0, *, T, L, RIN, valid):
    B = p32.shape[0]
    nc = T // RIN
    bl = jnp.zeros((1, LANES), F32).at[0, T_IM:T_IM + 2 * M_HEADS].set(b_gates)
    bs = jnp.broadcast_to(b_gates[:, None], (2 * M_HEADS, L))
    kern = functools.partial(_mlstm_kernel, L=L, RIN=RIN, valid=valid)
    cblk = lambda col: pl.BlockSpec((1, RIN, M_WIDTH), lambda b, c, col=col: (b, c, col // M_WIDTH))
    const2 = lambda shape: pl.BlockSpec(shape, lambda b, c: (0, 0))
    per_b = lambda shape: pl.BlockSpec(shape, lambda b, c: (b,) + (0,) * (len(shape) - 1))
    return pl.pallas_call(
        kern,
        grid=(B, nc),
        in_specs=[cblk(C_QM), cblk(C_KM), cblk(C_VM), cblk(C_OM),
                  pl.BlockSpec((1, RIN, LANES), lambda b, c: (b, c, C_TAIL // LANES)),
                  pl.BlockSpec((1, 2 * M_HEADS, L), lambda b, c: (b, 0, c)),
                  per_b((1, SUBLANES, 2 * M_WIDTH)),
                  const2((CONV_W, 2 * M_WIDTH)), const2((1, 2 * M_WIDTH)), const2((1, LANES)),
                  const2((2 * M_HEADS, L)), const2((1, M_WIDTH)),
                  per_b((1, M_HEADS, M_DH, M_DH)), per_b((1, M_HEADS, 1, M_DH)), per_b((1, M_HEADS, 1, LANES))],
        out_specs=[pl.BlockSpec((1, RIN, M_WIDTH), lambda b, c: (b, c, 0)),
                   per_b((1, M_HEADS, M_DH, M_DH)), per_b((1, M_HEADS, 1, M_DH)), per_b((1, M_HEADS, 1, LANES))],
        out_shape=[jax.ShapeDtypeStruct((B, T, M_WIDTH), BF16),
                   jax.ShapeDtypeStruct((B, M_HEADS, M_DH, M_DH), F32),
                   jax.ShapeDtypeStruct((B, M_HEADS, 1, M_DH), F32),
                   jax.ShapeDtypeStruct((B, M_HEADS, 1, LANES), F32)],
        scratch_shapes=[pltpu.VMEM((SUBLANES + L, M_WIDTH), F32), pltpu.VMEM((SUBLANES + L, M_WIDTH), F32),
                        pltpu.VMEM((M_HEADS, M_DH, M_DH), F32), pltpu.VMEM((M_HEADS, 1, M_DH), F32),
                        pltpu.VMEM((M_HEADS, 1, LANES), F32),
                        pltpu.VMEM((L, M_WIDTH), F32), pltpu.VMEM((L, M_WIDTH), F32), pltpu.VMEM((L, LANES), F32)],
        compiler_params=_cparams(("parallel", "arbitrary")),
        name="mlstm",
    )(p32, p32, p32, p32, p32, grow, convbuf, conv_w, conv_b.reshape(1, -1), bl, bs, m_norm_w.reshape(1, -1),
      c0, n0.reshape(B, M_HEADS, 1, M_DH), jnp.broadcast_to(m0[:, :, None, None], (B, M_HEADS, 1, LANES)))


def _score_key(sc):
    bits = lax.bitcast_convert_type(sc, I32)
    return jnp.where(bits < 0, INT_MIN - bits, bits)


DSA_TQ = 512
DSA_TK = 512
DSA_RG = 64
DSA_RB = 512
LOG2E = 1.4426950408889634


def _dsa_prompt_kernel(qi_tab, kj_tab, qidx_ref, tail_ref, kit_ref, qa_ref, ka_ref, va_ref, o_ref,
                       keys_s, cand_s, cnt_s, thr_s, thrm_s, cut_s, bias_s, tie_s, wrep_s, m_s, l_s, acc_s,
                       *, TQ, TK, topk, pos_bits):
    step = pl.program_id(0)
    qi = qi_tab[step]
    kj = kj_tab[step]
    RG = DSA_RG
    nlc = TK // LANES

    def count_pass(pred):
        def rbody(r, _):
            r0 = pl.multiple_of(r * RG, RG)
            cand = cand_s[pl.ds(r0, RG), :]
            aux = thr_s[pl.ds(r0, RG), :]

            def block(j, cnt, inc):
                for c in range(nlc):
                    blk = keys_s[j, pl.ds(r0, RG), c * LANES:(c + 1) * LANES]
                    pos = j * TK + c * LANES + lax.broadcasted_iota(I32, (RG, LANES), 1)
                    cnt = jnp.where(pred(blk, cand, aux, pos), cnt + inc, cnt)
                return cnt

            def pair(jj, cnt):
                return block(2 * jj + 1, block(2 * jj, cnt, 1.0), 1.0)

            nkb = qi + 1
            cnt = lax.fori_loop(0, nkb // 2, pair, jnp.zeros((RG, LANES), F32))
            cnt_s[pl.ds(r0, RG), :] = block(qi, cnt, (nkb % 2).astype(F32))
            return 0

        lax.fori_loop(0, TQ // RG, rbody, 0)
        return jnp.sum(cnt_s[...], axis=1, keepdims=True)

    @pl.when(kj == 0)
    def _phase1():
        w = tail_ref[:, T_WI:T_WI + IDX_HEADS] * (IDX_HEADS ** -0.5) * (IDX_DIM ** -0.5)
        for h in range(IDX_HEADS):
            wrep_s[h] = jnp.broadcast_to(w[:, h:h + 1], (TQ, LANES))
        rowpos = qi * TQ + lax.broadcasted_iota(I32, (TQ, TK), 0)

        def kbody(j, _):
            kt = kit_ref[j].astype(BF16)
            sc = jnp.zeros((TQ, TK), F32)
            for h in range(IDX_HEADS):
                qh = qidx_ref[:, h * LANES:(h + 1) * LANES].astype(BF16)
                s = jnp.dot(qh, kt, preferred_element_type=F32)
                sc = sc + jnp.maximum(s, 0.0) * jnp.concatenate([wrep_s[h]] * nlc, axis=1)
            colpos = j * TK + lax.broadcasted_iota(I32, (TQ, TK), 1)
            keys_s[j] = jnp.where(colpos <= rowpos, _score_key(sc), INT_MIN)
            return 0

        lax.fori_loop(0, qi + 1, kbody, 0)

        ge = lambda blk, cand, aux, pos: blk >= cand
        cand_s[...] = jnp.zeros((TQ, LANES), I32)
        cnt = count_pass(ge)
        lo = jnp.where(cnt >= topk, 0, INT_MIN).astype(I32)

        def bit_body(b, carry):
            lo, n_lo = carry
            cand = lo | lax.shift_left(jnp.int32(1), 30 - b)
            cand_s[...] = jnp.broadcast_to(cand, (TQ, LANES))
            cnt = count_pass(ge)
            take = cnt >= topk
            return jnp.where(take, cand, lo), jnp.where(take, cnt, n_lo)

        thr, n_ge = lax.fori_loop(0, 31, bit_body, (lo, cnt))
        thr_s[...] = jnp.broadcast_to(thr, (TQ, LANES))
        short = thr == INT_MIN
        thrm_s[...] = jnp.broadcast_to(jnp.where(short, INT_MIN, thr - 1), (TQ, LANES))
        cut_s[...] = jnp.broadcast_to(jnp.where(short, -1, 2 ** 30).astype(I32), (TQ, LANES))
        surplus = jnp.max(jnp.where((n_ge > topk) & jnp.logical_not(short), 1.0, 0.0))
        tie_s[0] = (surplus > 0.0).astype(I32)

        @pl.when(surplus > 0.0)
        def _ties():
            cand_s[...] = thr_s[...]
            need = topk - count_pass(lambda blk, cand, aux, pos: blk > cand)
            eq_below = lambda blk, cand, aux, pos: (blk == aux) & (pos < cand)

            def tie_body(b, x):
                cand = x + lax.shift_left(jnp.int32(1), pos_bits - 1 - b)
                cand_s[...] = jnp.broadcast_to(cand, (TQ, LANES))
                cnt = count_pass(eq_below)
                return jnp.where(cnt < need, cand, x)

            x = lax.fori_loop(0, pos_bits, tie_body, jnp.zeros((TQ, 1), I32))
            cut_s[...] = jnp.broadcast_to(jnp.where(short, -1, x), (TQ, LANES))

        m_s[...] = jnp.full(m_s.shape, NEG, F32)
        l_s[...] = jnp.zeros(l_s.shape, F32)
        acc_s[...] = jnp.zeros(acc_s.shape, F32)

    tile_l = lambda a: jnp.concatenate([a] * nlc, axis=1)

    @pl.when(tie_s[0] == 0)
    def _bias_plain():
        bias_s[...] = jnp.where(keys_s[kj] > tile_l(thrm_s[...]), 0.0, NEG)

    @pl.when(tie_s[0] != 0)
    def _bias_tied():
        key = keys_s[kj]
        thr = tile_l(thr_s[...])
        colpos = kj * TK + lax.broadcasted_iota(I32, (TQ, TK), 1)
        sel = (key > thr) | ((key == thr) & (colpos <= tile_l(cut_s[...])))
        bias_s[...] = jnp.where(sel, 0.0, NEG)

    RB = DSA_RB
    c1 = (A_DH ** -0.5) * LOG2E

    def rb_body(rb, _):
        rows = pl.ds(pl.multiple_of(rb * RB, RB), RB)
        bias = bias_s[rows, :]
        hsl = [slice(h * A_DH, (h + 1) * A_DH) for h in range(A_HEADS)]
        m_old = [m_s[h, rows, :] for h in range(A_HEADS)]
        l_old = [l_s[h, rows, :] for h in range(A_HEADS)]
        a_old = [acc_s[rows, hs] for hs in hsl]
        ts = [lax.dot_general(qa_ref[rows, hs], ka_ref[:, hs], (((1,), (1,)), ((), ())), preferred_element_type=F32) * c1 + bias
              for hs in hsl]
        m_new = [jnp.maximum(m_old[h], jnp.max(ts[h], axis=1, keepdims=True)) for h in range(A_HEADS)]
        l_new, a_new = [], []
        for h in range(A_HEADS):
            alpha = jnp.exp2(m_old[h] - m_new[h])
            p = jnp.exp2(ts[h] - tile_l(m_new[h]))
            psum = p[:, 0:LANES]
            for c in range(1, nlc):
                psum = psum + p[:, c * LANES:(c + 1) * LANES]
            l_new.append(alpha * l_old[h] + psum)
            a_new.append(alpha * a_old[h] + jnp.dot(p.astype(BF16), va_ref[:, hsl[h]], preferred_element_type=F32))
        for h in range(A_HEADS):
            m_s[h, rows, :] = m_new[h]
            l_s[h, rows, :] = l_new[h]
            acc_s[rows, hsl[h]] = a_new[h]
        return 0

    lax.fori_loop(0, TQ // RB, rb_body, 0)

    @pl.when(kj == qi)
    def _fin():
        for h in range(A_HEADS):
            hs = slice(h * A_DH, (h + 1) * A_DH)
            o_ref[:, hs] = (acc_s[:, hs] / jnp.sum(l_s[h], axis=1, keepdims=True)).astype(o_ref.dtype)


def _dsa_prompt(p32, p16, T, topk):
    TQ, TK = DSA_TQ, DSA_TK
    nq = T // TQ
    assert TQ == TK
    qi_tab = np.concatenate([np.full(i + 1, i) for i in range(nq)]).astype(np.int32)
    kj_tab = np.concatenate([np.arange(i + 1) for i in range(nq)]).astype(np.int32)
    ki = p32[:T, C_TAIL + T_KI:C_TAIL + T_KI + IDX_DIM]
    kit = jnp.pad(ki.T, ((0, LANES - IDX_DIM), (0, 0))).reshape(LANES, T // TK, TK).transpose(1, 0, 2)
    kern = functools.partial(_dsa_prompt_kernel, TQ=TQ, TK=TK, topk=topk, pos_bits=int(T - 1).bit_length())
    gs = pltpu.PrefetchScalarGridSpec(
        num_scalar_prefetch=2,
        grid=(len(qi_tab),),
        in_specs=[pl.BlockSpec((TQ, IDX_HEADS * LANES), lambda s, qt, kt: (qt[s], C_QI // (IDX_HEADS * LANES))),
                  pl.BlockSpec((TQ, LANES), lambda s, qt, kt: (qt[s], C_TAIL // LANES)),
                  pl.BlockSpec((T // TK, LANES, TK), lambda s, qt, kt: (0, 0, 0)),
                  pl.BlockSpec((TQ, A_WIDTH), lambda s, qt, kt: (qt[s], C_QA // A_WIDTH)),
                  pl.BlockSpec((TK, A_WIDTH), lambda s, qt, kt: (kt[s], C_KA // A_WIDTH)),
                  pl.BlockSpec((TK, A_WIDTH), lambda s, qt, kt: (kt[s], C_VA // A_WIDTH))],
        out_specs=pl.BlockSpec((TQ, A_WIDTH), lambda s, qt, kt: (qt[s], 0)),
        scratch_shapes=[pltpu.VMEM((T // TK, TQ, TK), I32),
                        pltpu.VMEM((TQ, LANES), I32), pltpu.VMEM((TQ, LANES), F32), pltpu.VMEM((TQ, LANES), I32),
                        pltpu.VMEM((TQ, LANES), I32), pltpu.VMEM((TQ, LANES), I32),
                        pltpu.VMEM((TQ, TK), F32), pltpu.SMEM((1,), I32), pltpu.VMEM((IDX_HEADS, TQ, LANES), F32),
                        pltpu.VMEM((A_HEADS, TQ, LANES), F32), pltpu.VMEM((A_HEADS, TQ, LANES), F32),
                        pltpu.VMEM((TQ, A_WIDTH), F32)])
    return pl.pallas_call(
        kern, grid_spec=gs,
        out_shape=jax.ShapeDtypeStruct((T, A_WIDTH), BF16),
        compiler_params=_cparams(("arbitrary",)),
        name="dsa_prompt",
    )(jnp.asarray(qi_tab), jnp.asarray(kj_tab), p32, p32, kit, p16, p16, p16)


SMP_PGS = 64
SMP_PG = 16
SMP_ROWS = SUBLANES
SMP_SEL_B = 16


def _idx_scores_t(q, w, kt):
    s = jnp.dot(q.astype(BF16), kt.astype(BF16), preferred_element_type=F32)
    s = jnp.maximum(s * (IDX_DIM ** -0.5), 0.0) * (w * (IDX_HEADS ** -0.5))
    n_tok = q.shape[0] // IDX_HEADS
    rows = [jnp.sum(s[t * IDX_HEADS:(t + 1) * IDX_HEADS], axis=0, keepdims=True) for t in range(n_tok)]
    rows.append(jnp.zeros((SMP_ROWS - n_tok, s.shape[1]), F32))
    return jnp.concatenate(rows, axis=0)


def _smp_scores2_kernel(pt_ref, q_ref, w_ref, *refs):
    o_ref = refs[-1]
    kt = jnp.concatenate([r[0] for r in refs[:-1]], axis=1)
    o_ref[0] = _idx_scores_t(q_ref[0], w_ref[0], kt)


def _smp_scores2(page_table, qs, ws, kidx_t):
    B, n_pages = page_table.shape
    R = qs.shape[1]
    page = lambda i: pl.BlockSpec((1, IDX_DIM, PAGE_SIZE), lambda b, p, pt: (pt[b, p * SMP_PGS + i], 0, 0))
    gs = pltpu.PrefetchScalarGridSpec(
        num_scalar_prefetch=1,
        grid=(B, n_pages // SMP_PGS),
        in_specs=[pl.BlockSpec((1, R, IDX_DIM), lambda b, p, pt: (b, 0, 0)),
                  pl.BlockSpec((1, R, 1), lambda b, p, pt: (b, 0, 0))] + [page(i) for i in range(SMP_PGS)],
        out_specs=pl.BlockSpec((1, SMP_ROWS, SMP_PGS * PAGE_SIZE), lambda b, p, pt: (b, 0, p)))
    return pl.pallas_call(
        _smp_scores2_kernel, grid_spec=gs,
        out_shape=jax.ShapeDtypeStruct((B, SMP_ROWS, n_pages * PAGE_SIZE), F32),
        compiler_params=_cparams(("parallel", "arbitrary")),
        name="smp_scores",
    )(page_table, qs, ws, *([kidx_t] * SMP_PGS))


def _smp_select2_kernel(sc_ref, q_ref, w_ref, kint_ref, mp_ref, mn_ref, keys_s, cand_s, thr_s, cut_s, cnt_s,
                        *, n_tok, topk):
    NB = q_ref.shape[0]
    R = NB * SMP_ROWS
    P = sc_ref.shape[1]
    NCH = P // LANES
    RG = DSA_RG
    lane = lax.broadcasted_iota(I32, (R, LANES), 1)
    trow = lax.broadcasted_iota(I32, (R, LANES), 0) % SMP_ROWS
    for c in range(NCH):
        keys_s[c] = _score_key(sc_ref[:, c * LANES:(c + 1) * LANES])
    s_new = jnp.concatenate([_idx_scores_t(q_ref[b], w_ref[b], kint_ref[b]) for b in range(NB)], axis=0)
    keys_s[NCH] = jnp.where(lane <= trow, _score_key(s_new), INT_MIN)

    def count_pass(pred):
        def rbody(r, _):
            rows = pl.ds(pl.multiple_of(r * RG, RG), RG)
            cand = cand_s[rows, :]
            aux = thr_s[rows, :]

            def cb(c, cnt):
                pos = c * LANES + lax.broadcasted_iota(I32, (RG, LANES), 1)
                return cnt + jnp.where(pred(keys_s[c, rows, :], cand, aux, pos), 1.0, 0.0)

            cnt_s[rows, :] = lax.fori_loop(0, NCH + 1, cb, jnp.zeros((RG, LANES), F32))
            return 0

        lax.fori_loop(0, R // RG, rbody, 0)
        return jnp.sum(cnt_s[...], axis=1, keepdims=True)

    ge = lambda blk, cand, aux, pos: blk >= cand
    cand_s[...] = jnp.zeros((R, LANES), I32)
    cnt0 = count_pass(ge)
    lo = jnp.where(cnt0 >= topk, 0, INT_MIN).astype(I32)

    def bit_body(b, carry):
        lo, n_lo = carry
        cand = lo | lax.shift_left(jnp.int32(1), 30 - b)
        cand_s[...] = jnp.broadcast_to(cand, (R, LANES))
        cnt = count_pass(ge)
        take = cnt >= topk
        return jnp.where(take, cand, lo), jnp.where(take, cnt, n_lo)

    thr, n_ge = lax.fori_loop(0, 31, bit_body, (lo, cnt0))
    thr_s[...] = jnp.broadcast_to(thr, (R, LANES))
    cut_s[...] = jnp.full((R, LANES), 2 ** 30, I32)
    real_row = lax.broadcasted_iota(I32, (R, 1), 0) % SMP_ROWS < n_tok
    surplus = jnp.max(jnp.where((n_ge > topk) & (thr != INT_MIN) & real_row, 1.0, 0.0))

    @pl.when(surplus > 0.0)
    def _ties():
        cand_s[...] = thr_s[...]
        need = topk - count_pass(lambda blk, cand, aux, pos: blk > cand)
        pos_bits = int(P + LANES - 1).bit_length()
        eq_below = lambda blk, cand, aux, pos: (blk == aux) & (pos < cand)

        def tie_body(b, x):
            cand = x + lax.shift_left(jnp.int32(1), pos_bits - 1 - b)
            cand_s[...] = jnp.broadcast_to(cand, (R, LANES))
            return jnp.where(count_pass(eq_below) < need, cand, x)

        cut_s[...] = jnp.broadcast_to(lax.fori_loop(0, pos_bits, tie_body, jnp.zeros((R, 1), I32)), (R, LANES))

    cut = cut_s[...]
    thr_b = thr_s[...]
    row_ok = trow < n_tok
    for c in range(NCH + 1):
        key = keys_s[c]
        sel = ((key > thr_b) | ((key == thr_b) & (c * LANES + lane <= cut))) & (key != INT_MIN) & row_ok
        if c < NCH:
            mp_ref[:, c * LANES:(c + 1) * LANES] = jnp.where(sel, 1.0, 0.0)
        else:
            mn_ref[...] = jnp.where(sel, 1.0, 0.0)


def _smp_select2(sc, qs, ws, kin_t, n_tok, topk):
    R, P = sc.shape
    B = qs.shape[0]
    NB = SMP_SEL_B
    RS = NB * SMP_ROWS
    Rq = qs.shape[1]
    return pl.pallas_call(
        functools.partial(_smp_select2_kernel, n_tok=n_tok, topk=topk),
        grid=(B // NB,),
        in_specs=[pl.BlockSpec((RS, P), lambda i: (i, 0)),
                  pl.BlockSpec((NB, Rq, IDX_DIM), lambda i: (i, 0, 0)), pl.BlockSpec((NB, Rq, 1), lambda i: (i, 0, 0)),
                  pl.BlockSpec((NB, IDX_DIM, LANES), lambda i: (i, 0, 0))],
        out_specs=[pl.BlockSpec((RS, P), lambda i: (i, 0)), pl.BlockSpec((RS, LANES), lambda i: (i, 0))],
        out_shape=[jax.ShapeDtypeStruct((R, P), F32), jax.ShapeDtypeStruct((R, LANES), F32)],
        scratch_shapes=[pltpu.VMEM((P // LANES + 1, RS, LANES), I32), pltpu.VMEM((RS, LANES), I32),
                        pltpu.VMEM((RS, LANES), I32), pltpu.VMEM((RS, LANES), I32), pltpu.VMEM((RS, LANES), F32)],
        compiler_params=_cparams(("parallel",)),
        name="smp_select",
    )(sc, qs, ws, kin_t)


def _smp_attn2_kernel(pt_ref, q_ref, kn_ref, vn_ref, mn_ref, mp_ref, *refs, n_tok):
    k_refs = refs[:SMP_PG]
    v_refs = refs[SMP_PG:2 * SMP_PG]
    o_ref, kpad_s, vpad_s, m_s, l_s, acc_s = refs[2 * SMP_PG:]
    p = pl.program_id(1)
    NL = PAGE_SIZE * A_HEADS
    R = n_tok * A_HEADS
    c1 = (A_DH ** -0.5) * LOG2E
    diag = jnp.where(lax.broadcasted_iota(I32, (A_HEADS, NL), 1) % A_HEADS == lax.broadcasted_iota(I32, (A_HEADS, NL), 0), 1.0, 0.0)
    expand = jnp.where(lax.broadcasted_iota(I32, (PAGE_SIZE, NL), 1) // A_HEADS == lax.broadcasted_iota(I32, (PAGE_SIZE, NL), 0),
                       1.0, 0.0).astype(BF16)
    qb = q_ref[0].astype(BF16)

    def attend(k_list, v_list, masks):
        n = len(k_list)
        x = jnp.dot(jnp.concatenate(masks, axis=0).astype(BF16), expand, preferred_element_type=F32)
        ts = []
        for i in range(n):
            s = lax.dot_general(qb, k_list[i], (((1,), (1,)), ((), ())), preferred_element_type=F32)
            ok = jnp.concatenate([jnp.broadcast_to(x[i * SMP_ROWS + t:i * SMP_ROWS + t + 1, :], (A_HEADS, NL)) * diag
                                  for t in range(n_tok)], axis=0)
            ts.append(s * c1 + jnp.where(ok > 0.5, 0.0, NEG))
        m_old = m_s[...]
        m_new = jnp.maximum(m_old, jnp.max(jnp.concatenate(ts, axis=1), axis=1, keepdims=True))
        alpha = jnp.exp2(m_old - m_new)
        acc = alpha * acc_s[...]
        lsum = alpha * l_s[...]
        m_t = jnp.concatenate([m_new] * (NL // LANES), axis=1)
        for i in range(n):
            pr = jnp.exp2(ts[i] - m_t)
            for c in range(NL // LANES):
                lsum = lsum + pr[:, c * LANES:(c + 1) * LANES]
            acc = acc + jnp.dot(pr.astype(BF16), v_list[i], preferred_element_type=F32)
        m_s[...] = m_new
        l_s[...] = lsum
        acc_s[...] = acc

    @pl.when(p == 0)
    def _first():
        m_s[...] = jnp.full(m_s.shape, NEG, F32)
        l_s[...] = jnp.zeros(l_s.shape, F32)
        acc_s[...] = jnp.zeros(acc_s.shape, F32)
        kpad_s[...] = jnp.zeros(kpad_s.shape, F32)
        vpad_s[...] = jnp.zeros(vpad_s.shape, F32)
        kpad_s[0:R, :] = kn_ref[0]
        vpad_s[0:R, :] = vn_ref[0]
        attend([kpad_s[...].astype(BF16)], [vpad_s[...].astype(BF16)], [mn_ref[0]])

    attend([r[0].reshape(NL, A_DH).astype(BF16) for r in k_refs], [r[0].reshape(NL, A_DH).astype(BF16) for r in v_refs],
           [mp_ref[0][:, i * PAGE_SIZE:(i + 1) * PAGE_SIZE] for i in range(SMP_PG)])

    @pl.when(p == pl.num_programs(1) - 1)
    def _fin():
        o_ref[0] = acc_s[...] / jnp.sum(l_s[...], axis=1, keepdims=True)


def _smp_attn2(page_table, q32, kn32, vn32, mn, mp, ck, cv, n_tok):
    B, n_pages = page_table.shape
    R = n_tok * A_HEADS
    NL = PAGE_SIZE * A_HEADS
    per_b = lambda shape: pl.BlockSpec(shape, lambda b, p, pt: (b, 0, 0))
    page = lambda i: pl.BlockSpec((1, PAGE_SIZE, A_HEADS, A_DH), lambda b, p, pt: (pt[b, p * SMP_PG + i], 0, 0, 0))
    gs = pltpu.PrefetchScalarGridSpec(
        num_scalar_prefetch=1,
        grid=(B, n_pages // SMP_PG),
        in_specs=[per_b((1, R, A_DH)), per_b((1, R, A_DH)), per_b((1, R, A_DH)), per_b((1, SMP_ROWS, LANES)),
                  pl.BlockSpec((1, SMP_ROWS, SMP_PG * PAGE_SIZE), lambda b, p, pt: (b, 0, p))]
                 + [page(i) for i in range(SMP_PG)] + [page(i) for i in range(SMP_PG)],
        out_specs=per_b((1, R, A_DH)),
        scratch_shapes=[pltpu.VMEM((NL, A_DH), F32), pltpu.VMEM((NL, A_DH), F32),
                        pltpu.VMEM((R, LANES), F32), pltpu.VMEM((R, LANES), F32), pltpu.VMEM((R, A_DH), F32)])
    return pl.pallas_call(
        functools.partial(_smp_attn2_kernel, n_tok=n_tok), grid_spec=gs,
        out_shape=jax.ShapeDtypeStruct((B, R, A_DH), F32),
        compiler_params=_cparams(("parallel", "arbitrary")),
        name="smp_attn",
    )(page_table, q32, kn32, vn32, mn, mp, *([ck] * SMP_PG), *([cv] * SMP_PG))


def _merge_kernel(hm_ref, ha_ref, wm_ref, wa_ref, gm_ref, ga_ref, o_ref):
    a = jnp.dot(hm_ref[...], wm_ref[...], preferred_element_type=F32)
    b = jnp.dot(ha_ref[...], wa_ref[...], preferred_element_type=F32)
    o_ref[...] = (_sigmoid(gm_ref[...]) * a + _sigmoid(ga_ref[...]) * b).astype(o_ref.dtype)


def _merge(hm, ha, wm, wa, p32, tm):
    m = hm.shape[0]
    tn = PROJ_TN
    return pl.pallas_call(
        _merge_kernel,
        grid=(D_MODEL // tn, m // tm),
        in_specs=[pl.BlockSpec((tm, M_WIDTH), lambda j, i: (i, 0)), pl.BlockSpec((tm, A_WIDTH), lambda j, i: (i, 0)),
                  pl.BlockSpec((M_WIDTH, tn), lambda j, i: (0, j)), pl.BlockSpec((A_WIDTH, tn), lambda j, i: (0, j)),
                  pl.BlockSpec((tm, tn), lambda j, i: (i, C_GM // tn + j)),
                  pl.BlockSpec((tm, tn), lambda j, i: (i, C_GA // tn + j))],
        out_specs=pl.BlockSpec((tm, tn), lambda j, i: (i, j)),
        out_shape=jax.ShapeDtypeStruct((m, D_MODEL), BF16),
        compiler_params=_cparams(("parallel", "parallel")),
        name="merge",
    )(hm, ha, wm, wa, p32, p32)


def _outproj_kernel(mg_ref, w_ref, x_ref, o_ref):
    o_ref[...] = x_ref[...] + jnp.dot(mg_ref[...], w_ref[...], preferred_element_type=F32)


def _outproj(mg, w, x, tm):
    m = mg.shape[0]
    tn = PROJ_TN
    return pl.pallas_call(
        _outproj_kernel,
        grid=(D_MODEL // tn, m // tm),
        in_specs=[pl.BlockSpec((tm, D_MODEL), lambda j, i: (i, 0)), pl.BlockSpec((D_MODEL, tn), lambda j, i: (0, j)),
                  pl.BlockSpec((tm, tn), lambda j, i: (i, j))],
        out_specs=pl.BlockSpec((tm, tn), lambda j, i: (i, j)),
        out_shape=jax.ShapeDtypeStruct((m, D_MODEL), F32),
        compiler_params=_cparams(("parallel", "parallel")),
        name="outproj",
    )(mg, w, x)


MOE_TM = 256
MOE_NBUF = 3
MOE_RG = 8


def _router_kernel(x_ref, g_ref, wr_ref, br_ref, xn_ref, r_ref):
    x = x_ref[...]
    y = (x * lax.rsqrt(jnp.mean(x * x, axis=-1, keepdims=True) + RMS_EPS)) * g_ref[...]
    xn_ref[...] = y
    lg = jnp.dot(y.astype(BF16), wr_ref[...], preferred_element_type=F32) + br_ref[...]
    lane = lax.broadcasted_iota(I32, lg.shape, 1).astype(F32)
    far = float(LANES)
    gmask = lane < N_GROUPS
    gl = jnp.where(gmask, lg, NEG)
    mg = jnp.max(gl, axis=1, keepdims=True)
    p_g = 1.0 / jnp.sum(jnp.where(gmask, jnp.exp(gl - mg), 0.0), axis=1, keepdims=True)
    g_sel = jnp.min(jnp.where(gmask & (gl == mg), lane, far), axis=1, keepdims=True)
    e_lo = N_GROUPS + g_sel * EXP_PER_GROUP
    emask = (lane >= e_lo) & (lane < e_lo + EXP_PER_GROUP)
    el = jnp.where(emask, lg, NEG)
    me = jnp.max(el, axis=1, keepdims=True)
    pe = jnp.where(emask, jnp.exp(el - me), 0.0)
    probs = pe / jnp.sum(pe, axis=1, keepdims=True)
    p1 = jnp.max(probs, axis=1, keepdims=True)
    i1 = jnp.min(jnp.where(emask & (probs == p1), lane, far), axis=1, keepdims=True)
    probs2 = jnp.where(lane == i1, -1.0, probs)
    p2 = jnp.max(probs2, axis=1, keepdims=True)
    i2 = jnp.min(jnp.where(emask & (probs2 == p2), lane, far), axis=1, keepdims=True)
    tot = p1 + p2
    vals = [i1 - N_GROUPS, i2 - N_GROUPS, p_g * (p1 / tot), p_g * (p2 / tot)]
    out = jnp.zeros(lg.shape, F32)
    for c, v in enumerate(vals):
        out = jnp.where(lane == c, v, out)
    r_ref[...] = out


def _router(x1, g, wr, br, tm):
    m, d = x1.shape
    return pl.pallas_call(
        _router_kernel,
        grid=(m // tm,),
        in_specs=[pl.BlockSpec((tm, d), lambda i: (i, 0)), pl.BlockSpec((1, d), lambda i: (0, 0)),
                  pl.BlockSpec((d, LANES), lambda i: (0, 0)), pl.BlockSpec((1, LANES), lambda i: (0, 0))],
        out_specs=[pl.BlockSpec((tm, d), lambda i: (i, 0)), pl.BlockSpec((tm, LANES), lambda i: (i, 0))],
        out_shape=[jax.ShapeDtypeStruct((m, d), F32), jax.ShapeDtypeStruct((m, LANES), F32)],
        compiler_params=_cparams(("parallel",)),
        name="router",
    )(x1, g.reshape(1, d), wr, br)


def _row_copy(src_hbm, row, dst, r, sem):
    return pltpu.make_async_copy(src_hbm.at[pl.ds(row, 1), :], dst.at[pl.ds(r, 1), :], sem)


def _expert_kernel(be_ref, na_ref, ng_ref, src_ref, x_hbm, wg_ref, wu_ref, wd_ref, o_ref, xbuf, sem):
    blk = pl.program_id(0)
    slot = blk % MOE_NBUF
    ahead = MOE_NBUF - 1

    def gather(b):
        s = b % MOE_NBUF

        def start(g, _):
            for i in range(MOE_RG):
                r = g * MOE_RG + i
                _row_copy(x_hbm, src_ref[b * MOE_TM + r], xbuf.at[s], r, sem.at[s]).start()
            return 0

        lax.fori_loop(0, ng_ref[b], start, 0)

    @pl.when(blk == 0)
    def _prime():
        xbuf[...] = jnp.zeros(xbuf.shape, F32)
        for b in range(ahead):
            @pl.when(b < na_ref[0])
            def _():
                gather(b)

    @pl.when(blk + ahead < na_ref[0])
    def _prefetch():
        gather(blk + ahead)

    @pl.when(blk < na_ref[0])
    def _active():
        def wait(g, _):
            for i in range(MOE_RG):
                _row_copy(x_hbm, 0, xbuf.at[slot], g * MOE_RG + i, sem.at[slot]).wait()
            return 0

        lax.fori_loop(0, ng_ref[blk], wait, 0)
        x = xbuf[slot].astype(BF16)
        hg = jnp.dot(x, wg_ref[0].astype(BF16), preferred_element_type=F32)
        hu = jnp.dot(x, wu_ref[0].astype(BF16), preferred_element_type=F32)
        h = (hg * _sigmoid(hg)) * hu
        o_ref[...] = jnp.dot(h.astype(BF16), wd_ref[0].astype(BF16), preferred_element_type=F32)

    @pl.when(blk >= na_ref[0])
    def _idle():
        o_ref[...] = jnp.zeros(o_ref.shape, F32)


def _experts(blk_exp, n_act, blk_groups, src, xn2, w_gate, w_up, w_down):
    npad = src.shape[0]
    d = xn2.shape[1]
    gs = pltpu.PrefetchScalarGridSpec(
        num_scalar_prefetch=4,
        grid=(npad // MOE_TM,),
        in_specs=[pl.BlockSpec(memory_space=pl.ANY),
                  pl.BlockSpec((1, d, D_EXPERT), lambda b, be, na, ng, sr: (be[b], 0, 0)),
                  pl.BlockSpec((1, d, D_EXPERT), lambda b, be, na, ng, sr: (be[b], 0, 0)),
                  pl.BlockSpec((1, D_EXPERT, d), lambda b, be, na, ng, sr: (be[b], 0, 0))],
        out_specs=pl.BlockSpec((MOE_TM, d), lambda b, be, na, ng, sr: (b, 0)),
        scratch_shapes=[pltpu.VMEM((MOE_NBUF, MOE_TM, d), F32), pltpu.SemaphoreType.DMA((MOE_NBUF,))])
    return pl.pallas_call(
        _expert_kernel, grid_spec=gs,
        out_shape=jax.ShapeDtypeStruct((npad, d), F32),
        compiler_params=_cparams(("arbitrary",)),
        name="experts",
    )(blk_exp, n_act, blk_groups, src, xn2, w_gate, w_up, w_down)


def _combine_kernel(pos_ref, ys_hbm, x1_ref, r_ref, g_ref, o_ref, buf, sem, *, TC, row0):
    i = pl.program_id(0)
    slot = i % 2

    def gather(step, s):
        base = (row0 + step * TC) * TOP_E

        def start(r, _):
            for e in range(TOP_E):
                _row_copy(ys_hbm, pos_ref[base + r * TOP_E + e], buf.at[s, e], r, sem.at[s]).start()
            return 0

        lax.fori_loop(0, TC, start, 0, unroll=8)

    @pl.when(i == 0)
    def _prime():
        gather(0, 0)

    @pl.when(i + 1 < pl.num_programs(0))
    def _prefetch():
        gather(i + 1, 1 - slot)

    def wait(r, _):
        for e in range(TOP_E):
            _row_copy(ys_hbm, 0, buf.at[slot, e], r, sem.at[slot]).wait()
        return 0

    lax.fori_loop(0, TC, wait, 0, unroll=8)
    gates = r_ref[:, TOP_E:2 * TOP_E]
    x = x1_ref[...]
    for e in range(TOP_E):
        x = x + gates[:, e:e + 1] * buf[slot, e]
    y = x * lax.rsqrt(jnp.mean(x * x, axis=-1, keepdims=True) + RMS_EPS)
    o_ref[...] = y * g_ref[...]


def _combine(pos, ys, x1, r, g, row0, n, TC):
    d = x1.shape[1]
    gs = pltpu.PrefetchScalarGridSpec(
        num_scalar_prefetch=1,
        grid=(n // TC,),
        in_specs=[pl.BlockSpec(memory_space=pl.ANY),
                  pl.BlockSpec((TC, d), lambda i, ps: (row0 // TC + i, 0)),
                  pl.BlockSpec((TC, LANES), lambda i, ps: (row0 // TC + i, 0)),
                  pl.BlockSpec((1, d), lambda i, ps: (0, 0))],
        out_specs=pl.BlockSpec((TC, d), lambda i, ps: (i, 0)),
        scratch_shapes=[pltpu.VMEM((2, TOP_E, TC, d), F32), pltpu.SemaphoreType.DMA((2,))])
    return pl.pallas_call(
        functools.partial(_combine_kernel, TC=TC, row0=row0), grid_spec=gs,
        out_shape=jax.ShapeDtypeStruct((n, d), F32),
        compiler_params=_cparams(("arbitrary",)),
        name="combine",
    )(pos, ys, x1, r, g.reshape(1, d))


def _route_tables(r, npad):
    nt = r.shape[0]
    ef = r[:, 0:TOP_E].astype(I32).reshape(-1)
    onehot = (ef[:, None] == jnp.arange(N_EXPERTS, dtype=I32)[None, :]).astype(I32)
    csum = jnp.cumsum(onehot, axis=0)
    rank = jnp.sum(onehot * csum, axis=1) - 1
    cnt = csum[-1]
    nblk = (cnt + MOE_TM - 1) // MOE_TM
    blk_end = jnp.cumsum(nblk)
    blk_start = blk_end - nblk
    pos = blk_start[ef] * MOE_TM + rank
    n_act = blk_end[-1:]
    ball = jnp.arange(npad // MOE_TM, dtype=I32)
    b = jnp.minimum(ball, n_act[0] - 1)
    blk_exp = jnp.minimum(jnp.sum((blk_end[None, :] <= b[:, None]).astype(I32), axis=1), N_EXPERTS - 1)
    rows = jnp.clip(cnt[blk_exp] - (ball - blk_start[blk_exp]) * MOE_TM, 0, MOE_TM)
    blk_groups = jnp.where(ball < n_act[0], (rows + MOE_RG - 1) // MOE_RG, 0).astype(I32)
    src = jnp.zeros((npad,), I32).at[pos].set(jnp.arange(nt * TOP_E, dtype=I32) // TOP_E)
    return blk_exp, n_act.astype(I32), blk_groups, src, pos.astype(I32)


def kernel(x_prompt, x_sample, cache_k, cache_v, cache_kidx, state_conv, state_C, state_n, state_m, page_table,
           g_attn, w_in, b_gates_m, conv_w, conv_b, m_norm_w, w_proj_m, w_proj_a, w_out, g_ffn,
           w_rg, b_rg, w_re, b_re, w_gate, w_up, w_down, g_final):
    assert x_prompt.shape[0] == 1 and g_attn.shape[0] == 1
    l = 0
    Tp = x_prompt.shape[1]
    Bs, Ts = x_sample.shape[:2]
    Ns = Bs * SMP_ROWS
    NT = Tp + Ns
    TM_BIG = 1408
    TM_ROW = 768
    assert NT % TM_BIG == 0 and NT % TM_ROW == 0 and Ts >= CONV_W - 1 and Ts <= SMP_ROWS
    P = page_table.shape[1] * PAGE_SIZE
    pad_rows = lambda a: jnp.pad(a, ((0, 0), (0, SMP_ROWS - Ts), (0, 0)))

    x_all = jnp.concatenate([x_prompt[0], pad_rows(x_sample).reshape(Ns, D_MODEL)], axis=0)
    xn = _rmsnorm(x_all, g_attn[l], BF16, TM_ROW)
    p32, p16 = _inproj(xn, _prep_w_in(w_in[l]), TM_BIG)

    gate_cols = slice(C_TAIL + T_IM, C_TAIL + T_IM + 2 * M_HEADS)
    ps8 = p32[Tp:].reshape(Bs, SMP_ROWS, D_CAT)
    ps3 = ps8[:, :Ts]

    zero = lambda *s: jnp.zeros(s, F32)
    hm_p, C_p, n_p, m_p = _mlstm(p32[None], p32[:Tp, gate_cols].T[None], zero(1, SUBLANES, 2 * M_WIDTH),
                                 conv_w[l], conv_b[l], b_gates_m[l], m_norm_w[l],
                                 zero(1, M_HEADS, M_DH, M_DH), zero(1, M_HEADS, M_DH), zero(1, M_HEADS),
                                 T=Tp, L=256, RIN=256, valid=256)
    grow_s = jnp.pad(jnp.swapaxes(ps3[:, :, gate_cols], 1, 2), ((0, 0), (0, 0), (0, LANES - Ts)))
    cb_s = jnp.pad(state_conv[l], ((0, 0), (SUBLANES - (CONV_W - 1), 0), (0, 0)))
    hm_s, C_s, n_s, m_s = _mlstm(ps8, grow_s, cb_s, conv_w[l], conv_b[l], b_gates_m[l], m_norm_w[l],
                                 state_C[l], state_n[l], state_m[l], T=SUBLANES, L=LANES, RIN=SUBLANES, valid=Ts)

    ha_p = _dsa_prompt(p32, p16, Tp, min(TOPK_MAX, Tp // 4))
    qs = ps3[:, :, C_QI:C_QI + IDX_HEADS * LANES].reshape(Bs, Ts, IDX_HEADS, LANES)[..., :IDX_DIM].reshape(Bs, Ts * IDX_HEADS, IDX_DIM)
    ws = ps3[:, :, C_TAIL + T_WI:C_TAIL + T_WI + IDX_HEADS].reshape(Bs, Ts * IDX_HEADS, 1)
    kin_t = jnp.pad(jnp.swapaxes(ps3[:, :, C_TAIL + T_KI:C_TAIL + T_KI + IDX_DIM], 1, 2), ((0, 0), (0, 0), (0, LANES - Ts)))
    sc = _smp_scores2(page_table, qs, ws, jnp.swapaxes(cache_kidx[l], 1, 2))
    mp, mn = _smp_select2(sc.reshape(Bs * SMP_ROWS, P), qs, ws, kin_t, Ts, min(TOPK_MAX, (P + Ts) // 4))
    rows_th = lambda c0: ps3[:, :, c0:c0 + A_WIDTH].reshape(Bs, Ts * A_HEADS, A_DH)
    ha_s = _smp_attn2(page_table, rows_th(C_QA), rows_th(C_KA), rows_th(C_VA), mn.reshape(Bs, SMP_ROWS, LANES),
                      mp.reshape(Bs, SMP_ROWS, P), cache_k[l], cache_v[l], Ts)

    hm_all = jnp.concatenate([hm_p[0], hm_s.reshape(Ns, M_WIDTH)], axis=0)
    ha_all = jnp.concatenate([ha_p, pad_rows(ha_s.reshape(Bs, Ts, A_WIDTH)).reshape(Ns, A_WIDTH).astype(BF16)], axis=0)
    merged = _merge(hm_all, ha_all, w_proj_m[l].astype(BF16), w_proj_a[l].astype(BF16), p32, TM_BIG)
    x1 = _outproj(merged, w_out[l].astype(BF16), x_all, TM_BIG)

    wr = jnp.pad(jnp.concatenate([w_rg[l], w_re[l]], axis=1), ((0, 0), (0, LANES - N_GROUPS - N_EXPERTS))).astype(BF16)
    br = jnp.pad(jnp.concatenate([b_rg[l], b_re[l]]), (0, LANES - N_GROUPS - N_EXPERTS)).reshape(1, LANES)
    xn2, r = _router(x1, g_ffn[l], wr, br, TM_ROW)
    npad = NT * TOP_E + N_EXPERTS * MOE_TM
    blk_exp, n_act, blk_groups, src, pos = _route_tables(r, npad)
    ys = _experts(blk_exp, n_act, blk_groups, src, xn2, w_gate[l], w_up[l], w_down[l])
    y_p = _combine(pos, ys, x1, r, g_final, 0, Tp, 256)
    y_s = _combine(pos, ys, x1, r, g_final, Tp, Ns, 256)

    st = lambda a, shape: a.reshape((1,) + shape)
    pp = p32[:Tp]
    return (y_p[None], y_s.reshape(Bs, SMP_ROWS, D_MODEL)[:, :Ts],
            st(pp[:, C_KA:C_KA + A_WIDTH], (1, Tp, A_HEADS, A_DH)), st(pp[:, C_VA:C_VA + A_WIDTH], (1, Tp, A_HEADS, A_DH)),
            st(pp[:, C_TAIL + T_KI:C_TAIL + T_KI + IDX_DIM], (1, Tp, IDX_DIM)),
            st(pp[Tp - (CONV_W - 1):, 0:2 * M_WIDTH], (1, CONV_W - 1, 2 * M_WIDTH)),
            st(C_p, (1, M_HEADS, M_DH, M_DH)), st(n_p, (1, M_HEADS, M_DH)), st(m_p[:, :, 0, 0], (1, M_HEADS)),
            st(ps3[:, :, C_KA:C_KA + A_WIDTH], (Bs, Ts, A_HEADS, A_DH)), st(ps3[:, :, C_VA:C_VA + A_WIDTH], (Bs, Ts, A_HEADS, A_DH)),
            st(ps3[:, :, C_TAIL + T_KI:C_TAIL + T_KI + IDX_DIM], (Bs, Ts, IDX_DIM)),
            st(ps3[:, Ts - (CONV_W - 1):, 0:2 * M_WIDTH], (Bs, CONV_W - 1, 2 * M_WIDTH)),
            st(C_s, (Bs, M_HEADS, M_DH, M_DH)), st(n_s, (Bs, M_HEADS, M_DH)), st(m_s[:, :, 0, 0], (Bs, M_HEADS)))
```
